```python
import jax, jax.numpy as jnp
from jax import lax
import numpy as np

D_MODEL = 1024
BATCH = 8
SEQ = 4096
DEPTH = 2
DEC_BATCH = 16
DEC_SEQ = 64
PAST_LEN = 4096

CHUNK = 64
N_A_LAYERS = DEPTH // 2
N_B_LAYERS = DEPTH - N_A_LAYERS
A_HEADS = 4
A_DK = D_MODEL // 2 // A_HEADS
A_DV = D_MODEL // A_HEADS
A_QK = A_HEADS * A_DK
A_VW = A_HEADS * A_DV
A_GATE_RANK = 16
A_GATE_TAU = 16.0
A_IN = 2 * A_QK + 2 * A_VW + A_GATE_RANK
B_HEADS = 16
B_HEAD_DIM = D_MODEL // B_HEADS
Q_BLOCK = 128
D_FF = 2816
EPS = 1e-6
NEG_INF = -1e30

kernel_name = 'gla_fox_yoco_macaron_stream_step'


def rms_norm(x, g):
    xf = x.astype(jnp.float32)
    y = xf * lax.rsqrt(jnp.mean(xf * xf, axis=-1, keepdims=True) + EPS)
    return (y * g.astype(jnp.float32)).astype(x.dtype)


def swiglu_ffn(x, w_gu, w_down):
    g, u = jnp.split(x @ w_gu, 2, axis=-1)
    return (jax.nn.silu(g) * u) @ w_down


def gla_mixer(xn, w_in, w_g2, b_g, g_out, w_o, s0, chunk):
    bsz, t, _ = xn.shape
    n_c = t // chunk
    proj = xn @ w_in
    q = proj[..., :A_QK]
    k = proj[..., A_QK:2 * A_QK]
    v = proj[..., 2 * A_QK:2 * A_QK + A_VW]
    r = proj[..., 2 * A_QK + A_VW:2 * A_QK + 2 * A_VW]
    gl = proj[..., 2 * A_QK + 2 * A_VW:]
    log_a = jax.nn.log_sigmoid((gl @ w_g2 + b_g).astype(jnp.float32)) / A_GATE_TAU

    def to_chunks(z, d):
        return z.reshape(bsz, n_c, chunk, A_HEADS, d).transpose(0, 3, 1, 2, 4).astype(jnp.float32)

    qc = to_chunks(q, A_DK) * (A_DK ** -0.5)
    kc = to_chunks(k, A_DK)
    vc = to_chunks(v, A_DV)
    b = jnp.cumsum(to_chunks(log_a, A_DK), axis=3)
    b_last = b[:, :, :, -1:, :]
    q_e = qc * jnp.exp(b)
    k_e = kc * jnp.exp(-b)
    causal = jnp.tril(jnp.ones((chunk, chunk), dtype=bool))
    att = jnp.where(causal, jnp.einsum('bhcld,bhcmd->bhclm', q_e, k_e), 0.0)
    o_intra = jnp.einsum('bhclm,bhcme->bhcle', att, vc)
    decay_c = jnp.exp(b_last[:, :, :, 0, :])
    k_dec = kc * jnp.exp(b_last - b)

    def chunk_step(s, inp):
        dec, kd, vv, qq = inp
        o_inter = jnp.einsum('bhld,bhde->bhle', qq, s)
        s_new = dec[..., None] * s + jnp.einsum('bhld,bhle->bhde', kd, vv)
        return s_new, o_inter

    xs = (jnp.moveaxis(decay_c, 2, 0), jnp.moveaxis(k_dec, 2, 0), jnp.moveaxis(vc, 2, 0), jnp.moveaxis(q_e, 2, 0))
    s_fin, o_inter = lax.scan(chunk_step, s0.astype(jnp.float32), xs)
    o = o_intra + jnp.moveaxis(o_inter, 0, 2)
    o = o.transpose(0, 2, 3, 1, 4).reshape(bsz, t, A_HEADS, A_DV)
    o = rms_norm(o, g_out).astype(xn.dtype).reshape(bsz, t, A_VW) * jax.nn.silu(r)
    return o @ w_o, s_fin.astype(s0.dtype)


def shared_kv(h, kv_norm, w_kvf, b_f, g_k):
    bsz, t, _ = h.shape
    kvf = rms_norm(h, kv_norm) @ w_kvf
    k = rms_norm(kvf[..., :D_MODEL].reshape(bsz, t, B_HEADS, B_HEAD_DIM), g_k)
    v = kvf[..., D_MODEL:2 * D_MODEL].reshape(bsz, t, B_HEADS, B_HEAD_DIM)
    logf = jax.nn.log_sigmoid((kvf[..., 2 * D_MODEL:] + b_f).astype(jnp.float32)).astype(h.dtype)
    return k, v, logf


def fox_block(q_blk, cq_blk, qpos_blk, k, v, ck, kpos):
    s = jnp.einsum('bqhd,bkhd->bhqk', q_blk, k).astype(jnp.float32) * (B_HEAD_DIM ** -0.5)
    s = s + jnp.transpose(cq_blk, (0, 2, 1))[..., None] - jnp.transpose(ck, (0, 2, 1))[:, :, None, :]
    s = jnp.where(kpos[None, :] <= qpos_blk[:, None], s, NEG_INF)
    p = jax.nn.softmax(s, axis=-1).astype(v.dtype)
    return jnp.einsum('bhqk,bkhd->bqhd', p, v)


def fox_mixer(xn, k, v, logf, w_qg, g_q, w_o):
    bsz, t, _ = xn.shape
    n_k = k.shape[1]
    qg = xn @ w_qg
    q = rms_norm(qg[..., :D_MODEL].reshape(bsz, t, B_HEADS, B_HEAD_DIM), g_q)
    gate = qg[..., D_MODEL:]
    c = jnp.cumsum(logf.astype(jnp.float32), axis=1)
    cq = c[:, n_k - t:]
    kpos = jnp.arange(n_k)
    qpos = jnp.arange(n_k - t, n_k)
    blk = min(Q_BLOCK, t)
    n_blk = t // blk
    qb = q.reshape(bsz, n_blk, blk, B_HEADS, B_HEAD_DIM).transpose(1, 0, 2, 3, 4)
    cqb = cq.reshape(bsz, n_blk, blk, B_HEADS).transpose(1, 0, 2, 3)
    qposb = qpos.reshape(n_blk, blk)
    o = lax.map(lambda a: fox_block(a[0], a[1], a[2], k, v, c, kpos), (qb, cqb, qposb))
    o = o.transpose(1, 0, 2, 3, 4).reshape(bsz, t, D_MODEL)
    o = o * jax.nn.sigmoid(gate)
    return o @ w_o


def run_trunk(x, gla_s0, past_k, past_v, past_logf, chunk, ffn_norm, w_ffn_gu, w_ffn_down, mix_norm,
              a_w_in, a_w_g2, a_b_g, a_g_out, a_w_o, kv_norm, w_kvf, b_f, g_k, b_w_qg, b_g_q, b_w_o):
    h = x
    new_gla = []
    keys = None
    k_new = v_new = lf_new = None
    for layer in range(DEPTH):
        h = h + 0.5 * swiglu_ffn(rms_norm(h, ffn_norm[layer, 0]), w_ffn_gu[layer, 0], w_ffn_down[layer, 0])
        hn = rms_norm(h, mix_norm[layer])
        if layer < N_A_LAYERS:
            y, s_new = gla_mixer(hn, a_w_in[layer], a_w_g2[layer], a_b_g[layer], a_g_out[layer], a_w_o[layer],
                                 gla_s0[:, layer], chunk)
            new_gla.append(s_new)
        else:
            j = layer - N_A_LAYERS
            y = fox_mixer(hn, keys[0], keys[1], keys[2], b_w_qg[j], b_g_q[j], b_w_o[j])
        h = h + y
        h = h + 0.5 * swiglu_ffn(rms_norm(h, ffn_norm[layer, 1]), w_ffn_gu[layer, 1], w_ffn_down[layer, 1])
        if layer == N_A_LAYERS - 1:
            k_new, v_new, lf_new = shared_kv(h, kv_norm, w_kvf, b_f, g_k)
            if past_k is None:
                keys = (k_new, v_new, lf_new)
            else:
                keys = (jnp.concatenate([past_k, k_new], axis=1),
                        jnp.concatenate([past_v, v_new], axis=1),
                        jnp.concatenate([past_logf, lf_new], axis=1))
    return h, jnp.stack(new_gla, axis=1), k_new, v_new, lf_new


def setup_inputs(seed: int = 0) -> dict:
    key = jax.random.key(seed)
    ks = jax.random.split(key, 24)

    def nrm(k, shape, scale):
        return jax.random.normal(k, shape, jnp.float32) * scale

    def gain(k, shape):
        return 1.0 + 0.05 * jax.random.normal(k, shape, jnp.float32)

    return {
        'x_prompt': nrm(ks[0], (BATCH, SEQ, D_MODEL), 1.0),
        'x_sample': nrm(ks[1], (DEC_BATCH, DEC_SEQ, D_MODEL), 1.0),
        'state_gla': nrm(ks[2], (DEC_BATCH, N_A_LAYERS, A_HEADS, A_DK, A_DV), 1.0),
        'cache_k': nrm(ks[3], (DEC_BATCH, PAST_LEN, B_HEADS, B_HEAD_DIM), 1.0),
        'cache_v': nrm(ks[4], (DEC_BATCH, PAST_LEN, B_HEADS, B_HEAD_DIM), 1.0),
        'cache_logf': jax.nn.log_sigmoid(jax.random.uniform(ks[5], (DEC_BATCH, PAST_LEN, B_HEADS), jnp.float32, 1.0, 4.0)),
        'ffn_norm': gain(ks[6], (DEPTH, 2, D_MODEL)),
        'w_ffn_gu': nrm(ks[7], (DEPTH, 2, D_MODEL, 2 * D_FF), D_MODEL ** -0.5),
        'w_ffn_down': nrm(ks[8], (DEPTH, 2, D_FF, D_MODEL), D_FF ** -0.5),
        'mix_norm': gain(ks[9], (DEPTH, D_MODEL)),
        'a_w_in': nrm(ks[10], (N_A_LAYERS, D_MODEL, A_IN), D_MODEL ** -0.5),
        'a_w_g2': nrm(ks[11], (N_A_LAYERS, A_GATE_RANK, A_QK), A_GATE_RANK ** -0.5),
        'a_b_g': nrm(ks[12], (N_A_LAYERS, A_QK), 0.1),
        'a_g_out': gain(ks[13], (N_A_LAYERS, A_DV)),
        'a_w_o': nrm(ks[14], (N_A_LAYERS, A_VW, D_MODEL), A_VW ** -0.5),
        'kv_norm': gain(ks[15], (D_MODEL,)),
        'w_kvf': nrm(ks[16], (D_MODEL, 2 * D_MODEL + B_HEADS), D_MODEL ** -0.5),
        'b_f': jax.random.uniform(ks[17], (B_HEADS,), jnp.float32, 1.0, 4.0),
        'g_k': gain(ks[18], (B_HEAD_DIM,)),
        'b_w_qg': nrm(ks[19], (N_B_LAYERS, D_MODEL, 2 * D_MODEL), D_MODEL ** -0.5),
        'b_g_q': gain(ks[20], (N_B_LAYERS, B_HEAD_DIM)),
        'b_w_o': nrm(ks[21], (N_B_LAYERS, D_MODEL, D_MODEL), D_MODEL ** -0.5),
    }


def reference(x_prompt, x_sample, state_gla, cache_k, cache_v, cache_logf, ffn_norm, w_ffn_gu, w_ffn_down,
              mix_norm, a_w_in, a_w_g2, a_b_g, a_g_out, a_w_o, kv_norm, w_kvf, b_f, g_k, b_w_qg, b_g_q, b_w_o):
    s0_prompt = jnp.zeros((x_prompt.shape[0], N_A_LAYERS, A_HEADS, A_DK, A_DV), x_prompt.dtype)
    y_prompt, gla_prompt, k_prompt, v_prompt, lf_prompt = run_trunk(
        x_prompt, s0_prompt, None, None, None, CHUNK, ffn_norm, w_ffn_gu, w_ffn_down, mix_norm,
        a_w_in, a_w_g2, a_b_g, a_g_out, a_w_o, kv_norm, w_kvf, b_f, g_k, b_w_qg, b_g_q, b_w_o)
    y_sample, gla_sample, k_sample, v_sample, lf_sample = run_trunk(
        x_sample, state_gla, cache_k, cache_v, cache_logf, x_sample.shape[1], ffn_norm, w_ffn_gu, w_ffn_down,
        mix_norm, a_w_in, a_w_g2, a_b_g, a_g_out, a_w_o, kv_norm, w_kvf, b_f, g_k, b_w_qg, b_g_q, b_w_o)
    return (y_prompt, y_sample, gla_prompt, gla_sample, k_prompt, v_prompt, lf_prompt, k_sample, v_sample, lf_sample)
```

```python
import functools

import jax
import jax.numpy as jnp
import numpy as np
from jax import lax
from jax.experimental import pallas as pl
from jax.experimental.pallas import tpu as pltpu

F32 = jnp.float32
BF16 = jnp.bfloat16

EPS = 1e-6
NEG_INF = -1e30

LANES = 128
V7X_VMEM_BYTES = 64 * 1024 * 1024
V7X_SCOPED_VMEM_CAP = 60000 * 1024

GLA_HEADS = 4
GLA_GATE_TAU = 16.0
FOX_HEADS = 16
FOX_HEAD_DIM = 64
FOX_PAIRS = FOX_HEADS // 2
BIAS_PARTS = 3
BIAS_LANES_PER_HEAD = 2 * BIAS_PARTS

NT_DIMS = (((1,), (1,)), ((), ()))
TN_DIMS = (((0,), (0,)), ((), ()))


def _cparams(semantics, vmem_bytes):
    limit = int(min(max(vmem_bytes, 16 * 1024 * 1024), V7X_SCOPED_VMEM_CAP))
    return pltpu.CompilerParams(dimension_semantics=semantics, vmem_limit_bytes=limit)


def _dot(a, b):
    return jnp.dot(a, b, preferred_element_type=F32)


def _rms(x, g):
    ms = jnp.mean(x * x, axis=-1, keepdims=True)
    return x * lax.rsqrt(ms + EPS) * g


def _split2(x):
    hi = x.astype(BF16)
    lo = (x - hi.astype(F32)).astype(BF16)
    return hi, lo


def _split3(x):
    p1 = x.astype(BF16)
    r1 = x - p1.astype(F32)
    p2 = r1.astype(BF16)
    p3 = (r1 - p2.astype(F32)).astype(BF16)
    return p1, p2, p3


def _log_sigmoid(x):
    return jnp.minimum(x, 0.0) - jnp.log(1.0 + jnp.exp(-jnp.abs(x)))


def _head_norm(x, gsum_ref, gexp_ref, gain, head_dim):
    hi, lo = _split2(x * x)
    ssq = _dot(hi, gsum_ref[...]) + _dot(lo, gsum_ref[...])
    inv = lax.rsqrt(ssq * (1.0 / head_dim) + EPS)
    ihi, ilo = _split2(inv)
    inv_full = _dot(ihi, gexp_ref[...]) + _dot(ilo, gexp_ref[...])
    return x * inv_full * gain


def _ffn_body(x_ref, g_ref, wg_ref, wu_ref, wd_ref, o_ref, xn_ref):
    j = pl.program_id(1)

    @pl.when(j == 0)
    def _():
        xn_ref[...] = _rms(x_ref[...], g_ref[...]).astype(BF16)

    xn = xn_ref[...]
    g = _dot(xn, wg_ref[...])
    u = _dot(xn, wu_ref[...])
    h = (g * jax.nn.sigmoid(g) * u).astype(BF16)
    d = _dot(h, wd_ref[...])

    @pl.when(j == 0)
    def _():
        o_ref[...] = d

    @pl.when(j > 0)
    def _():
        o_ref[...] += d

    @pl.when(j == pl.num_programs(1) - 1)
    def _():
        o_ref[...] = x_ref[...] + 0.5 * o_ref[...]


def _ffn(x, g, w_gu, w_down, *, tm, tf):
    n, d = x.shape
    d_ff = w_down.shape[0]
    tm = min(tm, n)
    nf = d_ff // tf
    assert n % tm == 0 and d_ff % tf == 0
    vmem = 4 * tm * d * 4 + tm * d * 2 + 2 * 3 * d * tf * 2 + 3 * tm * tf * 4 + tm * d * 4
    return pl.pallas_call(
        _ffn_body,
        out_shape=jax.ShapeDtypeStruct((n, d), F32),
        grid=(n // tm, nf),
        in_specs=[
            pl.BlockSpec((tm, d), lambda i, j: (i, 0)),
            pl.BlockSpec((1, d), lambda i, j: (0, 0)),
            pl.BlockSpec((d, tf), lambda i, j: (0, j)),
            pl.BlockSpec((d, tf), lambda i, j: (0, j + nf)),
            pl.BlockSpec((tf, d), lambda i, j: (j, 0)),
        ],
        out_specs=pl.BlockSpec((tm, d), lambda i, j: (i, 0)),
        scratch_shapes=[pltpu.VMEM((tm, d), BF16)],
        compiler_params=_cparams(("parallel", "arbitrary"), vmem + (8 << 20)),
        name="ffn",
    )(x, g.reshape(1, d), w_gu, w_gu, w_down)


def _proj_res_body(h_ref, a_ref, w_ref, o_ref):
    o_ref[...] = h_ref[...] + _dot(a_ref[...], w_ref[...])


def _proj_res(h, a, w, *, tm):
    n, d = h.shape
    k = a.shape[1]
    tm = min(tm, n)
    assert n % tm == 0
    vmem = 2 * (2 * tm * d * 4 + tm * k * 2 + k * d * 2) + tm * d * 4
    return pl.pallas_call(
        _proj_res_body,
        out_shape=jax.ShapeDtypeStruct((n, d), F32),
        grid=(n // tm,),
        in_specs=[
            pl.BlockSpec((tm, d), lambda i: (i, 0)),
            pl.BlockSpec((tm, k), lambda i: (i, 0)),
            pl.BlockSpec((k, d), lambda i: (0, 0)),
        ],
        out_specs=pl.BlockSpec((tm, d), lambda i: (i, 0)),
        compiler_params=_cparams(("parallel",), vmem + (4 << 20)),
        name="proj_res",
    )(h, a, w)


def _gla_in_body(x_ref, g_ref, wq_ref, wk_ref, wv_ref, wr_ref, wgl_ref, wg2_ref, bg_ref,
                 q_ref, k_ref, v_ref, r_ref, la_ref):
    xn = _rms(x_ref[...], g_ref[...]).astype(BF16)
    q_ref[...] = _dot(xn, wq_ref[...])
    k_ref[...] = _dot(xn, wk_ref[...])
    v_ref[...] = _dot(xn, wv_ref[...])
    r_ref[...] = _dot(xn, wr_ref[...])
    gl = _dot(xn, wgl_ref[...]).astype(BF16)
    z = _dot(gl, wg2_ref[...]) + bg_ref[...]
    la_ref[...] = _log_sigmoid(z) * (1.0 / GLA_GATE_TAU)


def _gla_in(x, g, wq, wk, wv, wr, wgl, wg2, bg, *, tm):
    n, d = x.shape
    qk, vw = wq.shape[1], wv.shape[1]
    tm = min(tm, n)
    assert n % tm == 0
    row = lambda i: (i, 0)
    fix = lambda i: (0, 0)
    w_bytes = 2 * (2 * d * qk + 2 * d * vw + d * LANES + LANES * qk)
    vmem = 2 * tm * (d + 3 * qk + 2 * vw) * 4 + 2 * w_bytes + tm * d * 2
    return pl.pallas_call(
        _gla_in_body,
        out_shape=[jax.ShapeDtypeStruct((n, qk), F32), jax.ShapeDtypeStruct((n, qk), F32),
                   jax.ShapeDtypeStruct((n, vw), F32), jax.ShapeDtypeStruct((n, vw), F32),
                   jax.ShapeDtypeStruct((n, qk), F32)],
        grid=(n // tm,),
        in_specs=[
            pl.BlockSpec((tm, d), row), pl.BlockSpec((1, d), fix),
            pl.BlockSpec((d, qk), fix), pl.BlockSpec((d, qk), fix),
            pl.BlockSpec((d, vw), fix), pl.BlockSpec((d, vw), fix),
            pl.BlockSpec((d, LANES), fix), pl.BlockSpec((LANES, qk), fix), pl.BlockSpec((1, qk), fix),
        ],
        out_specs=[pl.BlockSpec((tm, qk), row), pl.BlockSpec((tm, qk), row),
                   pl.BlockSpec((tm, vw), row), pl.BlockSpec((tm, vw), row),
                   pl.BlockSpec((tm, qk), row)],
        compiler_params=_cparams(("parallel",), vmem + (8 << 20)),
        name="gla_in",
    )(x, g.reshape(1, d), wq, wk, wv, wr, wgl, wg2, bg.reshape(1, qk))


def _gla_body(q_ref, k_ref, v_ref, r_ref, la_ref, s0_ref, gout_ref, tri_ref,
              og_ref, sfin_ref, st_ref, *, chunk, n_chunks, heads, dk, dv):
    t = pl.program_id(1)

    @pl.when(t == 0)
    def _():
        st_ref[...] = s0_ref[...]

    scale = dk ** -0.5
    row = lax.broadcasted_iota(jnp.int32, (chunk, chunk), 0)
    col = lax.broadcasted_iota(jnp.int32, (chunk, chunk), 1)
    causal = col <= row
    tri = tri_ref[...]
    gout = gout_ref[...]

    for c in range(n_chunks):
        sl = slice(c * chunk, (c + 1) * chunk)
        la_hi, la_lo = _split2(la_ref[sl, :])
        b = _dot(tri, la_hi) + _dot(tri, la_lo)
        b_last = b[chunk - 1:chunk, :]
        q = q_ref[sl, :]
        k = k_ref[sl, :]
        qe = (q * scale * jnp.exp(b)).astype(BF16)
        ke = (k * jnp.exp(-b)).astype(BF16)
        kd = (k * jnp.exp(b_last - b)).astype(BF16)
        dec = jnp.exp(b_last)
        for h in range(heads):
            ks = slice(h * dk, (h + 1) * dk)
            vs = slice(h * dv, (h + 1) * dv)
            vh = v_ref[sl, vs].astype(BF16)
            att = lax.dot_general(qe[:, ks], ke[:, ks], NT_DIMS, preferred_element_type=F32)
            att = jnp.where(causal, att, 0.0).astype(BF16)
            st = st_ref[h]
            o = _dot(att, vh) + lax.dot_general(qe[:, ks], st.astype(BF16), NT_DIMS,
                                                preferred_element_type=F32)
            st_ref[h] = st * dec[:, ks] + lax.dot_general(vh, kd[:, ks], TN_DIMS,
                                                          preferred_element_type=F32)
            on = _rms(o, gout)
            rh = r_ref[sl, vs]
            og_ref[sl, vs] = (on * (rh * jax.nn.sigmoid(rh))).astype(BF16)

    @pl.when(t == pl.num_programs(1) - 1)
    def _():
        sfin_ref[...] = st_ref[...]


def _gla(q, k, v, r, la, s0t, gout, *, chunk, tb):
    bsz, t, qk = q.shape
    vw = v.shape[2]
    heads = s0t.shape[1]
    dk, dv = qk // heads, vw // heads
    tb = min(tb, t)
    assert t % tb == 0 and tb % chunk == 0
    tri = jnp.tril(jnp.ones((chunk, chunk), F32)).astype(BF16)
    blk = lambda w: pl.BlockSpec((None, tb, w), lambda b, i: (b, i, 0))
    st_spec = pl.BlockSpec((None, heads, dv, dk), lambda b, i: (b, 0, 0, 0))
    vmem = 2 * tb * (3 * qk + 2 * vw) * 4 + 2 * tb * vw * 2 + 5 * heads * dv * dk * 4
    return pl.pallas_call(
        functools.partial(_gla_body, chunk=chunk, n_chunks=tb // chunk, heads=heads, dk=dk, dv=dv),
        out_shape=[jax.ShapeDtypeStruct((bsz, t, vw), BF16),
                   jax.ShapeDtypeStruct((bsz, heads, dv, dk), F32)],
        grid=(bsz, t // tb),
        in_specs=[blk(qk), blk(qk), blk(vw), blk(vw), blk(qk), st_spec,
                  pl.BlockSpec((1, dv), lambda b, i: (0, 0)),
                  pl.BlockSpec((chunk, chunk), lambda b, i: (0, 0))],
        out_specs=[blk(vw), st_spec],
        scratch_shapes=[pltpu.VMEM((heads, dv, dk), F32)],
        compiler_params=_cparams(("parallel", "arbitrary"), vmem + (8 << 20)),
        name="gla_chunks",
    )(q, k, v, r, la, s0t, gout.reshape(1, dv), tri)


def _kv_body(x_ref, g_ref, wk_ref, wv_ref, wf_ref, bf_ref, gk_ref, gsum_ref, gexp_ref,
             k_ref, v_ref, lf_ref, k16_ref, v16_ref):
    xn = _rms(x_ref[...], g_ref[...]).astype(BF16)
    k = _head_norm(_dot(xn, wk_ref[...]), gsum_ref, gexp_ref, gk_ref[...], FOX_HEAD_DIM)
    v = _dot(xn, wv_ref[...])
    k_ref[...] = k
    v_ref[...] = v
    k16_ref[...] = k.astype(BF16)
    v16_ref[...] = v.astype(BF16)
    lf_ref[...] = _log_sigmoid(_dot(xn, wf_ref[...]) + bf_ref[...])


def _shared_kv(x, g, wk, wv, wf, bf, gk, gsum, gexp, *, tm):
    n, d = x.shape
    tm = min(tm, n)
    assert n % tm == 0
    row = lambda i: (i, 0)
    fix = lambda i: (0, 0)
    vmem = 2 * tm * d * (4 + 4 + 4 + 2 + 2) + 2 * tm * LANES * 4 + 2 * 2 * (2 * d * d + 3 * d * LANES) \
        + 6 * tm * d * 4
    return pl.pallas_call(
        _kv_body,
        out_shape=[jax.ShapeDtypeStruct((n, d), F32), jax.ShapeDtypeStruct((n, d), F32),
                   jax.ShapeDtypeStruct((n, LANES), F32),
                   jax.ShapeDtypeStruct((n, d), BF16), jax.ShapeDtypeStruct((n, d), BF16)],
        grid=(n // tm,),
        in_specs=[
            pl.BlockSpec((tm, d), row), pl.BlockSpec((1, d), fix),
            pl.BlockSpec((d, d), fix), pl.BlockSpec((d, d), fix), pl.BlockSpec((d, LANES), fix),
            pl.BlockSpec((1, LANES), fix), pl.BlockSpec((1, d), fix),
            pl.BlockSpec((d, LANES), fix), pl.BlockSpec((LANES, d), fix),
        ],
        out_specs=[pl.BlockSpec((tm, d), row), pl.BlockSpec((tm, d), row),
                   pl.BlockSpec((tm, LANES), row),
                   pl.BlockSpec((tm, d), row), pl.BlockSpec((tm, d), row)],
        compiler_params=_cparams(("parallel",), vmem),
        name="shared_kv",
    )(x, g.reshape(1, d), wk, wv, wf, bf, gk, gsum, gexp)


def _fox_q_body(x_ref, g_ref, wq_ref, wg_ref, gq_ref, gsum_ref, gexp_ref, q16_ref, sg_ref):
    xn = _rms(x_ref[...], g_ref[...]).astype(BF16)
    q = _head_norm(_dot(xn, wq_ref[...]), gsum_ref, gexp_ref, gq_ref[...], FOX_HEAD_DIM)
    q16_ref[...] = (q * (FOX_HEAD_DIM ** -0.5)).astype(BF16)
    sg_ref[...] = jax.nn.sigmoid(_dot(xn, wg_ref[...]))


def _fox_q(x, g, wq, wg, gq, gsum, gexp, *, tm):
    n, d = x.shape
    tm = min(tm, n)
    assert n % tm == 0
    row = lambda i: (i, 0)
    fix = lambda i: (0, 0)
    vmem = 2 * tm * d * (4 + 2 + 4) + 2 * 2 * (2 * d * d + 2 * d * LANES) + 6 * tm * d * 4
    return pl.pallas_call(
        _fox_q_body,
        out_shape=[jax.ShapeDtypeStruct((n, d), BF16), jax.ShapeDtypeStruct((n, d), F32)],
        grid=(n // tm,),
        in_specs=[
            pl.BlockSpec((tm, d), row), pl.BlockSpec((1, d), fix),
            pl.BlockSpec((d, d), fix), pl.BlockSpec((d, d), fix), pl.BlockSpec((1, d), fix),
            pl.BlockSpec((d, LANES), fix), pl.BlockSpec((LANES, d), fix),
        ],
        out_specs=[pl.BlockSpec((tm, d), row), pl.BlockSpec((tm, d), row)],
        compiler_params=_cparams(("parallel",), vmem),
        name="fox_q",
    )(x, g.reshape(1, d), wq, wg, gq, gsum, gexp)


def _bias_body(lf_ref, tri_ref, selq_ref, selk_ref, oneq_ref, onek_ref, qb_ref, kb_ref, carry_ref):
    @pl.when(pl.program_id(1) == 0)
    def _():
        carry_ref[...] = jnp.zeros_like(carry_ref)

    tri = tri_ref[...]
    p1, p2, p3 = _split3(lf_ref[...])
    c = _dot(tri, p1) + _dot(tri, p2) + _dot(tri, p3) + carry_ref[...]
    tb = c.shape[0]
    carry_ref[...] = c[tb - 1:tb, :]
    c1, c2, c3 = _split3(c)
    qb = _dot(c1, selq_ref[0]) + _dot(c2, selq_ref[1]) + _dot(c3, selq_ref[2]) + oneq_ref[...]
    kb = _dot(c1, selk_ref[0]) + _dot(c2, selk_ref[1]) + _dot(c3, selk_ref[2]) + onek_ref[...]
    for p in range(FOX_PAIRS):
        qb_ref[p] = qb[:, p * LANES:(p + 1) * LANES].astype(BF16)
        kb_ref[p] = kb[:, p * LANES:(p + 1) * LANES].astype(BF16)


def _bias_constants():
    selq = np.zeros((BIAS_PARTS, LANES, FOX_PAIRS * LANES), np.float32)
    selk = np.zeros((BIAS_PARTS, LANES, FOX_PAIRS * LANES), np.float32)
    oneq = np.zeros((1, FOX_PAIRS * LANES), np.float32)
    onek = np.zeros((1, FOX_PAIRS * LANES), np.float32)
    for p in range(FOX_PAIRS):
        for e in range(2):
            base = p * LANES + e * BIAS_LANES_PER_HEAD
            for part in range(BIAS_PARTS):
                selq[part, 2 * p + e, base + part] = 1.0
                onek[0, base + part] = 1.0
                selk[part, 2 * p + e, base + BIAS_PARTS + part] = -1.0
                oneq[0, base + BIAS_PARTS + part] = 1.0
    return (jnp.asarray(selq, BF16), jnp.asarray(selk, BF16), jnp.asarray(oneq), jnp.asarray(onek))


def _fox_bias(lf, *, tb):
    bsz, kp, _ = lf.shape
    assert kp % tb == 0
    selq, selk, oneq, onek = _bias_constants()
    tri = jnp.tril(jnp.ones((tb, tb), F32)).astype(BF16)
    wide = FOX_PAIRS * LANES
    out_spec = pl.BlockSpec((None, FOX_PAIRS, tb, LANES), lambda b, i: (b, 0, i, 0))
    vmem = 2 * tb * LANES * 4 + 2 * tb * tb * 2 + 4 * BIAS_PARTS * LANES * wide * 2 \
        + 4 * FOX_PAIRS * tb * LANES * 2 + 6 * tb * wide * 4
    return pl.pallas_call(
        _bias_body,
        out_shape=[jax.ShapeDtypeStruct((bsz, FOX_PAIRS, kp, LANES), BF16)] * 2,
        grid=(bsz, kp // tb),
        in_specs=[
            pl.BlockSpec((None, tb, LANES), lambda b, i: (b, i, 0)),
            pl.BlockSpec((tb, tb), lambda b, i: (0, 0)),
            pl.BlockSpec((BIAS_PARTS, LANES, wide), lambda b, i: (0, 0, 0)),
            pl.BlockSpec((BIAS_PARTS, LANES, wide), lambda b, i: (0, 0, 0)),
            pl.BlockSpec((1, wide), lambda b, i: (0, 0)),
            pl.BlockSpec((1, wide), lambda b, i: (0, 0)),
        ],
        out_specs=[out_spec, out_spec],
        scratch_shapes=[pltpu.VMEM((1, LANES), F32)],
        compiler_params=_cparams(("parallel", "arbitrary"), vmem),
        name="fox_bias",
    )(lf, tri, selq, selk, oneq, onek)


def _attn_body(q_ref, qb_ref, k_ref, kb_ref, v_ref, sg_ref, o_ref, m_ref, l_ref, acc_ref,
               *, tq, tk, off):
    i = pl.program_id(2)
    lane = lax.broadcasted_iota(jnp.int32, (tq, LANES), 1)
    q = q_ref[...]
    qb = qb_ref[...]
    zero = jnp.zeros_like(q)
    q_aug = []
    for e in range(2):
        head_lanes = (lane >> 6) == e
        bias_lanes = jnp.logical_and(lane >= e * BIAS_LANES_PER_HEAD,
                                     lane < (e + 1) * BIAS_LANES_PER_HEAD)
        q_aug.append(jnp.concatenate(
            [jnp.where(head_lanes, q, zero), jnp.where(bias_lanes, qb, zero)], axis=1))

    m_ref[...] = jnp.full(m_ref.shape, NEG_INF, F32)
    l_ref[...] = jnp.zeros(l_ref.shape, F32)
    acc_ref[...] = jnp.zeros(acc_ref.shape, F32)
    q_lo = i * tq + off

    def step(j, masked):
        ks = pl.multiple_of(j * tk, tk)
        k_aug = jnp.concatenate([k_ref[pl.ds(ks, tk), :], kb_ref[pl.ds(ks, tk), :]], axis=1)
        v = v_ref[pl.ds(ks, tk), :]
        for e in range(2):
            s = lax.dot_general(q_aug[e], k_aug, NT_DIMS, preferred_element_type=F32)
            if masked:
                qpos = q_lo + lax.broadcasted_iota(jnp.int32, (tq, tk), 0)
                kpos = ks + lax.broadcasted_iota(jnp.int32, (tq, tk), 1)
                s = jnp.where(kpos <= qpos, s, NEG_INF)
            m_prev = m_ref[e]
            m_new = jnp.maximum(m_prev, jnp.max(s, axis=1, keepdims=True))
            alpha = jnp.exp(m_prev - m_new)
            p = jnp.exp(s - m_new)
            l_ref[e] = alpha * l_ref[e] + jnp.sum(p, axis=1, keepdims=True)
            acc_ref[e] = alpha * acc_ref[e] + _dot(p.astype(BF16), v)
            m_ref[e] = m_new

    n_full = (q_lo + 1) // tk
    n_all = (q_lo + tq + tk - 1) // tk

    def full_step(j, carry):
        step(j, False)
        return carry

    def masked_step(j, carry):
        step(j, True)
        return carry

    lax.fori_loop(0, n_full, full_step, 0)
    lax.fori_loop(n_full, n_all, masked_step, 0)

    o = jnp.where((lane >> 6) == 0, acc_ref[0] / l_ref[0], acc_ref[1] / l_ref[1])
    o_ref[...] = (o * sg_ref[...]).astype(BF16)


def _fox_attn(q16, qb, k16, kb, v16, sg, *, tq, tk, off):
    bsz, t, d = q16.shape
    kp = k16.shape[1]
    assert t % tq == 0 and kp % tk == 0 and off % tq == 0 and off + t <= kp
    off_blocks = off // tq
    qspec = pl.BlockSpec((None, tq, LANES), lambda b, p, i: (b, i, p))
    kspec = pl.BlockSpec((None, kp, LANES), lambda b, p, i: (b, 0, p))
    vmem = 2 * (3 * kp * LANES * 2 + tq * LANES * (2 + 2 + 4 + 2)) + 6 * tq * LANES * 4 \
        + 2 * tk * 2 * LANES * 2 + 6 * tq * tk * 4
    return pl.pallas_call(
        functools.partial(_attn_body, tq=tq, tk=tk, off=off),
        out_shape=jax.ShapeDtypeStruct((bsz, t, d), BF16),
        grid=(bsz, FOX_PAIRS, t // tq),
        in_specs=[
            qspec,
            pl.BlockSpec((None, None, tq, LANES), lambda b, p, i: (b, p, i + off_blocks, 0)),
            kspec,
            pl.BlockSpec((None, None, kp, LANES), lambda b, p, i: (b, p, 0, 0)),
            kspec,
            qspec,
        ],
        out_specs=qspec,
        scratch_shapes=[pltpu.VMEM((2, tq, 1), F32), pltpu.VMEM((2, tq, 1), F32),
                        pltpu.VMEM((2, tq, LANES), F32)],
        compiler_params=_cparams(("parallel", "parallel", "arbitrary"), vmem + (8 << 20)),
        name="fox_attn",
    )(q16, qb, k16, kb, v16, sg)


TM = 1024
FFN_TF = 256
GLA_TB = 256
ATTN_TQ = 512
ATTN_TK_PAST = 1408
BIAS_TB_CHOICES = (512, 384, 256, 128)


def _round_up(x, m):
    return (x + m - 1) // m * m


def _prep_weights(ffn_norm, w_ffn_gu, w_ffn_down, mix_norm, a_w_in, a_w_g2, a_b_g, a_g_out, a_w_o,
                  kv_norm, w_kvf, b_f, g_k, b_w_qg, b_g_q, b_w_o):
    d = w_ffn_gu.shape[2]
    rank = a_w_g2.shape[1]
    qk = a_w_g2.shape[2]
    vw = a_w_o.shape[1]
    w_in = a_w_in[0]
    pad_cols = lambda w: jnp.pad(w, ((0, 0), (0, LANES - w.shape[1])))
    groups = np.arange(d) // FOX_HEAD_DIM
    gsum = (groups[:, None] == np.arange(LANES)[None, :]).astype(np.float32)
    return dict(
        ffn_norm=ffn_norm, mix_norm=mix_norm, kv_norm=kv_norm,
        w_gu=w_ffn_gu.astype(BF16), w_down=w_ffn_down.astype(BF16),
        a_wq=w_in[:, :qk].astype(BF16), a_wk=w_in[:, qk:2 * qk].astype(BF16),
        a_wv=w_in[:, 2 * qk:2 * qk + vw].astype(BF16),
        a_wr=w_in[:, 2 * qk + vw:2 * qk + 2 * vw].astype(BF16),
        a_wgl=pad_cols(w_in[:, 2 * qk + 2 * vw:]).astype(BF16),
        a_wg2=jnp.pad(a_w_g2[0], ((0, LANES - rank), (0, 0))).astype(BF16),
        a_bg=a_b_g[0], a_gout=a_g_out[0], a_wo=a_w_o[0].astype(BF16),
        kv_wk=w_kvf[:, :d].astype(BF16), kv_wv=w_kvf[:, d:2 * d].astype(BF16),
        kv_wf=pad_cols(w_kvf[:, 2 * d:]).astype(BF16),
        kv_bf=jnp.pad(b_f, (0, LANES - b_f.shape[0])).reshape(1, LANES),
        gk=jnp.tile(g_k, FOX_HEADS).reshape(1, d), gq=jnp.tile(b_g_q[0], FOX_HEADS).reshape(1, d),
        b_wq=b_w_qg[0][:, :d].astype(BF16), b_wg=b_w_qg[0][:, d:].astype(BF16),
        b_wo=b_w_o[0].astype(BF16),
        gsum=jnp.asarray(gsum, BF16), gexp=jnp.asarray(gsum.T, BF16),
    )


def _trunk(x, s0, past, w):
    bsz, t, d = x.shape
    n = bsz * t
    h = x.reshape(n, d)
    ffn = lambda h_, layer, half: _ffn(h_, w["ffn_norm"][layer, half], w["w_gu"][layer, half],
                                       w["w_down"][layer, half], tm=TM, tf=FFN_TF)
    h = ffn(h, 0, 0)
    q, k, v, r, la = _gla_in(h, w["mix_norm"][0], w["a_wq"], w["a_wk"], w["a_wv"], w["a_wr"],
                             w["a_wgl"], w["a_wg2"], w["a_bg"], tm=TM)
    s3 = lambda z: z.reshape(bsz, t, z.shape[1])
    chunk = 64
    og, st_fin = _gla(s3(q), s3(k), s3(v), s3(r), s3(la), jnp.swapaxes(s0, 2, 3), w["a_gout"],
                      chunk=chunk, tb=GLA_TB)
    h = _proj_res(h, og.reshape(n, -1), w["a_wo"], tm=TM)
    h = ffn(h, 0, 1)
    k_new, v_new, lf, k16, v16 = _shared_kv(h, w["kv_norm"], w["kv_wk"], w["kv_wv"], w["kv_wf"],
                                            w["kv_bf"], w["gk"], w["gsum"], w["gexp"], tm=TM)
    h = ffn(h, 1, 0)
    q16, sg = _fox_q(h, w["mix_norm"][1], w["b_wq"], w["b_wg"], w["gq"], w["gsum"], w["gexp"], tm=TM)
    if past is None:
        k_all, v_all, lf_all = s3(k16), s3(v16), s3(lf)
        off, tq, tk = 0, min(ATTN_TQ, t), min(ATTN_TQ, t)
    else:
        past_k, past_v, past_lf = past
        p_len = past_k.shape[1]
        total = p_len + t
        tk = ATTN_TK_PAST if total >= ATTN_TK_PAST else _round_up(total, LANES)
        pad = _round_up(total, tk) - total
        cat = lambda a, b_: jnp.concatenate(
            [a, b_, jnp.zeros((bsz, pad, a.shape[2]), a.dtype)], axis=1)
        k_all = cat(past_k.reshape(bsz, p_len, d).astype(BF16), s3(k16))
        v_all = cat(past_v.reshape(bsz, p_len, d).astype(BF16), s3(v16))
        lf_all = cat(jnp.pad(past_lf, ((0, 0), (0, 0), (0, LANES - past_lf.shape[2]))), s3(lf))
        off, tq = p_len, t
    kp = k_all.shape[1]
    bias_tb = next(c for c in BIAS_TB_CHOICES if kp % c == 0)
    qb, kb = _fox_bias(lf_all, tb=bias_tb)
    og = _fox_attn(s3(q16), qb, k_all, kb, v_all, s3(sg), tq=tq, tk=tk, off=off)
    h = _proj_res(h, og.reshape(n, d), w["b_wo"], tm=TM)
    h = ffn(h, 1, 1)
    heads4 = lambda z: z.reshape(bsz, t, FOX_HEADS, FOX_HEAD_DIM)
    return (h.reshape(bsz, t, d), jnp.swapaxes(st_fin, 2, 3)[:, None], heads4(k_new), heads4(v_new),
            lf[:, :FOX_HEADS].reshape(bsz, t, FOX_HEADS))


def kernel(x_prompt, x_sample, state_gla, cache_k, cache_v, cache_logf, ffn_norm, w_ffn_gu, w_ffn_down, mix_norm, a_w_in, a_w_g2, a_b_g, a_g_out, a_w_o, kv_norm, w_kvf, b_f, g_k, b_w_qg, b_g_q, b_w_o):
    w = _prep_weights(ffn_norm, w_ffn_gu, w_ffn_down, mix_norm, a_w_in, a_w_g2, a_b_g, a_g_out,
                      a_w_o, kv_norm, w_kvf, b_f, g_k, b_w_qg, b_g_q, b_w_o)
    s0_prompt = jnp.zeros((x_prompt.shape[0],) + state_gla.shape[2:], F32)
    y_p, gla_p, k_p, v_p, lf_p = _trunk(x_prompt, s0_prompt, None, w)
    y_s, gla_s, k_s, v_s, lf_s = _trunk(x_sample, state_gla[:, 0], (cache_k, cache_v, cache_logf), w)
    return (y_p, y_s, gla_p, gla_s, k_p, v_p, lf_p, k_s, v_s, lf_s)
```

```python
import functools

import jax
import jax.numpy as jnp
import numpy as np
from jax import lax
from jax.experimental import pallas as pl
from jax.experimental.pallas import tpu as pltpu

F32 = jnp.float32
BF16 = jnp.bfloat16

EPS = 1e-6
NEG_INF = -1e30
LOG2E = 1.4426950408889634

LANES = 128
V7X_VMEM_BYTES = 64 * 1024 * 1024
V7X_SCOPED_VMEM_CAP = 60000 * 1024

GLA_HEADS = 4
GLA_GATE_TAU = 16.0
FOX_HEADS = 16
FOX_HEAD_DIM = 64
FOX_PAIRS = FOX_HEADS // 2
BIAS_PARTS = 3
BIAS_LANES_PER_HEAD = 2 * BIAS_PARTS

NT_DIMS = (((1,), (1,)), ((), ()))
TN_DIMS = (((0,), (0,)), ((), ()))


def _cparams(semantics, vmem_bytes):
    limit = int(min(max(vmem_bytes, 16 * 1024 * 1024), V7X_SCOPED_VMEM_CAP))
    return pltpu.CompilerParams(dimension_semantics=semantics, vmem_limit_bytes=limit)


def _dot(a, b):
    return jnp.dot(a, b, preferred_element_type=F32)


def _rms(x, g):
    ms = jnp.mean(x * x, axis=-1, keepdims=True)
    return x * lax.rsqrt(ms + EPS) * g


def _split2(x):
    hi = x.astype(BF16)
    lo = (x - hi.astype(F32)).astype(BF16)
    return hi, lo


def _split3(x):
    p1 = x.astype(BF16)
    r1 = x - p1.astype(F32)
    p2 = r1.astype(BF16)
    p3 = (r1 - p2.astype(F32)).astype(BF16)
    return p1, p2, p3


def _log_sigmoid(x):
    return jnp.minimum(x, 0.0) - jnp.log(1.0 + jnp.exp(-jnp.abs(x)))


def _head_norm(x, gsum_ref, gexp_ref, gain, head_dim):
    hi, lo = _split2(x * x)
    ssq = _dot(hi, gsum_ref[...]) + _dot(lo, gsum_ref[...])
    inv = lax.rsqrt(ssq * (1.0 / head_dim) + EPS)
    ihi, ilo = _split2(inv)
    inv_full = _dot(ihi, gexp_ref[...]) + _dot(ilo, gexp_ref[...])
    return x * inv_full * gain


def _ffn_body(x_ref, g_ref, wgu_ref, wd_ref, o_ref, xn_ref):
    j = pl.program_id(1)

    @pl.when(j == 0)
    def _():
        xn_ref[...] = _rms(x_ref[...], g_ref[...]).astype(BF16)
        o_ref[...] = jnp.zeros_like(o_ref)

    tf = wd_ref.shape[0]
    gu = _dot(xn_ref[...], wgu_ref[...])
    g = gu[:, :tf]
    u = gu[:, tf:]
    h = (g * jax.nn.sigmoid(g) * u).astype(BF16)
    o_ref[...] += _dot(h, wd_ref[...])

    @pl.when(j == pl.num_programs(1) - 1)
    def _():
        o_ref[...] = x_ref[...] + 0.5 * o_ref[...]


def _interleave_gate_up(w_gu, tf):
    lead, d = w_gu.shape[:-2], w_gu.shape[-2]
    nf = w_gu.shape[-1] // (2 * tf)
    w = w_gu.reshape(lead + (d, 2, nf, tf))
    return jnp.swapaxes(w, -3, -2).reshape(lead + (d, 2 * nf * tf))


def _ffn(x, g, w_gu, w_down, *, tm, tf):
    n, d = x.shape
    d_ff = w_down.shape[0]
    tm = min(tm, n)
    nf = d_ff // tf
    assert n % tm == 0 and d_ff % tf == 0
    vmem = 4 * tm * d * 4 + tm * d * 2 + 2 * 3 * d * tf * 2 + 3 * tm * tf * 4 + tm * d * 4
    return pl.pallas_call(
        _ffn_body,
        out_shape=jax.ShapeDtypeStruct((n, d), F32),
        grid=(n // tm, nf),
        in_specs=[
            pl.BlockSpec((tm, d), lambda i, j: (i, 0)),
            pl.BlockSpec((1, d), lambda i, j: (0, 0)),
            pl.BlockSpec((d, 2 * tf), lambda i, j: (0, j)),
            pl.BlockSpec((tf, d), lambda i, j: (j, 0)),
        ],
        out_specs=pl.BlockSpec((tm, d), lambda i, j: (i, 0)),
        scratch_shapes=[pltpu.VMEM((tm, d), BF16)],
        compiler_params=_cparams(("parallel", "arbitrary"), vmem + (8 << 20)),
        name="ffn",
    )(x, g.reshape(1, d), w_gu, w_down)


def _proj_res_body(h_ref, a_ref, w_ref, o_ref):
    o_ref[...] = h_ref[...] + _dot(a_ref[...], w_ref[...])


def _proj_res(h, a, w, *, tm):
    n, d = h.shape
    k = a.shape[1]
    tm = min(tm, n)
    assert n % tm == 0
    vmem = 2 * (2 * tm * d * 4 + tm * k * 2 + k * d * 2) + tm * d * 4
    return pl.pallas_call(
        _proj_res_body,
        out_shape=jax.ShapeDtypeStruct((n, d), F32),
        grid=(n // tm,),
        in_specs=[
            pl.BlockSpec((tm, d), lambda i: (i, 0)),
            pl.BlockSpec((tm, k), lambda i: (i, 0)),
            pl.BlockSpec((k, d), lambda i: (0, 0)),
        ],
        out_specs=pl.BlockSpec((tm, d), lambda i: (i, 0)),
        compiler_params=_cparams(("parallel",), vmem + (4 << 20)),
        name="proj_res",
    )(h, a, w)


def _gla_in_body(x_ref, g_ref, wq_ref, wk_ref, wv_ref, wr_ref, wgl_ref, wg2_ref, bg_ref,
                 q_ref, k_ref, v_ref, r_ref, la_ref):
    xn = _rms(x_ref[...], g_ref[...]).astype(BF16)
    q_ref[...] = _dot(xn, wq_ref[...])
    k_ref[...] = _dot(xn, wk_ref[...])
    v_ref[...] = _dot(xn, wv_ref[...])
    r_ref[...] = _dot(xn, wr_ref[...])
    gl = _dot(xn, wgl_ref[...]).astype(BF16)
    z = _dot(gl, wg2_ref[...]) + bg_ref[...]
    la_ref[...] = _log_sigmoid(z) * (1.0 / GLA_GATE_TAU)


def _gla_in(x, g, wq, wk, wv, wr, wgl, wg2, bg, *, tm):
    n, d = x.shape
    qk, vw = wq.shape[1], wv.shape[1]
    tm = min(tm, n)
    assert n % tm == 0
    row = lambda i: (i, 0)
    fix = lambda i: (0, 0)
    w_bytes = 2 * (2 * d * qk + 2 * d * vw + d * LANES + LANES * qk)
    vmem = 2 * tm * (d + 3 * qk + 2 * vw) * 4 + 2 * w_bytes + tm * d * 2
    return pl.pallas_call(
        _gla_in_body,
        out_shape=[jax.ShapeDtypeStruct((n, qk), F32), jax.ShapeDtypeStruct((n, qk), F32),
                   jax.ShapeDtypeStruct((n, vw), F32), jax.ShapeDtypeStruct((n, vw), F32),
                   jax.ShapeDtypeStruct((n, qk), F32)],
        grid=(n // tm,),
        in_specs=[
            pl.BlockSpec((tm, d), row), pl.BlockSpec((1, d), fix),
            pl.BlockSpec((d, qk), fix), pl.BlockSpec((d, qk), fix),
            pl.BlockSpec((d, vw), fix), pl.BlockSpec((d, vw), fix),
            pl.BlockSpec((d, LANES), fix), pl.BlockSpec((LANES, qk), fix), pl.BlockSpec((1, qk), fix),
        ],
        out_specs=[pl.BlockSpec((tm, qk), row), pl.BlockSpec((tm, qk), row),
                   pl.BlockSpec((tm, vw), row), pl.BlockSpec((tm, vw), row),
                   pl.BlockSpec((tm, qk), row)],
        compiler_params=_cparams(("parallel",), vmem + (8 << 20)),
        name="gla_in",
    )(x, g.reshape(1, d), wq, wk, wv, wr, wgl, wg2, bg.reshape(1, qk))


def _gla_body(q_ref, k_ref, v_ref, r_ref, la_ref, s0_ref, gout_ref, tri_ref,
              og_ref, sfin_ref, st_ref, *, chunk, n_chunks, heads, dk, dv):
    t = pl.program_id(1)

    @pl.when(t == 0)
    def _():
        st_ref[...] = s0_ref[...]

    scale = dk ** -0.5
    row = lax.broadcasted_iota(jnp.int32, (chunk, chunk), 0)
    col = lax.broadcasted_iota(jnp.int32, (chunk, chunk), 1)
    causal = col <= row
    tri = tri_ref[...]
    gout = gout_ref[...]

    for c in range(n_chunks):
        sl = slice(c * chunk, (c + 1) * chunk)
        la_hi, la_lo = _split2(la_ref[sl, :])
        b = _dot(tri, la_hi) + _dot(tri, la_lo)
        b_last = b[chunk - 1:chunk, :]
        q = q_ref[sl, :]
        k = k_ref[sl, :]
        qe = (q * scale * jnp.exp(b)).astype(BF16)
        ke = (k * jnp.exp(-b)).astype(BF16)
        kd = (k * jnp.exp(b_last - b)).astype(BF16)
        dec = jnp.exp(b_last)
        for h in range(heads):
            ks = slice(h * dk, (h + 1) * dk)
            vs = slice(h * dv, (h + 1) * dv)
            vh = v_ref[sl, vs].astype(BF16)
            att = lax.dot_general(qe[:, ks], ke[:, ks], NT_DIMS, preferred_element_type=F32)
            att = jnp.where(causal, att, 0.0).astype(BF16)
            st = st_ref[h]
            o = _dot(att, vh) + lax.dot_general(qe[:, ks], st.astype(BF16), NT_DIMS,
                                                preferred_element_type=F32)
            st_ref[h] = st * dec[:, ks] + lax.dot_general(vh, kd[:, ks], TN_DIMS,
                                                          preferred_element_type=F32)
            on = _rms(o, gout)
            rh = r_ref[sl, vs]
            og_ref[sl, vs] = (on * (rh * jax.nn.sigmoid(rh))).astype(BF16)

    @pl.when(t == pl.num_programs(1) - 1)
    def _():
        sfin_ref[...] = st_ref[...]


def _gla(q, k, v, r, la, s0t, gout, *, chunk, tb):
    bsz, t, qk = q.shape
    vw = v.shape[2]
    heads = s0t.shape[1]
    dk, dv = qk // heads, vw // heads
    tb = min(tb, t)
    assert t % tb == 0 and tb % chunk == 0
    tri = jnp.tril(jnp.ones((chunk, chunk), F32)).astype(BF16)
    blk = lambda w: pl.BlockSpec((None, tb, w), lambda b, i: (b, i, 0))
    st_spec = pl.BlockSpec((None, heads, dv, dk), lambda b, i: (b, 0, 0, 0))
    vmem = 2 * tb * (3 * qk + 2 * vw) * 4 + 2 * tb * vw * 2 + 5 * heads * dv * dk * 4
    return pl.pallas_call(
        functools.partial(_gla_body, chunk=chunk, n_chunks=tb // chunk, heads=heads, dk=dk, dv=dv),
        out_shape=[jax.ShapeDtypeStruct((bsz, t, vw), BF16),
                   jax.ShapeDtypeStruct((bsz, heads, dv, dk), F32)],
        grid=(bsz, t // tb),
        in_specs=[blk(qk), blk(qk), blk(vw), blk(vw), blk(qk), st_spec,
                  pl.BlockSpec((1, dv), lambda b, i: (0, 0)),
                  pl.BlockSpec((chunk, chunk), lambda b, i: (0, 0))],
        out_specs=[blk(vw), st_spec],
        scratch_shapes=[pltpu.VMEM((heads, dv, dk), F32)],
        compiler_params=_cparams(("parallel", "arbitrary"), vmem + (8 << 20)),
        name="gla_chunks",
    )(q, k, v, r, la, s0t, gout.reshape(1, dv), tri)


def _kv_body(x_ref, g_ref, wk_ref, wv_ref, wf_ref, bf_ref, gk_ref, gsum_ref, gexp_ref,
             k_ref, v_ref, lf_ref, k16_ref, v16_ref, *, vt_block):
    xn = _rms(x_ref[...], g_ref[...]).astype(BF16)
    k = _head_norm(_dot(xn, wk_ref[...]), gsum_ref, gexp_ref, gk_ref[...], FOX_HEAD_DIM)
    k_ref[...] = k
    k16_ref[...] = k.astype(BF16)
    lf_ref[...] = _log_sigmoid(_dot(xn, wf_ref[...]) + bf_ref[...])
    if vt_block is None:
        v = _dot(xn, wv_ref[...])
        v_ref[...] = v
        v16_ref[...] = v.astype(BF16)
    else:
        vt = lax.dot_general(wv_ref[...], xn, NT_DIMS, preferred_element_type=F32)
        v_ref[...] = vt.T
        vt16 = vt.astype(BF16)
        for c in range(v16_ref.shape[0]):
            v16_ref[c] = vt16[:, c * vt_block:(c + 1) * vt_block]


def _shared_kv(x, g, wk, wv, wf, bf, gk, gsum, gexp, *, tm, vt_block=None):
    n, d = x.shape
    tm = min(tm, n)
    assert n % tm == 0
    row = lambda i: (i, 0)
    fix = lambda i: (0, 0)
    if vt_block is None:
        v16_shape, v16_spec = (n, d), pl.BlockSpec((tm, d), row)
    else:
        assert tm % vt_block == 0
        v16_shape = (n // vt_block, d, vt_block)
        v16_spec = pl.BlockSpec((tm // vt_block, d, vt_block), lambda i: (i, 0, 0))
    vmem = 2 * tm * d * (4 + 4 + 4 + 2 + 2) + 2 * tm * LANES * 4 + 2 * 2 * (2 * d * d + 3 * d * LANES) \
        + 6 * tm * d * 4
    return pl.pallas_call(
        functools.partial(_kv_body, vt_block=vt_block),
        out_shape=[jax.ShapeDtypeStruct((n, d), F32), jax.ShapeDtypeStruct((n, d), F32),
                   jax.ShapeDtypeStruct((n, LANES), F32),
                   jax.ShapeDtypeStruct((n, d), BF16), jax.ShapeDtypeStruct(v16_shape, BF16)],
        grid=(n // tm,),
        in_specs=[
            pl.BlockSpec((tm, d), row), pl.BlockSpec((1, d), fix),
            pl.BlockSpec((d, d), fix), pl.BlockSpec((d, d), fix), pl.BlockSpec((d, LANES), fix),
            pl.BlockSpec((1, LANES), fix), pl.BlockSpec((1, d), fix),
            pl.BlockSpec((d, LANES), fix), pl.BlockSpec((LANES, d), fix),
        ],
        out_specs=[pl.BlockSpec((tm, d), row), pl.BlockSpec((tm, d), row),
                   pl.BlockSpec((tm, LANES), row),
                   pl.BlockSpec((tm, d), row), v16_spec],
        compiler_params=_cparams(("parallel",), vmem),
        name="shared_kv",
    )(x, g.reshape(1, d), wk, wv, wf, bf, gk, gsum, gexp)


def _fox_q_body(x_ref, g_ref, wq_ref, wg_ref, gq_ref, gsum_ref, gexp_ref, q16_ref, sg_ref):
    xn = _rms(x_ref[...], g_ref[...]).astype(BF16)
    q = _head_norm(_dot(xn, wq_ref[...]), gsum_ref, gexp_ref, gq_ref[...], FOX_HEAD_DIM)
    q16_ref[...] = (q * (FOX_HEAD_DIM ** -0.5 * LOG2E)).astype(BF16)
    sg_ref[...] = jax.nn.sigmoid(_dot(xn, wg_ref[...]))


def _fox_q(x, g, wq, wg, gq, gsum, gexp, *, tm):
    n, d = x.shape
    tm = min(tm, n)
    assert n % tm == 0
    row = lambda i: (i, 0)
    fix = lambda i: (0, 0)
    vmem = 2 * tm * d * (4 + 2 + 4) + 2 * 2 * (2 * d * d + 2 * d * LANES) + 6 * tm * d * 4
    return pl.pallas_call(
        _fox_q_body,
        out_shape=[jax.ShapeDtypeStruct((n, d), BF16), jax.ShapeDtypeStruct((n, d), F32)],
        grid=(n // tm,),
        in_specs=[
            pl.BlockSpec((tm, d), row), pl.BlockSpec((1, d), fix),
            pl.BlockSpec((d, d), fix), pl.BlockSpec((d, d), fix), pl.BlockSpec((1, d), fix),
            pl.BlockSpec((d, LANES), fix), pl.BlockSpec((LANES, d), fix),
        ],
        out_specs=[pl.BlockSpec((tm, d), row), pl.BlockSpec((tm, d), row)],
        compiler_params=_cparams(("parallel",), vmem),
        name="fox_q",
    )(x, g.reshape(1, d), wq, wg, gq, gsum, gexp)


def _bias_body(lf_ref, tri_ref, selq_ref, selk_ref, oneq_ref, onek_ref, qb_ref, kb_ref, carry_ref):
    @pl.when(pl.program_id(1) == 0)
    def _():
        carry_ref[...] = jnp.zeros_like(carry_ref)

    tri = tri_ref[...]
    p1, p2, p3 = _split3(lf_ref[...])
    c = _dot(tri, p1) + _dot(tri, p2) + _dot(tri, p3) + carry_ref[...]
    tb = c.shape[0]
    carry_ref[...] = c[tb - 1:tb, :]
    c1, c2, c3 = _split3(c * LOG2E)
    qb = _dot(c1, selq_ref[0]) + _dot(c2, selq_ref[1]) + _dot(c3, selq_ref[2]) + oneq_ref[...]
    kb = _dot(c1, selk_ref[0]) + _dot(c2, selk_ref[1]) + _dot(c3, selk_ref[2]) + onek_ref[...]
    for p in range(FOX_PAIRS):
        qb_ref[p] = qb[:, p * LANES:(p + 1) * LANES].astype(BF16)
        kb_ref[p] = kb[:, p * LANES:(p + 1) * LANES].astype(BF16)


def _bias_constants():
    selq = np.zeros((BIAS_PARTS, LANES, FOX_PAIRS * LANES), np.float32)
    selk = np.zeros((BIAS_PARTS, LANES, FOX_PAIRS * LANES), np.float32)
    oneq = np.zeros((1, FOX_PAIRS * LANES), np.float32)
    onek = np.zeros((1, FOX_PAIRS * LANES), np.float32)
    for p in range(FOX_PAIRS):
        for e in range(2):
            base = p * LANES + e * BIAS_LANES_PER_HEAD
            for part in range(BIAS_PARTS):
                selq[part, 2 * p + e, base + part] = 1.0
                onek[0, base + part] = 1.0
                selk[part, 2 * p + e, base + BIAS_PARTS + part] = -1.0
                oneq[0, base + BIAS_PARTS + part] = 1.0
    return (jnp.asarray(selq, BF16), jnp.asarray(selk, BF16), jnp.asarray(oneq), jnp.asarray(onek))


def _fox_bias(lf, *, tb):
    bsz, kp, _ = lf.shape
    assert kp % tb == 0
    selq, selk, oneq, onek = _bias_constants()
    tri = jnp.tril(jnp.ones((tb, tb), F32)).astype(BF16)
    wide = FOX_PAIRS * LANES
    out_spec = pl.BlockSpec((None, FOX_PAIRS, tb, LANES), lambda b, i: (b, 0, i, 0))
    vmem = 2 * tb * LANES * 4 + 2 * tb * tb * 2 + 4 * BIAS_PARTS * LANES * wide * 2 \
        + 4 * FOX_PAIRS * tb * LANES * 2 + 6 * tb * wide * 4
    return pl.pallas_call(
        _bias_body,
        out_shape=[jax.ShapeDtypeStruct((bsz, FOX_PAIRS, kp, LANES), BF16)] * 2,
        grid=(bsz, kp // tb),
        in_specs=[
            pl.BlockSpec((None, tb, LANES), lambda b, i: (b, i, 0)),
            pl.BlockSpec((tb, tb), lambda b, i: (0, 0)),
            pl.BlockSpec((BIAS_PARTS, LANES, wide), lambda b, i: (0, 0, 0)),
            pl.BlockSpec((BIAS_PARTS, LANES, wide), lambda b, i: (0, 0, 0)),
            pl.BlockSpec((1, wide), lambda b, i: (0, 0)),
            pl.BlockSpec((1, wide), lambda b, i: (0, 0)),
        ],
        out_specs=[out_spec, out_spec],
        scratch_shapes=[pltpu.VMEM((1, LANES), F32)],
        compiler_params=_cparams(("parallel", "arbitrary"), vmem),
        name="fox_bias",
    )(lf, tri, selq, selk, oneq, onek)


def _attn_body(q_ref, qb_ref, k_ref, kb_ref, v_ref, sg_ref, o_ref, m_ref, l_ref, acc_ref,
               *, tq, tk, off, keys_major):
    k_axis = 0 if keys_major else 1
    i = pl.program_id(2)
    lane = lax.broadcasted_iota(jnp.int32, (tq, LANES), 1)
    q = q_ref[...]
    qb = qb_ref[...]
    zero = jnp.zeros_like(q)
    q_aug = []
    for e in range(2):
        head_lanes = (lane >> 6) == e
        bias_lanes = jnp.logical_and(lane >= e * BIAS_LANES_PER_HEAD,
                                     lane < (e + 1) * BIAS_LANES_PER_HEAD)
        q_aug.append(jnp.concatenate(
            [jnp.where(head_lanes, q, zero), jnp.where(bias_lanes, qb, zero)], axis=1))

    m_ref[...] = jnp.full(m_ref.shape, NEG_INF, F32)
    l_ref[...] = jnp.zeros(l_ref.shape, F32)
    acc_ref[...] = jnp.zeros(acc_ref.shape, F32)
    q_lo = i * tq + off

    def step(j, masked):
        ks = pl.multiple_of(j * tk, tk)
        k_aug = jnp.concatenate([k_ref[pl.ds(ks, tk), :], kb_ref[pl.ds(ks, tk), :]], axis=1)
        v = v_ref[j] if keys_major else v_ref[pl.ds(ks, tk), :]
        for e in range(2):
            if keys_major:
                s = lax.dot_general(k_aug, q_aug[e], NT_DIMS, preferred_element_type=F32)
            else:
                s = lax.dot_general(q_aug[e], k_aug, NT_DIMS, preferred_element_type=F32)
            if masked:
                qpos = q_lo + lax.broadcasted_iota(jnp.int32, s.shape, 1 - k_axis)
                kpos = ks + lax.broadcasted_iota(jnp.int32, s.shape, k_axis)
                s = jnp.where(kpos <= qpos, s, NEG_INF)
            m_prev = m_ref[e]
            m_new = jnp.maximum(m_prev, jnp.max(s, axis=k_axis, keepdims=True))
            alpha = jnp.exp2(m_prev - m_new)
            p = jnp.exp2(s - m_new)
            l_ref[e] = alpha * l_ref[e] + jnp.sum(p, axis=k_axis, keepdims=True)
            p16 = p.astype(BF16)
            acc_ref[e] = alpha * acc_ref[e] + (_dot(v, p16) if keys_major else _dot(p16, v))
            m_ref[e] = m_new

    n_full = (q_lo + 1) // tk
    n_all = (q_lo + tq + tk - 1) // tk

    def full_step(j, carry):
        step(j, False)
        return carry

    def masked_step(j, carry):
        step(j, True)
        return carry

    lax.fori_loop(0, n_full, full_step, 0)
    lax.fori_loop(n_full, n_all, masked_step, 0)

    o0 = acc_ref[0] * (1.0 / l_ref[0])
    o1 = acc_ref[1] * (1.0 / l_ref[1])
    if keys_major:
        feature = lax.broadcasted_iota(jnp.int32, (LANES, tq), 0)
        o = jnp.where((feature >> 6) == 0, o0, o1).T
    else:
        o = jnp.where((lane >> 6) == 0, o0, o1)
    o_ref[...] = (o * sg_ref[...]).astype(BF16)


def _fox_attn(q16, qb, k16, kb, v16, sg, *, tq, tk, off, keys_major):
    bsz, t, d = q16.shape
    kp = k16.shape[1]
    assert t % tq == 0 and kp % tk == 0 and off % tq == 0 and off + t <= kp
    off_blocks = off // tq
    qspec = pl.BlockSpec((None, tq, LANES), lambda b, p, i: (b, i, p))
    kspec = pl.BlockSpec((None, kp, LANES), lambda b, p, i: (b, 0, p))
    if keys_major:
        vspec = pl.BlockSpec((kp // tk, LANES, tk), lambda b, p, i: (b, p, 0))
        stat_shape, acc_shape = (2, 1, tq), (2, LANES, tq)
    else:
        vspec = kspec
        stat_shape, acc_shape = (2, tq, 1), (2, tq, LANES)
    vmem = 2 * (3 * kp * LANES * 2 + tq * LANES * (2 + 2 + 4 + 2)) + 6 * tq * LANES * 4 \
        + 2 * tk * 2 * LANES * 2 + 6 * tq * tk * 4
    return pl.pallas_call(
        functools.partial(_attn_body, tq=tq, tk=tk, off=off, keys_major=keys_major),
        out_shape=jax.ShapeDtypeStruct((bsz, t, d), BF16),
        grid=(bsz, FOX_PAIRS, t // tq),
        in_specs=[
            qspec,
            pl.BlockSpec((None, None, tq, LANES), lambda b, p, i: (b, p, i + off_blocks, 0)),
            kspec,
            pl.BlockSpec((None, None, kp, LANES), lambda b, p, i: (b, p, 0, 0)),
            vspec,
            qspec,
        ],
        out_specs=qspec,
        scratch_shapes=[pltpu.VMEM(stat_shape, F32), pltpu.VMEM(stat_shape, F32),
                        pltpu.VMEM(acc_shape, F32)],
        compiler_params=_cparams(("parallel", "parallel", "arbitrary"), vmem + (8 << 20)),
        name="fox_attn",
    )(q16, qb, k16, kb, v16, sg)


TM = 1024
FFN_TF = 256
GLA_TB = 256
ATTN_TQ = 512
ATTN_TK_PAST = 1408
BIAS_TB_CHOICES = (512, 384, 256, 128)


def _round_up(x, m):
    return (x + m - 1) // m * m


def _prep_weights(ffn_norm, w_ffn_gu, w_ffn_down, mix_norm, a_w_in, a_w_g2, a_b_g, a_g_out, a_w_o,
                  kv_norm, w_kvf, b_f, g_k, b_w_qg, b_g_q, b_w_o):
    d = w_ffn_gu.shape[2]
    rank = a_w_g2.shape[1]
    qk = a_w_g2.shape[2]
    vw = a_w_o.shape[1]
    w_in = a_w_in[0]
    pad_cols = lambda w: jnp.pad(w, ((0, 0), (0, LANES - w.shape[1])))
    groups = np.arange(d) // FOX_HEAD_DIM
    gsum = (groups[:, None] == np.arange(LANES)[None, :]).astype(np.float32)
    return dict(
        ffn_norm=ffn_norm, mix_norm=mix_norm, kv_norm=kv_norm,
        w_gu=_interleave_gate_up(w_ffn_gu.astype(BF16), FFN_TF), w_down=w_ffn_down.astype(BF16),
        a_wq=w_in[:, :qk].astype(BF16), a_wk=w_in[:, qk:2 * qk].astype(BF16),
        a_wv=w_in[:, 2 * qk:2 * qk + vw].astype(BF16),
        a_wr=w_in[:, 2 * qk + vw:2 * qk + 2 * vw].astype(BF16),
        a_wgl=pad_cols(w_in[:, 2 * qk + 2 * vw:]).astype(BF16),
        a_wg2=jnp.pad(a_w_g2[0], ((0, LANES - rank), (0, 0))).astype(BF16),
        a_bg=a_b_g[0], a_gout=a_g_out[0], a_wo=a_w_o[0].astype(BF16),
        kv_wk=w_kvf[:, :d].astype(BF16), kv_wv=w_kvf[:, d:2 * d].astype(BF16),
        kv_wv_t=w_kvf[:, d:2 * d].T.astype(BF16),
        kv_wf=pad_cols(w_kvf[:, 2 * d:]).astype(BF16),
        kv_bf=jnp.pad(b_f, (0, LANES - b_f.shape[0])).reshape(1, LANES),
        gk=jnp.tile(g_k, FOX_HEADS).reshape(1, d), gq=jnp.tile(b_g_q[0], FOX_HEADS).reshape(1, d),
        b_wq=b_w_qg[0][:, :d].astype(BF16), b_wg=b_w_qg[0][:, d:].astype(BF16),
        b_wo=b_w_o[0].astype(BF16),
        gsum=jnp.asarray(gsum, BF16), gexp=jnp.asarray(gsum.T, BF16),
    )


def _trunk(x, s0, past, w):
    bsz, t, d = x.shape
    n = bsz * t
    h = x.reshape(n, d)
    ffn = lambda h_, layer, half: _ffn(h_, w["ffn_norm"][layer, half], w["w_gu"][layer, half],
                                       w["w_down"][layer, half], tm=TM, tf=FFN_TF)
    h = ffn(h, 0, 0)
    q, k, v, r, la = _gla_in(h, w["mix_norm"][0], w["a_wq"], w["a_wk"], w["a_wv"], w["a_wr"],
                             w["a_wgl"], w["a_wg2"], w["a_bg"], tm=TM)
    s3 = lambda z: z.reshape(bsz, t, z.shape[1])
    chunk = 64
    og, st_fin = _gla(s3(q), s3(k), s3(v), s3(r), s3(la), jnp.swapaxes(s0, 2, 3), w["a_gout"],
                      chunk=chunk, tb=GLA_TB)
    h = _proj_res(h, og.reshape(n, -1), w["a_wo"], tm=TM)
    h = ffn(h, 0, 1)
    keys_major = past is None
    tq = tk = min(ATTN_TQ, t)
    k_new, v_new, lf, k16, v16 = _shared_kv(
        h, w["kv_norm"], w["kv_wk"], w["kv_wv_t"] if keys_major else w["kv_wv"], w["kv_wf"],
        w["kv_bf"], w["gk"], w["gsum"], w["gexp"], tm=TM, vt_block=tk if keys_major else None)
    h = ffn(h, 1, 0)
    q16, sg = _fox_q(h, w["mix_norm"][1], w["b_wq"], w["b_wg"], w["gq"], w["gsum"], w["gexp"], tm=TM)
    if past is None:
        k_all, v_all, lf_all = s3(k16), v16, s3(lf)
        off = 0
    else:
        past_k, past_v, past_lf = past
        p_len = past_k.shape[1]
        total = p_len + t
        tk = ATTN_TK_PAST if total >= ATTN_TK_PAST else _round_up(total, LANES)
        pad = _round_up(total, tk) - total
        cat = lambda a, b_: jnp.concatenate(
            [a, b_, jnp.zeros((bsz, pad, a.shape[2]), a.dtype)], axis=1)
        k_all = cat(past_k.reshape(bsz, p_len, d).astype(BF16), s3(k16))
        v_all = cat(past_v.reshape(bsz, p_len, d).astype(BF16), s3(v16))
        lf_all = cat(jnp.pad(past_lf, ((0, 0), (0, 0), (0, LANES - past_lf.shape[2]))), s3(lf))
        off, tq = p_len, t
    kp = k_all.shape[1]
    bias_tb = next(c for c in BIAS_TB_CHOICES if kp % c == 0)
    qb, kb = _fox_bias(lf_all, tb=bias_tb)
    og = _fox_attn(s3(q16), qb, k_all, kb, v_all, s3(sg), tq=tq, tk=tk, off=off,
                   keys_major=keys_major)
    h = _proj_res(h, og.reshape(n, d), w["b_wo"], tm=TM)
    h = ffn(h, 1, 1)
    heads4 = lambda z: z.reshape(bsz, t, FOX_HEADS, FOX_HEAD_DIM)
    return (h.reshape(bsz, t, d), jnp.swapaxes(st_fin, 2, 3)[:, None], heads4(k_new), heads4(v_new),
            lf[:, :FOX_HEADS].reshape(bsz, t, FOX_HEADS))


def kernel(x_prompt, x_sample, state_gla, cache_k, cache_v, cache_logf, ffn_norm, w_ffn_gu, w_ffn_down, mix_norm, a_w_in, a_w_g2, a_b_g, a_g_out, a_w_o, kv_norm, w_kvf, b_f, g_k, b_w_qg, b_g_q, b_w_o):
    w = _prep_weights(ffn_norm, w_ffn_gu, w_ffn_down, mix_norm, a_w_in, a_w_g2, a_b_g, a_g_out,
                      a_w_o, kv_norm, w_kvf, b_f, g_k, b_w_qg, b_g_q, b_w_o)
    s0_prompt = jnp.zeros((x_prompt.shape[0],) + state_gla.shape[2:], F32)
    y_p, gla_p, k_p, v_p, lf_p = _trunk(x_prompt, s0_prompt, None, w)
    y_s, gla_s, k_s, v_s, lf_s = _trunk(x_sample, state_gla[:, 0], (cache_k, cache_v, cache_logf), w)
    return (y_p, y_s, gla_p, gla_s, k_p, v_p, lf_p, k_s, v_s, lf_s)
```

```python
import functools

import jax
import jax.numpy as jnp
import numpy as np
from jax import lax
from jax.experimental import pallas as pl
from jax.experimental.pallas import tpu as pltpu

F32 = jnp.float32
BF16 = jnp.bfloat16

EPS = 1e-6
NEG_INF = -1e30
LOG2E = 1.4426950408889634

LANES = 128
V7X_VMEM_BYTES = 64 * 1024 * 1024
V7X_SCOPED_VMEM_CAP = 60000 * 1024

GLA_HEADS = 4
GLA_GATE_TAU = 16.0
FOX_HEADS = 16
FOX_HEAD_DIM = 64
FOX_PAIRS = FOX_HEADS // 2
BIAS_PARTS = 3
BIAS_LANES_PER_HEAD = 2 * BIAS_PARTS

NT_DIMS = (((1,), (1,)), ((), ()))
TN_DIMS = (((0,), (0,)), ((), ()))


def _cparams(semantics, vmem_bytes):
    limit = int(min(max(vmem_bytes, 16 * 1024 * 1024), V7X_SCOPED_VMEM_CAP))
    return pltpu.CompilerParams(dimension_semantics=semantics, vmem_limit_bytes=limit)


def _dot(a, b):
    return jnp.dot(a, b, preferred_element_type=F32)


def _rms(x, g):
    ms = jnp.mean(x * x, axis=-1, keepdims=True)
    return x * lax.rsqrt(ms + EPS) * g


def _split2(x):
    hi = x.astype(BF16)
    lo = (x - hi.astype(F32)).astype(BF16)
    return hi, lo


def _split3(x):
    p1 = x.astype(BF16)
    r1 = x - p1.astype(F32)
    p2 = r1.astype(BF16)
    p3 = (r1 - p2.astype(F32)).astype(BF16)
    return p1, p2, p3


def _log_sigmoid(x):
    return jnp.minimum(x, 0.0) - jnp.log(1.0 + jnp.exp(-jnp.abs(x)))


def _head_norm(x, gsum_ref, gexp_ref, gain, head_dim):
    hi, lo = _split2(x * x)
    ssq = _dot(hi, gsum_ref[...]) + _dot(lo, gsum_ref[...])
    inv = lax.rsqrt(ssq * (1.0 / head_dim) + EPS)
    ihi, ilo = _split2(inv)
    inv_full = _dot(ihi, gexp_ref[...]) + _dot(ilo, gexp_ref[...])
    return x * inv_full * gain


def _ffn_body(x_ref, g_ref, wg_ref, wu_ref, wd_ref, o_ref, xn_ref):
    j = pl.program_id(1)

    @pl.when(j == 0)
    def _():
        xn_ref[...] = _rms(x_ref[...], g_ref[...]).astype(BF16)
        o_ref[...] = jnp.zeros_like(o_ref)

    xn = xn_ref[...]
    g = _dot(xn, wg_ref[...])
    u = _dot(xn, wu_ref[...])
    h = (g * jax.nn.sigmoid(g) * u).astype(BF16)
    o_ref[...] += _dot(h, wd_ref[...])

    @pl.when(j == pl.num_programs(1) - 1)
    def _():
        o_ref[...] = x_ref[...] + 0.5 * o_ref[...]


def _tile_cols_body(w_ref, o_ref):
    tf = o_ref.shape[2]
    for c in range(o_ref.shape[0]):
        o_ref[c] = w_ref[:, c * tf:(c + 1) * tf].astype(o_ref.dtype)


def _tile_cols(w, tf, *, rows):
    g, d, n = w.shape
    assert n % tf == 0 and d % rows == 0
    return pl.pallas_call(
        _tile_cols_body,
        out_shape=jax.ShapeDtypeStruct((g, n // tf, d, tf), BF16),
        grid=(g, d // rows),
        in_specs=[pl.BlockSpec((None, rows, n), lambda a, r: (a, r, 0))],
        out_specs=pl.BlockSpec((None, n // tf, rows, tf), lambda a, r: (a, 0, r, 0)),
        compiler_params=_cparams(("parallel", "parallel"), 2 * rows * n * (4 + 2) + (4 << 20)),
        name="tile_cols",
    )(w)


def _ffn(x, g, w_gu, w_down, *, tm):
    n, d = x.shape
    d_ff = w_down.shape[0]
    tf = w_gu.shape[2]
    tm = min(tm, n)
    nf = d_ff // tf
    assert n % tm == 0 and w_gu.shape[0] == 2 * nf
    vmem = 4 * tm * d * 4 + tm * d * 2 + 2 * 3 * d * tf * 2 + 3 * tm * tf * 4 + tm * d * 4
    return pl.pallas_call(
        _ffn_body,
        out_shape=jax.ShapeDtypeStruct((n, d), F32),
        grid=(n // tm, nf),
        in_specs=[
            pl.BlockSpec((tm, d), lambda i, j: (i, 0)),
            pl.BlockSpec((1, d), lambda i, j: (0, 0)),
            pl.BlockSpec((None, d, tf), lambda i, j: (j, 0, 0)),
            pl.BlockSpec((None, d, tf), lambda i, j: (j + nf, 0, 0)),
            pl.BlockSpec((tf, d), lambda i, j: (j, 0)),
        ],
        out_specs=pl.BlockSpec((tm, d), lambda i, j: (i, 0)),
        scratch_shapes=[pltpu.VMEM((tm, d), BF16)],
        compiler_params=_cparams(("parallel", "arbitrary"), vmem + (8 << 20)),
        name="ffn",
    )(x, g.reshape(1, d), w_gu, w_gu, w_down)


def _proj_res_body(h_ref, a_ref, w_ref, o_ref):
    o_ref[...] = h_ref[...] + _dot(a_ref[...], w_ref[...])


def _proj_res(h, a, w, *, tm):
    n, d = h.shape
    k = a.shape[1]
    tm = min(tm, n)
    assert n % tm == 0
    vmem = 2 * (2 * tm * d * 4 + tm * k * 2 + k * d * 2) + tm * d * 4
    return pl.pallas_call(
        _proj_res_body,
        out_shape=jax.ShapeDtypeStruct((n, d), F32),
        grid=(n // tm,),
        in_specs=[
            pl.BlockSpec((tm, d), lambda i: (i, 0)),
            pl.BlockSpec((tm, k), lambda i: (i, 0)),
            pl.BlockSpec((k, d), lambda i: (0, 0)),
        ],
        out_specs=pl.BlockSpec((tm, d), lambda i: (i, 0)),
        compiler_params=_cparams(("parallel",), vmem + (4 << 20)),
        name="proj_res",
    )(h, a, w)


def _gla_in_body(x_ref, g_ref, wq_ref, wk_ref, wv_ref, wr_ref, wgl_ref, wg2_ref, bg_ref,
                 q_ref, k_ref, v_ref, r_ref, la_ref):
    xn = _rms(x_ref[...], g_ref[...]).astype(BF16)
    q_ref[...] = _dot(xn, wq_ref[...]).astype(q_ref.dtype)
    k_ref[...] = _dot(xn, wk_ref[...]).astype(k_ref.dtype)
    v_ref[...] = _dot(xn, wv_ref[...]).astype(v_ref.dtype)
    r_ref[...] = _dot(xn, wr_ref[...]).astype(r_ref.dtype)
    gl = _dot(xn, wgl_ref[...]).astype(BF16)
    z = _dot(gl, wg2_ref[...]) + bg_ref[...]
    la_ref[...] = _log_sigmoid(z) * (1.0 / GLA_GATE_TAU)


def _gla_in(x, g, wq, wk, wv, wr, wgl, wg2, bg, *, tm):
    n, d = x.shape
    qk, vw = wq.shape[1], wv.shape[1]
    tm = min(tm, n)
    assert n % tm == 0
    row = lambda i: (i, 0)
    fix = lambda i: (0, 0)
    w_bytes = 2 * (2 * d * qk + 2 * d * vw + d * LANES + LANES * qk)
    vmem = 2 * tm * (d + 3 * qk + 2 * vw) * 4 + 2 * w_bytes + tm * d * 2
    return pl.pallas_call(
        _gla_in_body,
        out_shape=[jax.ShapeDtypeStruct((n, qk), BF16), jax.ShapeDtypeStruct((n, qk), BF16),
                   jax.ShapeDtypeStruct((n, vw), BF16), jax.ShapeDtypeStruct((n, vw), BF16),
                   jax.ShapeDtypeStruct((n, qk), F32)],
        grid=(n // tm,),
        in_specs=[
            pl.BlockSpec((tm, d), row), pl.BlockSpec((1, d), fix),
            pl.BlockSpec((d, qk), fix), pl.BlockSpec((d, qk), fix),
            pl.BlockSpec((d, vw), fix), pl.BlockSpec((d, vw), fix),
            pl.BlockSpec((d, LANES), fix), pl.BlockSpec((LANES, qk), fix), pl.BlockSpec((1, qk), fix),
        ],
        out_specs=[pl.BlockSpec((tm, qk), row), pl.BlockSpec((tm, qk), row),
                   pl.BlockSpec((tm, vw), row), pl.BlockSpec((tm, vw), row),
                   pl.BlockSpec((tm, qk), row)],
        compiler_params=_cparams(("parallel",), vmem + (8 << 20)),
        name="gla_in",
    )(x, g.reshape(1, d), wq, wk, wv, wr, wgl, wg2, bg.reshape(1, qk))


def _gla_body(q_ref, k_ref, v_ref, r_ref, la_ref, s0_ref, gout_ref, tri_ref,
              og_ref, sfin_ref, st_ref, *, chunk, n_chunks, heads, dk, dv):
    t = pl.program_id(1)

    @pl.when(t == 0)
    def _():
        st_ref[...] = s0_ref[...]

    scale = dk ** -0.5
    row = lax.broadcasted_iota(jnp.int32, (chunk, chunk), 0)
    col = lax.broadcasted_iota(jnp.int32, (chunk, chunk), 1)
    causal = col <= row
    tri = tri_ref[...]
    gout = gout_ref[...]

    for c in range(n_chunks):
        sl = slice(c * chunk, (c + 1) * chunk)
        la_hi, la_lo = _split2(la_ref[sl, :])
        b = _dot(tri, la_hi) + _dot(tri, la_lo)
        b_last = b[chunk - 1:chunk, :]
        q = q_ref[sl, :].astype(F32)
        k = k_ref[sl, :].astype(F32)
        qe = (q * scale * jnp.exp(b)).astype(BF16)
        ke = (k * jnp.exp(-b)).astype(BF16)
        kd = (k * jnp.exp(b_last - b)).astype(BF16)
        dec = jnp.exp(b_last)
        for h in range(heads):
            ks = slice(h * dk, (h + 1) * dk)
            vs = slice(h * dv, (h + 1) * dv)
            vh = v_ref[sl, vs]
            att = lax.dot_general(qe[:, ks], ke[:, ks], NT_DIMS, preferred_element_type=F32)
            att = jnp.where(causal, att, 0.0).astype(BF16)
            st = st_ref[h]
            o = _dot(att, vh) + lax.dot_general(qe[:, ks], st.astype(BF16), NT_DIMS,
                                                preferred_element_type=F32)
            st_ref[h] = st * dec[:, ks] + lax.dot_general(vh, kd[:, ks], TN_DIMS,
                                                          preferred_element_type=F32)
            on = _rms(o, gout)
            rh = r_ref[sl, vs].astype(F32)
            og_ref[sl, vs] = (on * (rh * jax.nn.sigmoid(rh))).astype(BF16)

    @pl.when(t == pl.num_programs(1) - 1)
    def _():
        sfin_ref[...] = st_ref[...]


def _gla(q, k, v, r, la, s0t, gout, *, chunk, tb):
    bsz, t, qk = q.shape
    vw = v.shape[2]
    heads = s0t.shape[1]
    dk, dv = qk // heads, vw // heads
    tb = min(tb, t)
    assert t % tb == 0 and tb % chunk == 0
    tri = jnp.tril(jnp.ones((chunk, chunk), F32)).astype(BF16)
    blk = lambda w: pl.BlockSpec((None, tb, w), lambda b, i: (b, i, 0))
    st_spec = pl.BlockSpec((None, heads, dv, dk), lambda b, i: (b, 0, 0, 0))
    vmem = 2 * tb * (3 * qk + 2 * vw) * 4 + 2 * tb * vw * 2 + 5 * heads * dv * dk * 4
    return pl.pallas_call(
        functools.partial(_gla_body, chunk=chunk, n_chunks=tb // chunk, heads=heads, dk=dk, dv=dv),
        out_shape=[jax.ShapeDtypeStruct((bsz, t, vw), BF16),
                   jax.ShapeDtypeStruct((bsz, heads, dv, dk), F32)],
        grid=(bsz, t // tb),
        in_specs=[blk(qk), blk(qk), blk(vw), blk(vw), blk(qk), st_spec,
                  pl.BlockSpec((1, dv), lambda b, i: (0, 0)),
                  pl.BlockSpec((chunk, chunk), lambda b, i: (0, 0))],
        out_specs=[blk(vw), st_spec],
        scratch_shapes=[pltpu.VMEM((heads, dv, dk), F32)],
        compiler_params=_cparams(("parallel", "arbitrary"), vmem + (8 << 20)),
        name="gla_chunks",
    )(q, k, v, r, la, s0t, gout.reshape(1, dv), tri)


def _store_pairs(ref, x):
    for p in range(ref.shape[0]):
        ref[p] = x[:, p * LANES:(p + 1) * LANES]


def _kv_body(x_ref, g_ref, wk_ref, wv_ref, wf_ref, bf_ref, gk_ref, gsum_ref, gexp_ref,
             k_ref, v_ref, lf_ref, k16_ref, v16_ref, *, vt_block):
    xn = _rms(x_ref[...], g_ref[...]).astype(BF16)
    k = _head_norm(_dot(xn, wk_ref[...]), gsum_ref, gexp_ref, gk_ref[...], FOX_HEAD_DIM)
    k_ref[...] = k
    _store_pairs(k16_ref, k.astype(BF16))
    lf_ref[...] = _log_sigmoid(_dot(xn, wf_ref[...]) + bf_ref[...])
    if vt_block is None:
        v = _dot(xn, wv_ref[...])
        v_ref[...] = v
        _store_pairs(v16_ref, v.astype(BF16))
    else:
        vt = lax.dot_general(wv_ref[...], xn, NT_DIMS, preferred_element_type=F32)
        v_ref[...] = vt.T
        vt16 = vt.astype(BF16)
        for c in range(v16_ref.shape[0]):
            v16_ref[c] = vt16[:, c * vt_block:(c + 1) * vt_block]


def _shared_kv(x, g, wk, wv, wf, bf, gk, gsum, gexp, *, tm, vt_block=None):
    n, d = x.shape
    tm = min(tm, n)
    assert n % tm == 0
    row = lambda i: (i, 0)
    fix = lambda i: (0, 0)
    pair_shape = (d // LANES, n, LANES)
    pair_spec = pl.BlockSpec((d // LANES, tm, LANES), lambda i: (0, i, 0))
    if vt_block is None:
        v16_shape, v16_spec = pair_shape, pair_spec
    else:
        assert tm % vt_block == 0
        v16_shape = (n // vt_block, d, vt_block)
        v16_spec = pl.BlockSpec((tm // vt_block, d, vt_block), lambda i: (i, 0, 0))
    vmem = 2 * tm * d * (4 + 4 + 4 + 2 + 2) + 2 * tm * LANES * 4 + 2 * 2 * (2 * d * d + 3 * d * LANES) \
        + 6 * tm * d * 4
    return pl.pallas_call(
        functools.partial(_kv_body, vt_block=vt_block),
        out_shape=[jax.ShapeDtypeStruct((n, d), F32), jax.ShapeDtypeStruct((n, d), F32),
                   jax.ShapeDtypeStruct((n, LANES), F32),
                   jax.ShapeDtypeStruct(pair_shape, BF16), jax.ShapeDtypeStruct(v16_shape, BF16)],
        grid=(n // tm,),
        in_specs=[
            pl.BlockSpec((tm, d), row), pl.BlockSpec((1, d), fix),
            pl.BlockSpec((d, d), fix), pl.BlockSpec((d, d), fix), pl.BlockSpec((d, LANES), fix),
            pl.BlockSpec((1, LANES), fix), pl.BlockSpec((1, d), fix),
            pl.BlockSpec((d, LANES), fix), pl.BlockSpec((LANES, d), fix),
        ],
        out_specs=[pl.BlockSpec((tm, d), row), pl.BlockSpec((tm, d), row),
                   pl.BlockSpec((tm, LANES), row),
                   pair_spec, v16_spec],
        compiler_params=_cparams(("parallel",), vmem),
        name="shared_kv",
    )(x, g.reshape(1, d), wk, wv, wf, bf, gk, gsum, gexp)


def _fox_q_body(x_ref, g_ref, wq_ref, wg_ref, gq_ref, gsum_ref, gexp_ref, q16_ref, sg_ref):
    xn = _rms(x_ref[...], g_ref[...]).astype(BF16)
    q = _head_norm(_dot(xn, wq_ref[...]), gsum_ref, gexp_ref, gq_ref[...], FOX_HEAD_DIM)
    q16_ref[...] = (q * (FOX_HEAD_DIM ** -0.5 * LOG2E)).astype(BF16)
    sg_ref[...] = jax.nn.sigmoid(_dot(xn, wg_ref[...])).astype(sg_ref.dtype)


def _fox_q(x, g, wq, wg, gq, gsum, gexp, *, tm):
    n, d = x.shape
    tm = min(tm, n)
    assert n % tm == 0
    row = lambda i: (i, 0)
    fix = lambda i: (0, 0)
    vmem = 2 * tm * d * (4 + 2 + 4) + 2 * 2 * (2 * d * d + 2 * d * LANES) + 6 * tm * d * 4
    return pl.pallas_call(
        _fox_q_body,
        out_shape=[jax.ShapeDtypeStruct((n, d), BF16), jax.ShapeDtypeStruct((n, d), BF16)],
        grid=(n // tm,),
        in_specs=[
            pl.BlockSpec((tm, d), row), pl.BlockSpec((1, d), fix),
            pl.BlockSpec((d, d), fix), pl.BlockSpec((d, d), fix), pl.BlockSpec((1, d), fix),
            pl.BlockSpec((d, LANES), fix), pl.BlockSpec((LANES, d), fix),
        ],
        out_specs=[pl.BlockSpec((tm, d), row), pl.BlockSpec((tm, d), row)],
        compiler_params=_cparams(("parallel",), vmem),
        name="fox_q",
    )(x, g.reshape(1, d), wq, wg, gq, gsum, gexp)


def _pack_parts(x):
    p1, p2, p3 = (p.astype(F32) for p in _split3(x))
    packed = p1 + pltpu.roll(p2, FOX_HEADS, 1) + pltpu.roll(p3, 2 * FOX_HEADS, 1)
    return packed.astype(BF16)


def _bias_body(lf_ref, tri_ref, sel_ref, one_ref, qb_ref, kb_ref, carry_ref):
    @pl.when(pl.program_id(1) == 0)
    def _():
        carry_ref[...] = jnp.zeros_like(carry_ref)

    lf = lf_ref[...]
    tb = lf.shape[0]
    head_lanes = lax.broadcasted_iota(jnp.int32, lf.shape, 1) < FOX_HEADS
    cp = _dot(tri_ref[...], _pack_parts(jnp.where(head_lanes, lf, 0.0)))
    c = cp + pltpu.roll(cp, LANES - FOX_HEADS, 1) + pltpu.roll(cp, LANES - 2 * FOX_HEADS, 1)
    c = jnp.where(head_lanes, c, 0.0) + carry_ref[...]
    carry_ref[...] = c[tb - 1:tb, :]
    qkb = _dot(_pack_parts(c * LOG2E), sel_ref[...]) + one_ref[...]
    wide = FOX_PAIRS * LANES
    for p in range(FOX_PAIRS):
        qb_ref[p] = qkb[:, p * LANES:(p + 1) * LANES].astype(BF16)
        kb_ref[p] = qkb[:, wide + p * LANES:wide + (p + 1) * LANES].astype(BF16)


def _bias_constants():
    wide = FOX_PAIRS * LANES
    sel = np.zeros((LANES, 2 * wide), np.float32)
    one = np.zeros((1, 2 * wide), np.float32)
    for p in range(FOX_PAIRS):
        for e in range(2):
            base = p * LANES + e * BIAS_LANES_PER_HEAD
            for part in range(BIAS_PARTS):
                row = part * FOX_HEADS + 2 * p + e
                sel[row, base + part] = 1.0
                one[0, base + BIAS_PARTS + part] = 1.0
                one[0, wide + base + part] = 1.0
                sel[row, wide + base + BIAS_PARTS + part] = -1.0
    return jnp.asarray(sel, BF16), jnp.asarray(one)


def _fox_bias(lf, *, tb):
    bsz, kp, _ = lf.shape
    assert kp % tb == 0
    sel, one = _bias_constants()
    tri = jnp.tril(jnp.ones((tb, tb), F32)).astype(BF16)
    wide = FOX_PAIRS * LANES
    out_spec = pl.BlockSpec((None, FOX_PAIRS, tb, LANES), lambda b, i: (b, 0, i, 0))
    vmem = 2 * tb * LANES * 4 + 2 * tb * tb * 2 + 2 * LANES * 2 * wide * 2 \
        + 4 * FOX_PAIRS * tb * LANES * 2 + 4 * tb * 2 * wide * 4
    return pl.pallas_call(
        _bias_body,
        out_shape=[jax.ShapeDtypeStruct((bsz, FOX_PAIRS, kp, LANES), BF16)] * 2,
        grid=(bsz, kp // tb),
        in_specs=[
            pl.BlockSpec((None, tb, LANES), lambda b, i: (b, i, 0)),
            pl.BlockSpec((tb, tb), lambda b, i: (0, 0)),
            pl.BlockSpec((LANES, 2 * wide), lambda b, i: (0, 0)),
            pl.BlockSpec((1, 2 * wide), lambda b, i: (0, 0)),
        ],
        out_specs=[out_spec, out_spec],
        scratch_shapes=[pltpu.VMEM((1, LANES), F32)],
        compiler_params=_cparams(("parallel", "arbitrary"), vmem),
        name="fox_bias",
    )(lf, tri, sel, one)


def _attn_body(q_ref, qb_ref, k_ref, kb_ref, v_ref, sg_ref, o_ref, m_ref, l_ref, acc_ref,
               *, tq, tk, off, keys_major):
    k_axis = 0 if keys_major else 1
    i = pl.program_id(2)
    lane = lax.broadcasted_iota(jnp.int32, (tq, LANES), 1)
    q = q_ref[...]
    qb = qb_ref[...]
    zero = jnp.zeros_like(q)
    q_aug = []
    for e in range(2):
        head_lanes = (lane >> 6) == e
        bias_lanes = jnp.logical_and(lane >= e * BIAS_LANES_PER_HEAD,
                                     lane < (e + 1) * BIAS_LANES_PER_HEAD)
        q_aug.append(jnp.concatenate(
            [jnp.where(head_lanes, q, zero), jnp.where(bias_lanes, qb, zero)], axis=1))

    m_ref[...] = jnp.full(m_ref.shape, NEG_INF, F32)
    l_ref[...] = jnp.zeros(l_ref.shape, F32)
    acc_ref[...] = jnp.zeros(acc_ref.shape, F32)
    q_lo = i * tq + off

    def step(j, masked):
        ks = pl.multiple_of(j * tk, tk)
        k_aug = jnp.concatenate([k_ref[pl.ds(ks, tk), :], kb_ref[pl.ds(ks, tk), :]], axis=1)
        v = v_ref[j] if keys_major else v_ref[pl.ds(ks, tk), :]
        if keys_major:
            scores = [lax.dot_general(k_aug, q_aug[e], NT_DIMS, preferred_element_type=F32)
                      for e in range(2)]
        else:
            scores = [lax.dot_general(q_aug[e], k_aug, NT_DIMS, preferred_element_type=F32)
                      for e in range(2)]
        for e in range(2):
            s = scores[e]
            if masked:
                qpos = q_lo + lax.broadcasted_iota(jnp.int32, s.shape, 1 - k_axis)
                kpos = ks + lax.broadcasted_iota(jnp.int32, s.shape, k_axis)
                s = jnp.where(kpos <= qpos, s, NEG_INF)
            m_prev = m_ref[e]
            m_new = jnp.maximum(m_prev, jnp.max(s, axis=k_axis, keepdims=True))
            alpha = jnp.exp2(m_prev - m_new)
            p = jnp.exp2(s - m_new)
            l_ref[e] = alpha * l_ref[e] + jnp.sum(p, axis=k_axis, keepdims=True)
            p16 = p.astype(BF16)
            acc_ref[e] = alpha * acc_ref[e] + (_dot(v, p16) if keys_major else _dot(p16, v))
            m_ref[e] = m_new

    n_full = (q_lo + 1) // tk
    n_all = (q_lo + tq + tk - 1) // tk

    def full_step(j, carry):
        step(j, False)
        return carry

    def masked_step(j, carry):
        step(j, True)
        return carry

    lax.fori_loop(0, n_full, full_step, 0)
    lax.fori_loop(n_full, n_all, masked_step, 0)

    o0 = acc_ref[0] * (1.0 / l_ref[0])
    o1 = acc_ref[1] * (1.0 / l_ref[1])
    if keys_major:
        feature = lax.broadcasted_iota(jnp.int32, (LANES, tq), 0)
        o = jnp.where((feature >> 6) == 0, o0, o1).T
    else:
        o = jnp.where((lane >> 6) == 0, o0, o1)
    o_ref[...] = (o * sg_ref[...].astype(F32)).astype(BF16)


def _fox_attn(q16, qb, k16, kb, v16, sg, *, tq, tk, off, keys_major):
    bsz, t, d = q16.shape
    kp = k16.shape[2]
    assert t % tq == 0 and kp % tk == 0 and off % tq == 0 and off + t <= kp
    off_blocks = off // tq
    qspec = pl.BlockSpec((None, tq, LANES), lambda b, p, i: (b, i, p))
    kspec = pl.BlockSpec((None, None, kp, LANES), lambda b, p, i: (p, b, 0, 0))
    if keys_major:
        vspec = pl.BlockSpec((kp // tk, LANES, tk), lambda b, p, i: (b, p, 0))
        stat_shape, acc_shape = (2, 1, tq), (2, LANES, tq)
    else:
        vspec = kspec
        stat_shape, acc_shape = (2, tq, 1), (2, tq, LANES)
    vmem = 2 * (3 * kp * LANES * 2 + tq * LANES * (2 + 2 + 4 + 2)) + 6 * tq * LANES * 4 \
        + 2 * tk * 2 * LANES * 2 + 6 * tq * tk * 4
    return pl.pallas_call(
        functools.partial(_attn_body, tq=tq, tk=tk, off=off, keys_major=keys_major),
        out_shape=jax.ShapeDtypeStruct((bsz, t, d), BF16),
        grid=(bsz, FOX_PAIRS, t // tq),
        in_specs=[
            qspec,
            pl.BlockSpec((None, None, tq, LANES), lambda b, p, i: (b, p, i + off_blocks, 0)),
            kspec,
            pl.BlockSpec((None, None, kp, LANES), lambda b, p, i: (b, p, 0, 0)),
            vspec,
            qspec,
        ],
        out_specs=qspec,
        scratch_shapes=[pltpu.VMEM(stat_shape, F32), pltpu.VMEM(stat_shape, F32),
                        pltpu.VMEM(acc_shape, F32)],
        compiler_params=_cparams(("parallel", "parallel", "arbitrary"), vmem + (8 << 20)),
        name="fox_attn",
    )(q16, qb, k16, kb, v16, sg)


TM = 1024
FFN_TF = 256
REPACK_ROWS = 128
GLA_TB = 256
ATTN_TQ = 512
ATTN_TK = 1024
ATTN_TK_PAST = 1408
BIAS_TB_CHOICES = (512, 384, 256, 128)


def _round_up(x, m):
    return (x + m - 1) // m * m


def _prep_weights(ffn_norm, w_ffn_gu, w_ffn_down, mix_norm, a_w_in, a_w_g2, a_b_g, a_g_out, a_w_o,
                  kv_norm, w_kvf, b_f, g_k, b_w_qg, b_g_q, b_w_o):
    d = w_ffn_gu.shape[2]
    rank = a_w_g2.shape[1]
    qk = a_w_g2.shape[2]
    vw = a_w_o.shape[1]
    w_in = a_w_in[0]
    pad_cols = lambda w: jnp.pad(w, ((0, 0), (0, LANES - w.shape[1])))
    groups = np.arange(d) // FOX_HEAD_DIM
    gsum = (groups[:, None] == np.arange(LANES)[None, :]).astype(np.float32)
    return dict(
        ffn_norm=ffn_norm, mix_norm=mix_norm, kv_norm=kv_norm,
        w_gu=_tile_cols(w_ffn_gu.reshape((-1,) + w_ffn_gu.shape[2:]), FFN_TF, rows=REPACK_ROWS),
        w_down=w_ffn_down.astype(BF16),
        a_wq=w_in[:, :qk].astype(BF16), a_wk=w_in[:, qk:2 * qk].astype(BF16),
        a_wv=w_in[:, 2 * qk:2 * qk + vw].astype(BF16),
        a_wr=w_in[:, 2 * qk + vw:2 * qk + 2 * vw].astype(BF16),
        a_wgl=pad_cols(w_in[:, 2 * qk + 2 * vw:]).astype(BF16),
        a_wg2=jnp.pad(a_w_g2[0], ((0, LANES - rank), (0, 0))).astype(BF16),
        a_bg=a_b_g[0], a_gout=a_g_out[0], a_wo=a_w_o[0].astype(BF16),
        kv_wk=w_kvf[:, :d].astype(BF16), kv_wv=w_kvf[:, d:2 * d].astype(BF16),
        kv_wv_t=w_kvf[:, d:2 * d].T.astype(BF16),
        kv_wf=pad_cols(w_kvf[:, 2 * d:]).astype(BF16),
        kv_bf=jnp.pad(b_f, (0, LANES - b_f.shape[0])).reshape(1, LANES),
        gk=jnp.tile(g_k, FOX_HEADS).reshape(1, d), gq=jnp.tile(b_g_q[0], FOX_HEADS).reshape(1, d),
        b_wq=b_w_qg[0][:, :d].astype(BF16), b_wg=b_w_qg[0][:, d:].astype(BF16),
        b_wo=b_w_o[0].astype(BF16),
        gsum=jnp.asarray(gsum, BF16), gexp=jnp.asarray(gsum.T, BF16),
    )


def _trunk(x, s0, past, w):
    bsz, t, d = x.shape
    n = bsz * t
    h = x.reshape(n, d)
    ffn = lambda h_, layer, half: _ffn(h_, w["ffn_norm"][layer, half], w["w_gu"][2 * layer + half],
                                       w["w_down"][layer, half], tm=TM)
    h = ffn(h, 0, 0)
    q, k, v, r, la = _gla_in(h, w["mix_norm"][0], w["a_wq"], w["a_wk"], w["a_wv"], w["a_wr"],
                             w["a_wgl"], w["a_wg2"], w["a_bg"], tm=TM)
    s3 = lambda z: z.reshape(bsz, t, z.shape[1])
    chunk = 64
    og, st_fin = _gla(s3(q), s3(k), s3(v), s3(r), s3(la), jnp.swapaxes(s0, 2, 3), w["a_gout"],
                      chunk=chunk, tb=GLA_TB)
    h = _proj_res(h, og.reshape(n, -1), w["a_wo"], tm=TM)
    h = ffn(h, 0, 1)
    keys_major = past is None
    tq, tk = min(ATTN_TQ, t), min(ATTN_TK, t)
    k_new, v_new, lf, k16, v16 = _shared_kv(
        h, w["kv_norm"], w["kv_wk"], w["kv_wv_t"] if keys_major else w["kv_wv"], w["kv_wf"],
        w["kv_bf"], w["gk"], w["gsum"], w["gexp"], tm=TM, vt_block=tk if keys_major else None)
    h = ffn(h, 1, 0)
    q16, sg = _fox_q(h, w["mix_norm"][1], w["b_wq"], w["b_wg"], w["gq"], w["gsum"], w["gexp"], tm=TM)
    pairs4 = lambda z: z.reshape(FOX_PAIRS, bsz, t, LANES)
    if past is None:
        k_all, v_all, lf_all = pairs4(k16), v16, s3(lf)
        off = 0
    else:
        past_k, past_v, past_lf = past
        p_len = past_k.shape[1]
        total = p_len + t
        tk = ATTN_TK_PAST if total >= ATTN_TK_PAST else _round_up(total, LANES)
        pad = _round_up(total, tk) - total
        past_pairs = lambda a: jnp.transpose(
            a.astype(BF16).reshape(bsz, p_len, FOX_PAIRS, LANES), (2, 0, 1, 3))
        cat_keys = lambda a, b_: jnp.concatenate(
            [a, b_, jnp.zeros((FOX_PAIRS, bsz, pad, LANES), a.dtype)], axis=2)
        k_all = cat_keys(past_pairs(past_k), pairs4(k16))
        v_all = cat_keys(past_pairs(past_v), pairs4(v16))
        lf_all = jnp.concatenate(
            [jnp.pad(past_lf, ((0, 0), (0, 0), (0, LANES - past_lf.shape[2]))), s3(lf),
             jnp.zeros((bsz, pad, LANES), F32)], axis=1)
        off, tq = p_len, t
    kp = k_all.shape[2]
    bias_tb = next(c for c in BIAS_TB_CHOICES if kp % c == 0)
    qb, kb = _fox_bias(lf_all, tb=bias_tb)
    og = _fox_attn(s3(q16), qb, k_all, kb, v_all, s3(sg), tq=tq, tk=tk, off=off,
                   keys_major=keys_major)
    h = _proj_res(h, og.reshape(n, d), w["b_wo"], tm=TM)
    h = ffn(h, 1, 1)
    heads4 = lambda z: z.reshape(bsz, t, FOX_HEADS, FOX_HEAD_DIM)
    return (h.reshape(bsz, t, d), jnp.swapaxes(st_fin, 2, 3)[:, None], heads4(k_new), heads4(v_new),
            lf[:, :FOX_HEADS].reshape(bsz, t, FOX_HEADS))


def kernel(x_prompt, x_sample, state_gla, cache_k, cache_v, cache_logf, ffn_norm, w_ffn_gu, w_ffn_down, mix_norm, a_w_in, a_w_g2, a_b_g, a_g_out, a_w_o, kv_norm, w_kvf, b_f, g_k, b_w_qg, b_g_q, b_w_o):
    w = _prep_weights(ffn_norm, w_ffn_gu, w_ffn_down, mix_norm, a_w_in, a_w_g2, a_b_g, a_g_out,
                      a_w_o, kv_norm, w_kvf, b_f, g_k, b_w_qg, b_g_q, b_w_o)
    s0_prompt = jnp.zeros((x_prompt.shape[0],) + state_gla.shape[2:], F32)
    y_p, gla_p, k_p, v_p, lf_p = _trunk(x_prompt, s0_prompt, None, w)
    y_s, gla_s, k_s, v_s, lf_s = _trunk(x_sample, state_gla[:, 0], (cache_k, cache_v, cache_logf), w)
    return (y_p, y_s, gla_p, gla_s, k_p, v_p, lf_p, k_s, v_s, lf_s)
```

```python
import functools

import jax
import jax.numpy as jnp
import numpy as np
from jax import lax
from jax.experimental import pallas as pl
from jax.experimental.pallas import tpu as pltpu

F32 = jnp.float32
BF16 = jnp.bfloat16

EPS = 1e-6
NEG_INF = -1e30
LOG2E = 1.4426950408889634

LANES = 128
V7X_VMEM_BYTES = 64 * 1024 * 1024
V7X_SCOPED_VMEM_CAP = 60000 * 1024

GLA_HEADS = 4
GLA_GATE_TAU = 16.0
FOX_HEADS = 16
FOX_HEAD_DIM = 64
FOX_PAIRS = FOX_HEADS // 2
BIAS_PARTS = 3
BIAS_LANES_PER_HEAD = 2 * BIAS_PARTS

NT_DIMS = (((1,), (1,)), ((), ()))
TN_DIMS = (((0,), (0,)), ((), ()))


def _cparams(semantics, vmem_bytes):
    limit = int(min(max(vmem_bytes, 16 * 1024 * 1024), V7X_SCOPED_VMEM_CAP))
    return pltpu.CompilerParams(dimension_semantics=semantics, vmem_limit_bytes=limit)


def _dot(a, b):
    return jnp.dot(a, b, preferred_element_type=F32)


def _rms(x, g):
    ms = jnp.mean(x * x, axis=-1, keepdims=True)
    return x * lax.rsqrt(ms + EPS) * g


def _split2(x):
    hi = x.astype(BF16)
    lo = (x - hi.astype(F32)).astype(BF16)
    return hi, lo


def _split3(x):
    p1 = x.astype(BF16)
    r1 = x - p1.astype(F32)
    p2 = r1.astype(BF16)
    p3 = (r1 - p2.astype(F32)).astype(BF16)
    return p1, p2, p3


def _log_sigmoid(x):
    return jnp.minimum(x, 0.0) - jnp.log(1.0 + jnp.exp(-jnp.abs(x)))


def _head_norm(x, gsum_ref, gexp_ref, gain, head_dim):
    ssq = _dot((x * x).astype(BF16), gsum_ref[...])
    inv = lax.rsqrt(ssq * (1.0 / head_dim) + EPS)
    ihi, ilo = _split2(inv)
    inv_full = _dot(ihi, gexp_ref[...]) + _dot(ilo, gexp_ref[...])
    return x * inv_full * gain


def _ffn_body(x_ref, g_ref, wg_ref, wu_ref, wd_ref, o_ref, xn_ref):
    j = pl.program_id(1)

    @pl.when(j == 0)
    def _():
        xn_ref[...] = _rms(x_ref[...], g_ref[...]).astype(BF16)
        o_ref[...] = jnp.zeros_like(o_ref)

    xn = xn_ref[...]
    g = _dot(xn, wg_ref[...])
    u = _dot(xn, wu_ref[...])
    h = (g * jax.nn.sigmoid(g) * u).astype(BF16)
    o_ref[...] += _dot(h, wd_ref[...])

    @pl.when(j == pl.num_programs(1) - 1)
    def _():
        o_ref[...] = x_ref[...] + 0.5 * o_ref[...]


def _tile_cols_body(w_ref, o_ref):
    tf = o_ref.shape[2]
    for c in range(o_ref.shape[0]):
        o_ref[c] = w_ref[:, c * tf:(c + 1) * tf].astype(o_ref.dtype)


def _tile_cols(w, tf, *, rows):
    g, d, n = w.shape
    assert n % tf == 0 and d % rows == 0
    return pl.pallas_call(
        _tile_cols_body,
        out_shape=jax.ShapeDtypeStruct((g, n // tf, d, tf), BF16),
        grid=(g, d // rows),
        in_specs=[pl.BlockSpec((None, rows, n), lambda a, r: (a, r, 0))],
        out_specs=pl.BlockSpec((None, n // tf, rows, tf), lambda a, r: (a, 0, r, 0)),
        compiler_params=_cparams(("parallel", "parallel"), 2 * rows * n * (4 + 2) + (4 << 20)),
        name="tile_cols",
    )(w)


def _ffn(x, g, w_gu, w_down, *, tm):
    n, d = x.shape
    d_ff = w_down.shape[0]
    tf = w_gu.shape[2]
    tm = min(tm, n)
    nf = d_ff // tf
    assert n % tm == 0 and w_gu.shape[0] == 2 * nf
    vmem = 4 * tm * d * 4 + tm * d * 2 + 2 * 3 * d * tf * 2 + 3 * tm * tf * 4 + tm * d * 4
    return pl.pallas_call(
        _ffn_body,
        out_shape=jax.ShapeDtypeStruct((n, d), F32),
        grid=(n // tm, nf),
        in_specs=[
            pl.BlockSpec((tm, d), lambda i, j: (i, 0)),
            pl.BlockSpec((1, d), lambda i, j: (0, 0)),
            pl.BlockSpec((None, d, tf), lambda i, j: (j, 0, 0)),
            pl.BlockSpec((None, d, tf), lambda i, j: (j + nf, 0, 0)),
            pl.BlockSpec((tf, d), lambda i, j: (j, 0)),
        ],
        out_specs=pl.BlockSpec((tm, d), lambda i, j: (i, 0)),
        scratch_shapes=[pltpu.VMEM((tm, d), BF16)],
        compiler_params=_cparams(("parallel", "arbitrary"), vmem + (8 << 20)),
        name="ffn",
    )(x, g.reshape(1, d), w_gu, w_gu, w_down)


def _proj_res_body(h_ref, a_ref, w_ref, o_ref):
    o_ref[...] = h_ref[...] + _dot(a_ref[...], w_ref[...])


def _proj_res(h, a, w, *, tm):
    n, d = h.shape
    k = a.shape[1]
    tm = min(tm, n)
    assert n % tm == 0
    vmem = 2 * (2 * tm * d * 4 + tm * k * 2 + k * d * 2) + tm * d * 4
    return pl.pallas_call(
        _proj_res_body,
        out_shape=jax.ShapeDtypeStruct((n, d), F32),
        grid=(n // tm,),
        in_specs=[
            pl.BlockSpec((tm, d), lambda i: (i, 0)),
            pl.BlockSpec((tm, k), lambda i: (i, 0)),
            pl.BlockSpec((k, d), lambda i: (0, 0)),
        ],
        out_specs=pl.BlockSpec((tm, d), lambda i: (i, 0)),
        compiler_params=_cparams(("parallel",), vmem + (4 << 20)),
        name="proj_res",
    )(h, a, w)


def _gla_in_body(x_ref, g_ref, wq_ref, wk_ref, wv_ref, wr_ref, wgl_ref, wg2_ref, bg_ref,
                 q_ref, k_ref, v_ref, r_ref, la_ref):
    xn = _rms(x_ref[...], g_ref[...]).astype(BF16)
    q_ref[...] = _dot(xn, wq_ref[...]).astype(q_ref.dtype)
    k_ref[...] = _dot(xn, wk_ref[...]).astype(k_ref.dtype)
    v_ref[...] = _dot(xn, wv_ref[...]).astype(v_ref.dtype)
    r_ref[...] = _dot(xn, wr_ref[...]).astype(r_ref.dtype)
    gl = _dot(xn, wgl_ref[...]).astype(BF16)
    z = _dot(gl, wg2_ref[...]) + bg_ref[...]
    la_ref[...] = _log_sigmoid(z) * (1.0 / GLA_GATE_TAU)


def _gla_in(x, g, wq, wk, wv, wr, wgl, wg2, bg, *, tm):
    n, d = x.shape
    qk, vw = wq.shape[1], wv.shape[1]
    tm = min(tm, n)
    assert n % tm == 0
    row = lambda i: (i, 0)
    fix = lambda i: (0, 0)
    w_bytes = 2 * (2 * d * qk + 2 * d * vw + d * LANES + LANES * qk)
    vmem = 2 * tm * (d + 3 * qk + 2 * vw) * 4 + 2 * w_bytes + tm * d * 2
    return pl.pallas_call(
        _gla_in_body,
        out_shape=[jax.ShapeDtypeStruct((n, qk), BF16), jax.ShapeDtypeStruct((n, qk), BF16),
                   jax.ShapeDtypeStruct((n, vw), BF16), jax.ShapeDtypeStruct((n, vw), BF16),
                   jax.ShapeDtypeStruct((n, qk), F32)],
        grid=(n // tm,),
        in_specs=[
            pl.BlockSpec((tm, d), row), pl.BlockSpec((1, d), fix),
            pl.BlockSpec((d, qk), fix), pl.BlockSpec((d, qk), fix),
            pl.BlockSpec((d, vw), fix), pl.BlockSpec((d, vw), fix),
            pl.BlockSpec((d, LANES), fix), pl.BlockSpec((LANES, qk), fix), pl.BlockSpec((1, qk), fix),
        ],
        out_specs=[pl.BlockSpec((tm, qk), row), pl.BlockSpec((tm, qk), row),
                   pl.BlockSpec((tm, vw), row), pl.BlockSpec((tm, vw), row),
                   pl.BlockSpec((tm, qk), row)],
        compiler_params=_cparams(("parallel",), vmem + (8 << 20)),
        name="gla_in",
    )(x, g.reshape(1, d), wq, wk, wv, wr, wgl, wg2, bg.reshape(1, qk))


def _gla_body(q_ref, k_ref, v_ref, r_ref, la_ref, s0_ref, gout_ref, tri_ref,
              og_ref, sfin_ref, st_ref, *, chunk, n_chunks, heads, dk, dv):
    t = pl.program_id(1)

    @pl.when(t == 0)
    def _():
        st_ref[...] = s0_ref[...]

    scale = dk ** -0.5
    row = lax.broadcasted_iota(jnp.int32, (chunk, chunk), 0)
    col = lax.broadcasted_iota(jnp.int32, (chunk, chunk), 1)
    causal = col <= row
    tri = tri_ref[...]
    gout = gout_ref[...]

    for c in range(n_chunks):
        sl = slice(c * chunk, (c + 1) * chunk)
        la_hi, la_lo = _split2(la_ref[sl, :])
        b = _dot(tri, la_hi) + _dot(tri, la_lo)
        b_last = b[chunk - 1:chunk, :]
        q = q_ref[sl, :].astype(F32)
        k = k_ref[sl, :].astype(F32)
        qe = (q * scale * jnp.exp(b)).astype(BF16)
        ke = (k * jnp.exp(-b)).astype(BF16)
        kd = (k * jnp.exp(b_last - b)).astype(BF16)
        dec = jnp.exp(b_last)
        for h in range(heads):
            ks = slice(h * dk, (h + 1) * dk)
            vs = slice(h * dv, (h + 1) * dv)
            vh = v_ref[sl, vs]
            att = lax.dot_general(qe[:, ks], ke[:, ks], NT_DIMS, preferred_element_type=F32)
            att = jnp.where(causal, att, 0.0).astype(BF16)
            st = st_ref[h]
            o = _dot(att, vh) + lax.dot_general(qe[:, ks], st.astype(BF16), NT_DIMS,
                                                preferred_element_type=F32)
            st_ref[h] = st * dec[:, ks] + lax.dot_general(vh, kd[:, ks], TN_DIMS,
                                                          preferred_element_type=F32)
            on = _rms(o, gout)
            rh = r_ref[sl, vs].astype(F32)
            og_ref[sl, vs] = (on * (rh * jax.nn.sigmoid(rh))).astype(BF16)

    @pl.when(t == pl.num_programs(1) - 1)
    def _():
        sfin_ref[...] = st_ref[...]


def _gla(q, k, v, r, la, s0t, gout, *, chunk, tb):
    bsz, t, qk = q.shape
    vw = v.shape[2]
    heads = s0t.shape[1]
    dk, dv = qk // heads, vw // heads
    tb = min(tb, t)
    assert t % tb == 0 and tb % chunk == 0
    tri = jnp.tril(jnp.ones((chunk, chunk), F32)).astype(BF16)
    blk = lambda w: pl.BlockSpec((None, tb, w), lambda b, i: (b, i, 0))
    st_spec = pl.BlockSpec((None, heads, dv, dk), lambda b, i: (b, 0, 0, 0))
    vmem = 2 * tb * (3 * qk + 2 * vw) * 4 + 2 * tb * vw * 2 + 5 * heads * dv * dk * 4
    return pl.pallas_call(
        functools.partial(_gla_body, chunk=chunk, n_chunks=tb // chunk, heads=heads, dk=dk, dv=dv),
        out_shape=[jax.ShapeDtypeStruct((bsz, t, vw), BF16),
                   jax.ShapeDtypeStruct((bsz, heads, dv, dk), F32)],
        grid=(bsz, t // tb),
        in_specs=[blk(qk), blk(qk), blk(vw), blk(vw), blk(qk), st_spec,
                  pl.BlockSpec((1, dv), lambda b, i: (0, 0)),
                  pl.BlockSpec((chunk, chunk), lambda b, i: (0, 0))],
        out_specs=[blk(vw), st_spec],
        scratch_shapes=[pltpu.VMEM((heads, dv, dk), F32)],
        compiler_params=_cparams(("parallel", "arbitrary"), vmem + (8 << 20)),
        name="gla_chunks",
    )(q, k, v, r, la, s0t, gout.reshape(1, dv), tri)


def _store_pairs(ref, x):
    for p in range(ref.shape[0]):
        ref[p] = x[:, p * LANES:(p + 1) * LANES]


def _kv_body(x_ref, g_ref, wk_ref, wv_ref, wf_ref, bf_ref, gk_ref, gsum_ref, gexp_ref,
             k_ref, v_ref, lf_ref, k16_ref, v16_ref, *, vt_block):
    xn = _rms(x_ref[...], g_ref[...]).astype(BF16)
    k = _head_norm(_dot(xn, wk_ref[...]), gsum_ref, gexp_ref, gk_ref[...], FOX_HEAD_DIM)
    k_ref[...] = k
    _store_pairs(k16_ref, k.astype(BF16))
    lf_ref[...] = _log_sigmoid(_dot(xn, wf_ref[...]) + bf_ref[...])
    if vt_block is None:
        v = _dot(xn, wv_ref[...])
        v_ref[...] = v
        _store_pairs(v16_ref, v.astype(BF16))
    else:
        vt = lax.dot_general(wv_ref[...], xn, NT_DIMS, preferred_element_type=F32)
        v_ref[...] = vt.T
        vt16 = vt.astype(BF16)
        for c in range(v16_ref.shape[0]):
            v16_ref[c] = vt16[:, c * vt_block:(c + 1) * vt_block]


def _shared_kv(x, g, wk, wv, wf, bf, gk, gsum, gexp, *, tm, vt_block=None):
    n, d = x.shape
    tm = min(tm, n)
    assert n % tm == 0
    row = lambda i: (i, 0)
    fix = lambda i: (0, 0)
    pair_shape = (d // LANES, n, LANES)
    pair_spec = pl.BlockSpec((d // LANES, tm, LANES), lambda i: (0, i, 0))
    if vt_block is None:
        v16_shape, v16_spec = pair_shape, pair_spec
    else:
        assert tm % vt_block == 0
        v16_shape = (n // vt_block, d, vt_block)
        v16_spec = pl.BlockSpec((tm // vt_block, d, vt_block), lambda i: (i, 0, 0))
    vmem = 2 * tm * d * (4 + 4 + 4 + 2 + 2) + 2 * tm * LANES * 4 + 2 * 2 * (2 * d * d + 3 * d * LANES) \
        + 6 * tm * d * 4
    return pl.pallas_call(
        functools.partial(_kv_body, vt_block=vt_block),
        out_shape=[jax.ShapeDtypeStruct((n, d), F32), jax.ShapeDtypeStruct((n, d), F32),
                   jax.ShapeDtypeStruct((n, LANES), F32),
                   jax.ShapeDtypeStruct(pair_shape, BF16), jax.ShapeDtypeStruct(v16_shape, BF16)],
        grid=(n // tm,),
        in_specs=[
            pl.BlockSpec((tm, d), row), pl.BlockSpec((1, d), fix),
            pl.BlockSpec((d, d), fix), pl.BlockSpec((d, d), fix), pl.BlockSpec((d, LANES), fix),
            pl.BlockSpec((1, LANES), fix), pl.BlockSpec((1, d), fix),
            pl.BlockSpec((d, LANES), fix), pl.BlockSpec((LANES, d), fix),
        ],
        out_specs=[pl.BlockSpec((tm, d), row), pl.BlockSpec((tm, d), row),
                   pl.BlockSpec((tm, LANES), row),
                   pair_spec, v16_spec],
        compiler_params=_cparams(("parallel",), vmem),
        name="shared_kv",
    )(x, g.reshape(1, d), wk, wv, wf, bf, gk, gsum, gexp)


def _fox_q_body(x_ref, g_ref, wq_ref, wg_ref, gq_ref, gsum_ref, gexp_ref, q16_ref, sg_ref):
    xn = _rms(x_ref[...], g_ref[...]).astype(BF16)
    q = _head_norm(_dot(xn, wq_ref[...]), gsum_ref, gexp_ref, gq_ref[...], FOX_HEAD_DIM)
    q16_ref[...] = (q * (FOX_HEAD_DIM ** -0.5 * LOG2E)).astype(BF16)
    sg_ref[...] = jax.nn.sigmoid(_dot(xn, wg_ref[...])).astype(sg_ref.dtype)


def _fox_q(x, g, wq, wg, gq, gsum, gexp, *, tm):
    n, d = x.shape
    tm = min(tm, n)
    assert n % tm == 0
    row = lambda i: (i, 0)
    fix = lambda i: (0, 0)
    vmem = 2 * tm * d * (4 + 2 + 4) + 2 * 2 * (2 * d * d + 2 * d * LANES) + 6 * tm * d * 4
    return pl.pallas_call(
        _fox_q_body,
        out_shape=[jax.ShapeDtypeStruct((n, d), BF16), jax.ShapeDtypeStruct((n, d), BF16)],
        grid=(n // tm,),
        in_specs=[
            pl.BlockSpec((tm, d), row), pl.BlockSpec((1, d), fix),
            pl.BlockSpec((d, d), fix), pl.BlockSpec((d, d), fix), pl.BlockSpec((1, d), fix),
            pl.BlockSpec((d, LANES), fix), pl.BlockSpec((LANES, d), fix),
        ],
        out_specs=[pl.BlockSpec((tm, d), row), pl.BlockSpec((tm, d), row)],
        compiler_params=_cparams(("parallel",), vmem),
        name="fox_q",
    )(x, g.reshape(1, d), wq, wg, gq, gsum, gexp)


def _pack_parts(x):
    p1, p2, p3 = (p.astype(F32) for p in _split3(x))
    packed = p1 + pltpu.roll(p2, FOX_HEADS, 1) + pltpu.roll(p3, 2 * FOX_HEADS, 1)
    return packed.astype(BF16)


def _bias_body(lf_ref, tri_ref, sel_ref, one_ref, qb_ref, kb_ref, carry_ref):
    @pl.when(pl.program_id(1) == 0)
    def _():
        carry_ref[...] = jnp.zeros_like(carry_ref)

    lf = lf_ref[...]
    tb = lf.shape[0]
    head_lanes = lax.broadcasted_iota(jnp.int32, lf.shape, 1) < FOX_HEADS
    cp = _dot(tri_ref[...], _pack_parts(jnp.where(head_lanes, lf, 0.0)))
    c = cp + pltpu.roll(cp, LANES - FOX_HEADS, 1) + pltpu.roll(cp, LANES - 2 * FOX_HEADS, 1)
    c = jnp.where(head_lanes, c, 0.0) + carry_ref[...]
    carry_ref[...] = c[tb - 1:tb, :]
    qkb = _dot(_pack_parts(c * LOG2E), sel_ref[...]) + one_ref[...]
    wide = FOX_PAIRS * LANES
    for p in range(FOX_PAIRS):
        qb_ref[p] = qkb[:, p * LANES:(p + 1) * LANES].astype(BF16)
        kb_ref[p] = qkb[:, wide + p * LANES:wide + (p + 1) * LANES].astype(BF16)


def _bias_constants():
    wide = FOX_PAIRS * LANES
    sel = np.zeros((LANES, 2 * wide), np.float32)
    one = np.zeros((1, 2 * wide), np.float32)
    for p in range(FOX_PAIRS):
        for e in range(2):
            base = p * LANES + e * BIAS_LANES_PER_HEAD
            for part in range(BIAS_PARTS):
                row = part * FOX_HEADS + 2 * p + e
                sel[row, base + part] = 1.0
                one[0, base + BIAS_PARTS + part] = 1.0
                one[0, wide + base + part] = 1.0
                sel[row, wide + base + BIAS_PARTS + part] = -1.0
    return jnp.asarray(sel, BF16), jnp.asarray(one)


def _fox_bias(lf, *, tb):
    bsz, kp, _ = lf.shape
    assert kp % tb == 0
    sel, one = _bias_constants()
    tri = jnp.tril(jnp.ones((tb, tb), F32)).astype(BF16)
    wide = FOX_PAIRS * LANES
    out_spec = pl.BlockSpec((None, FOX_PAIRS, tb, LANES), lambda b, i: (b, 0, i, 0))
    vmem = 2 * tb * LANES * 4 + 2 * tb * tb * 2 + 2 * LANES * 2 * wide * 2 \
        + 4 * FOX_PAIRS * tb * LANES * 2 + 4 * tb * 2 * wide * 4
    return pl.pallas_call(
        _bias_body,
        out_shape=[jax.ShapeDtypeStruct((bsz, FOX_PAIRS, kp, LANES), BF16)] * 2,
        grid=(bsz, kp // tb),
        in_specs=[
            pl.BlockSpec((None, tb, LANES), lambda b, i: (b, i, 0)),
            pl.BlockSpec((tb, tb), lambda b, i: (0, 0)),
            pl.BlockSpec((LANES, 2 * wide), lambda b, i: (0, 0)),
            pl.BlockSpec((1, 2 * wide), lambda b, i: (0, 0)),
        ],
        out_specs=[out_spec, out_spec],
        scratch_shapes=[pltpu.VMEM((1, LANES), F32)],
        compiler_params=_cparams(("parallel", "arbitrary"), vmem),
        name="fox_bias",
    )(lf, tri, sel, one)


def _q_aug(q, qb, lane):
    zero = jnp.zeros_like(q)
    out = []
    for e in range(2):
        head_lanes = (lane >> 6) == e
        bias_lanes = jnp.logical_and(lane >= e * BIAS_LANES_PER_HEAD,
                                     lane < (e + 1) * BIAS_LANES_PER_HEAD)
        out.append(jnp.concatenate(
            [jnp.where(head_lanes, q, zero), jnp.where(bias_lanes, qb, zero)], axis=1))
    return out


def _attn_body(q_ref, qb_ref, k_ref, kb_ref, v_ref, sg_ref, o_ref, m_ref, l_ref, acc_ref,
               sa_ref, sb_ref, *, tq, tk):
    i = pl.program_id(2)
    lane = lax.broadcasted_iota(jnp.int32, (tq, LANES), 1)
    q_aug = _q_aug(q_ref[...], qb_ref[...], lane)

    m_ref[...] = jnp.full(m_ref.shape, NEG_INF, F32)
    l_ref[...] = jnp.zeros(l_ref.shape, F32)
    acc_ref[...] = jnp.zeros(acc_ref.shape, F32)

    def qk_scores(j, qwin=slice(None)):
        ks = pl.multiple_of(j * tk, tk)
        k_aug = jnp.concatenate([k_ref[pl.ds(ks, tk), :], kb_ref[pl.ds(ks, tk), :]], axis=1)
        qs = q_aug if qwin == slice(None) else [qa[qwin] for qa in q_aug]
        return [lax.dot_general(k_aug, qs[e], NT_DIMS, preferred_element_type=F32)
                for e in range(2)]

    def softmax_pv(j, scores, qwin=slice(None)):
        v = v_ref[j]
        if qwin == slice(None):
            at = lambda ref, e: ref.at[e]
        else:
            at = lambda ref, e: ref.at[e, :, qwin]
        for e in range(2):
            s = scores[e]
            m_prev = at(m_ref, e)[...]
            m_new = jnp.maximum(m_prev, jnp.max(s, axis=0, keepdims=True))
            alpha = jnp.exp2(m_prev - m_new)
            p = jnp.exp2(s - m_new)
            at(l_ref, e)[...] = alpha * at(l_ref, e)[...] + jnp.sum(p, axis=0, keepdims=True)
            p16 = p.astype(BF16)
            at(acc_ref, e)[...] = alpha * at(acc_ref, e)[...] + _dot(v, p16)
            at(m_ref, e)[...] = m_new

    def stash(ref, scores):
        for e in range(2):
            ref[e] = scores[e]

    def fetch(ref):
        return [ref[0], ref[1]]

    stash(sa_ref, qk_scores(0))

    def unmasked_pair(jj, carry):
        j = 2 * jj
        stash(sb_ref, qk_scores(j + 1))
        softmax_pv(j, fetch(sa_ref))
        stash(sa_ref, qk_scores(j + 2))
        softmax_pv(j + 1, fetch(sb_ref))
        return carry

    lax.fori_loop(0, i, unmasked_pair, 0)
    early, late = slice(0, tk), slice(tk, tq)
    tmask = jnp.where(lax.broadcasted_iota(jnp.int32, (tk, tk), 0)
                      <= lax.broadcasted_iota(jnp.int32, (tk, tk), 1), 0.0, NEG_INF)
    late_scores = qk_scores(2 * i + 1, late)
    softmax_pv(2 * i, [jnp.concatenate([s[:, early] + tmask, s[:, late]], axis=1)
                       for s in fetch(sa_ref)])
    softmax_pv(2 * i + 1, [s + tmask for s in late_scores], late)

    o0 = acc_ref[0] * (1.0 / l_ref[0])
    o1 = acc_ref[1] * (1.0 / l_ref[1])
    feature = lax.broadcasted_iota(jnp.int32, (LANES, tq), 0)
    o = jnp.where((feature >> 6) == 0, o0, o1).T
    o_ref[...] = (o * sg_ref[...].astype(F32)).astype(BF16)


def _fox_attn(q16, qb, k16, kb, vt16, sg, *, tq, tk):
    bsz, t, d = q16.shape
    assert t % tq == 0 and tq == 2 * tk and k16.shape[2] == t
    qspec = pl.BlockSpec((None, tq, LANES), lambda b, p, i: (b, i, p))
    vmem = 2 * (3 * t * LANES * 2 + tq * LANES * (2 + 2 + 2 + 2)) + 3 * tq * LANES * 4 \
        + 2 * tk * 2 * LANES * 2 + 10 * tq * tk * 4
    return pl.pallas_call(
        functools.partial(_attn_body, tq=tq, tk=tk),
        out_shape=jax.ShapeDtypeStruct((bsz, t, d), BF16),
        grid=(bsz, FOX_PAIRS, t // tq),
        in_specs=[
            qspec,
            pl.BlockSpec((None, None, tq, LANES), lambda b, p, i: (b, p, i, 0)),
            pl.BlockSpec((None, None, t, LANES), lambda b, p, i: (p, b, 0, 0)),
            pl.BlockSpec((None, None, t, LANES), lambda b, p, i: (b, p, 0, 0)),
            pl.BlockSpec((t // tk, LANES, tk), lambda b, p, i: (b, p, 0)),
            qspec,
        ],
        out_specs=qspec,
        scratch_shapes=[pltpu.VMEM((2, 1, tq), F32), pltpu.VMEM((2, 1, tq), F32),
                        pltpu.VMEM((2, LANES, tq), F32),
                        pltpu.VMEM((2, tk, tq), F32), pltpu.VMEM((2, tk, tq), F32)],
        compiler_params=_cparams(("parallel", "parallel", "arbitrary"), vmem + (8 << 20)),
        name="fox_attn",
    )(q16, qb, k16, kb, vt16, sg)


def _cache_attn_body(q_ref, qb_ref, k_ref, kb_ref, v_ref, sg_ref, o_ref, m_ref, l_ref, acc_ref,
                     *, tk, off):
    t = q_ref.shape[0]
    lane = lax.broadcasted_iota(jnp.int32, (t, LANES), 1)
    q_aug = _q_aug(q_ref[...], qb_ref[...], lane)
    m_ref[...] = jnp.full(m_ref.shape, NEG_INF, F32)
    l_ref[...] = jnp.zeros(l_ref.shape, F32)
    acc_ref[...] = jnp.zeros(acc_ref.shape, F32)

    def step(j, masked):
        ks = pl.multiple_of(j * tk, tk)
        k_aug = jnp.concatenate([k_ref[pl.ds(ks, tk), :], kb_ref[pl.ds(ks, tk), :]], axis=1)
        v = v_ref[pl.ds(ks, tk), :]
        scores = [lax.dot_general(q_aug[e], k_aug, NT_DIMS, preferred_element_type=F32)
                  for e in range(2)]
        for e in range(2):
            s = scores[e]
            if masked:
                qpos = off + lax.broadcasted_iota(jnp.int32, s.shape, 0)
                kpos = ks + lax.broadcasted_iota(jnp.int32, s.shape, 1)
                s = jnp.where(kpos <= qpos, s, NEG_INF)
            m_prev = m_ref[e]
            m_new = jnp.maximum(m_prev, jnp.max(s, axis=1, keepdims=True))
            alpha = jnp.exp2(m_prev - m_new)
            p = jnp.exp2(s - m_new)
            l_ref[e] = alpha * l_ref[e] + jnp.sum(p, axis=1, keepdims=True)
            acc_ref[e] = alpha * acc_ref[e] + _dot(p.astype(BF16), v)
            m_ref[e] = m_new

    n_full = (off + 1) // tk
    n_all = (off + t + tk - 1) // tk
    for j in range(n_all):
        step(j, j >= n_full)

    o = jnp.where((lane >> 6) == 0, acc_ref[0] * (1.0 / l_ref[0]), acc_ref[1] * (1.0 / l_ref[1]))
    o_ref[...] = (o * sg_ref[...].astype(F32)).astype(BF16)


def _fox_attn_cached(q16, qb, k16, kb, v16, sg, *, tk, off):
    bsz, t, d = q16.shape
    kp = k16.shape[2]
    assert kp % tk == 0 and off % t == 0 and off + t <= kp
    qspec = pl.BlockSpec((None, t, LANES), lambda b, p: (b, 0, p))
    kspec = pl.BlockSpec((None, None, kp, LANES), lambda b, p: (p, b, 0, 0))
    vmem = 2 * (3 * kp * LANES * 2 + t * LANES * 8) + 6 * t * LANES * 4 + 8 * t * tk * 4 \
        + 4 * tk * LANES * 2
    return pl.pallas_call(
        functools.partial(_cache_attn_body, tk=tk, off=off),
        out_shape=jax.ShapeDtypeStruct((bsz, t, d), BF16),
        grid=(bsz, FOX_PAIRS),
        in_specs=[
            qspec,
            pl.BlockSpec((None, None, t, LANES), lambda b, p: (b, p, off // t, 0)),
            kspec,
            pl.BlockSpec((None, None, kp, LANES), lambda b, p: (b, p, 0, 0)),
            kspec,
            qspec,
        ],
        out_specs=qspec,
        scratch_shapes=[pltpu.VMEM((2, t, 1), F32), pltpu.VMEM((2, t, 1), F32),
                        pltpu.VMEM((2, t, LANES), F32)],
        compiler_params=_cparams(("parallel", "parallel"), vmem + (8 << 20)),
        name="fox_attn_cached",
    )(q16, qb, k16, kb, v16, sg)


TM = 1024
FFN_TF = 256
REPACK_ROWS = 128
GLA_TB = 256
ATTN_TQ = 1024
ATTN_TK = 512
CACHE_TK = 1408
BIAS_TB_CHOICES = (512, 384, 256, 128)


def _round_up(x, m):
    return (x + m - 1) // m * m


def _prep_weights(ffn_norm, w_ffn_gu, w_ffn_down, mix_norm, a_w_in, a_w_g2, a_b_g, a_g_out, a_w_o,
                  kv_norm, w_kvf, b_f, g_k, b_w_qg, b_g_q, b_w_o):
    d = w_ffn_gu.shape[2]
    rank = a_w_g2.shape[1]
    qk = a_w_g2.shape[2]
    vw = a_w_o.shape[1]
    w_in = a_w_in[0]
    pad_cols = lambda w: jnp.pad(w, ((0, 0), (0, LANES - w.shape[1])))
    groups = np.arange(d) // FOX_HEAD_DIM
    gsum = (groups[:, None] == np.arange(LANES)[None, :]).astype(np.float32)
    return dict(
        ffn_norm=ffn_norm, mix_norm=mix_norm, kv_norm=kv_norm,
        w_gu=_tile_cols(w_ffn_gu.reshape((-1,) + w_ffn_gu.shape[2:]), FFN_TF, rows=REPACK_ROWS),
        w_down=w_ffn_down.astype(BF16),
        a_wq=w_in[:, :qk].astype(BF16), a_wk=w_in[:, qk:2 * qk].astype(BF16),
        a_wv=w_in[:, 2 * qk:2 * qk + vw].astype(BF16),
        a_wr=w_in[:, 2 * qk + vw:2 * qk + 2 * vw].astype(BF16),
        a_wgl=pad_cols(w_in[:, 2 * qk + 2 * vw:]).astype(BF16),
        a_wg2=jnp.pad(a_w_g2[0], ((0, LANES - rank), (0, 0))).astype(BF16),
        a_bg=a_b_g[0], a_gout=a_g_out[0], a_wo=a_w_o[0].astype(BF16),
        kv_wk=w_kvf[:, :d].astype(BF16), kv_wv=w_kvf[:, d:2 * d].astype(BF16),
        kv_wv_t=w_kvf[:, d:2 * d].T.astype(BF16),
        kv_wf=pad_cols(w_kvf[:, 2 * d:]).astype(BF16),
        kv_bf=jnp.pad(b_f, (0, LANES - b_f.shape[0])).reshape(1, LANES),
        gk=jnp.tile(g_k, FOX_HEADS).reshape(1, d), gq=jnp.tile(b_g_q[0], FOX_HEADS).reshape(1, d),
        b_wq=b_w_qg[0][:, :d].astype(BF16), b_wg=b_w_qg[0][:, d:].astype(BF16),
        b_wo=b_w_o[0].astype(BF16),
        gsum=jnp.asarray(gsum, BF16), gexp=jnp.asarray(gsum.T, BF16),
    )


def _trunk(x, s0, past, w):
    bsz, t, d = x.shape
    n = bsz * t
    h = x.reshape(n, d)
    ffn = lambda h_, layer, half: _ffn(h_, w["ffn_norm"][layer, half], w["w_gu"][2 * layer + half],
                                       w["w_down"][layer, half], tm=TM)
    h = ffn(h, 0, 0)
    q, k, v, r, la = _gla_in(h, w["mix_norm"][0], w["a_wq"], w["a_wk"], w["a_wv"], w["a_wr"],
                             w["a_wgl"], w["a_wg2"], w["a_bg"], tm=TM)
    s3 = lambda z: z.reshape(bsz, t, z.shape[1])
    chunk = 64
    og, st_fin = _gla(s3(q), s3(k), s3(v), s3(r), s3(la), jnp.swapaxes(s0, 2, 3), w["a_gout"],
                      chunk=chunk, tb=GLA_TB)
    h = _proj_res(h, og.reshape(n, -1), w["a_wo"], tm=TM)
    h = ffn(h, 0, 1)
    tq = min(ATTN_TQ, t)
    tk = min(ATTN_TK, tq // 2)
    k_new, v_new, lf, k16, v16 = _shared_kv(
        h, w["kv_norm"], w["kv_wk"], w["kv_wv_t"] if past is None else w["kv_wv"], w["kv_wf"],
        w["kv_bf"], w["gk"], w["gsum"], w["gexp"], tm=TM, vt_block=tk if past is None else None)
    h = ffn(h, 1, 0)
    q16, sg = _fox_q(h, w["mix_norm"][1], w["b_wq"], w["b_wg"], w["gq"], w["gsum"], w["gexp"], tm=TM)
    pairs4 = lambda z: z.reshape(FOX_PAIRS, bsz, t, LANES)
    if past is None:
        lf_all = s3(lf)
    else:
        past_k, past_v, past_lf = past
        p_len = past_k.shape[1]
        total = p_len + t
        ctk = CACHE_TK if total >= CACHE_TK else _round_up(total, LANES)
        pad = _round_up(total, ctk) - total
        past_pairs = lambda a: jnp.transpose(
            a.astype(BF16).reshape(bsz, p_len, FOX_PAIRS, LANES), (2, 0, 1, 3))
        cat_keys = lambda a, b_: jnp.concatenate(
            [a, b_, jnp.zeros((FOX_PAIRS, bsz, pad, LANES), a.dtype)], axis=2)
        k_all = cat_keys(past_pairs(past_k), pairs4(k16))
        v_all = cat_keys(past_pairs(past_v), pairs4(v16))
        lf_all = jnp.concatenate(
            [jnp.pad(past_lf, ((0, 0), (0, 0), (0, LANES - past_lf.shape[2]))), s3(lf),
             jnp.zeros((bsz, pad, LANES), F32)], axis=1)
    bias_tb = next(c for c in BIAS_TB_CHOICES if lf_all.shape[1] % c == 0)
    qb, kb = _fox_bias(lf_all, tb=bias_tb)
    if past is None:
        og = _fox_attn(s3(q16), qb, pairs4(k16), kb, v16, s3(sg), tq=tq, tk=tk)
    else:
        og = _fox_attn_cached(s3(q16), qb, k_all, kb, v_all, s3(sg), tk=ctk, off=p_len)
    h = _proj_res(h, og.reshape(n, d), w["b_wo"], tm=TM)
    h = ffn(h, 1, 1)
    heads4 = lambda z: z.reshape(bsz, t, FOX_HEADS, FOX_HEAD_DIM)
    return (h.reshape(bsz, t, d), jnp.swapaxes(st_fin, 2, 3)[:, None], heads4(k_new), heads4(v_new),
            lf[:, :FOX_HEADS].reshape(bsz, t, FOX_HEADS))


def kernel(x_prompt, x_sample, state_gla, cache_k, cache_v, cache_logf, ffn_norm, w_ffn_gu, w_ffn_down, mix_norm, a_w_in, a_w_g2, a_b_g, a_g_out, a_w_o, kv_norm, w_kvf, b_f, g_k, b_w_qg, b_g_q, b_w_o):
    w = _prep_weights(ffn_norm, w_ffn_gu, w_ffn_down, mix_norm, a_w_in, a_w_g2, a_b_g, a_g_out,
                      a_w_o, kv_norm, w_kvf, b_f, g_k, b_w_qg, b_g_q, b_w_o)
    s0_prompt = jnp.zeros((x_prompt.shape[0],) + state_gla.shape[2:], F32)
    y_p, gla_p, k_p, v_p, lf_p = _trunk(x_prompt, s0_prompt, None, w)
    y_s, gla_s, k_s, v_s, lf_s = _trunk(x_sample, state_gla[:, 0], (cache_k, cache_v, cache_logf), w)
    return (y_p, y_s, gla_p, gla_s, k_p, v_p, lf_p, k_s, v_s, lf_s)
```

```python
import functools

import jax
import jax.numpy as jnp
import numpy as np
from jax import lax
from jax.experimental import pallas as pl
from jax.experimental.pallas import tpu as pltpu

F32 = jnp.float32
BF16 = jnp.bfloat16

EPS = 1e-6
NEG_INF = -1e30
LOG2E = 1.4426950408889634

LANES = 128
V7X_VMEM_BYTES = 64 * 1024 * 1024
V7X_SCOPED_VMEM_CAP = 60000 * 1024

GLA_HEADS = 4
GLA_GATE_TAU = 16.0
FOX_HEADS = 16
FOX_HEAD_DIM = 64
FOX_PAIRS = FOX_HEADS // 2
BIAS_PARTS = 3
BIAS_LANES_PER_HEAD = 2 * BIAS_PARTS

NT_DIMS = (((1,), (1,)), ((), ()))
TN_DIMS = (((0,), (0,)), ((), ()))


def _cparams(semantics, vmem_bytes):
    limit = int(min(max(vmem_bytes, 16 * 1024 * 1024), V7X_SCOPED_VMEM_CAP))
    return pltpu.CompilerParams(dimension_semantics=semantics, vmem_limit_bytes=limit)


def _dot(a, b):
    return jnp.dot(a, b, preferred_element_type=F32)


def _rms(x, g):
    ms = jnp.mean(x * x, axis=-1, keepdims=True)
    return x * lax.rsqrt(ms + EPS) * g


def _split2(x):
    hi = x.astype(BF16)
    lo = (x - hi.astype(F32)).astype(BF16)
    return hi, lo


def _split3(x):
    p1 = x.astype(BF16)
    r1 = x - p1.astype(F32)
    p2 = r1.astype(BF16)
    p3 = (r1 - p2.astype(F32)).astype(BF16)
    return p1, p2, p3


def _log_sigmoid(x):
    return jnp.minimum(x, 0.0) - jnp.log(1.0 + jnp.exp(-jnp.abs(x)))


def _head_norm(x, gsum_ref, gexp_ref, gain, head_dim):
    ssq = _dot((x * x).astype(BF16), gsum_ref[...])
    inv = lax.rsqrt(ssq * (1.0 / head_dim) + EPS)
    ihi, ilo = _split2(inv)
    inv_full = _dot(ihi, gexp_ref[...]) + _dot(ilo, gexp_ref[...])
    return x * inv_full * gain


def _ffn_body(x_ref, g_ref, wg_ref, wu_ref, wd_ref, o_ref, xn_ref):
    j = pl.program_id(1)

    @pl.when(j == 0)
    def _():
        x = x_ref[...]
        xn_ref[...] = _rms(x, g_ref[...]).astype(BF16)
        o_ref[...] = x

    xn = xn_ref[...]
    g = _dot(xn, wg_ref[...])
    u = _dot(xn, wu_ref[...])
    h = (0.5 * (g * jax.nn.sigmoid(g) * u)).astype(BF16)
    o_ref[...] += _dot(h, wd_ref[...])


def _tile_cols_body(w_ref, o_ref):
    tf = o_ref.shape[2]
    for c in range(o_ref.shape[0]):
        o_ref[c] = w_ref[:, c * tf:(c + 1) * tf].astype(o_ref.dtype)


def _tile_cols(w, tf, *, rows):
    g, d, n = w.shape
    assert n % tf == 0 and d % rows == 0
    return pl.pallas_call(
        _tile_cols_body,
        out_shape=jax.ShapeDtypeStruct((g, n // tf, d, tf), BF16),
        grid=(g, d // rows),
        in_specs=[pl.BlockSpec((None, rows, n), lambda a, r: (a, r, 0))],
        out_specs=pl.BlockSpec((None, n // tf, rows, tf), lambda a, r: (a, 0, r, 0)),
        compiler_params=_cparams(("parallel", "parallel"), 2 * rows * n * (4 + 2) + (4 << 20)),
        name="tile_cols",
    )(w)


def _ffn(x, g, w_gu, w_down, *, tm):
    n, d = x.shape
    d_ff = w_down.shape[0]
    tf = w_gu.shape[2]
    tm = min(tm, n)
    nf = d_ff // tf
    assert n % tm == 0 and w_gu.shape[0] == 2 * nf
    vmem = 4 * tm * d * 4 + tm * d * 2 + 2 * 3 * d * tf * 2 + 3 * tm * tf * 4 + tm * d * 4
    return pl.pallas_call(
        _ffn_body,
        out_shape=jax.ShapeDtypeStruct((n, d), F32),
        grid=(n // tm, nf),
        in_specs=[
            pl.BlockSpec((tm, d), lambda i, j: (i, 0)),
            pl.BlockSpec((1, d), lambda i, j: (0, 0)),
            pl.BlockSpec((None, d, tf), lambda i, j: (j, 0, 0)),
            pl.BlockSpec((None, d, tf), lambda i, j: (j + nf, 0, 0)),
            pl.BlockSpec((tf, d), lambda i, j: (j, 0)),
        ],
        out_specs=pl.BlockSpec((tm, d), lambda i, j: (i, 0)),
        scratch_shapes=[pltpu.VMEM((tm, d), BF16)],
        compiler_params=_cparams(("parallel", "arbitrary"), vmem + (8 << 20)),
        name="ffn",
    )(x, g.reshape(1, d), w_gu, w_gu, w_down)


def _proj_res_body(h_ref, a_ref, w_ref, o_ref):
    o_ref[...] = h_ref[...] + _dot(a_ref[...], w_ref[...])


def _proj_res(h, a, w, *, tm):
    n, d = h.shape
    k = a.shape[1]
    tm = min(tm, n)
    assert n % tm == 0
    vmem = 2 * (2 * tm * d * 4 + tm * k * 2 + k * d * 2) + tm * d * 4
    return pl.pallas_call(
        _proj_res_body,
        out_shape=jax.ShapeDtypeStruct((n, d), F32),
        grid=(n // tm,),
        in_specs=[
            pl.BlockSpec((tm, d), lambda i: (i, 0)),
            pl.BlockSpec((tm, k), lambda i: (i, 0)),
            pl.BlockSpec((k, d), lambda i: (0, 0)),
        ],
        out_specs=pl.BlockSpec((tm, d), lambda i: (i, 0)),
        compiler_params=_cparams(("parallel",), vmem + (4 << 20)),
        name="proj_res",
    )(h, a, w)


def _gla_in_body(x_ref, g_ref, wq_ref, wk_ref, wv_ref, wr_ref, wgl_ref, wg2_ref, bg_ref,
                 q_ref, k_ref, v_ref, r_ref, la_ref):
    xn = _rms(x_ref[...], g_ref[...]).astype(BF16)
    q_ref[...] = _dot(xn, wq_ref[...]).astype(q_ref.dtype)
    k_ref[...] = _dot(xn, wk_ref[...]).astype(k_ref.dtype)
    v_ref[...] = _dot(xn, wv_ref[...]).astype(v_ref.dtype)
    r_ref[...] = _dot(xn, wr_ref[...]).astype(r_ref.dtype)
    gl = _dot(xn, wgl_ref[...]).astype(BF16)
    z = _dot(gl, wg2_ref[...]) + bg_ref[...]
    la_ref[...] = _log_sigmoid(z) * (1.0 / GLA_GATE_TAU)


def _gla_in(x, g, wq, wk, wv, wr, wgl, wg2, bg, *, tm):
    n, d = x.shape
    qk, vw = wq.shape[1], wv.shape[1]
    tm = min(tm, n)
    assert n % tm == 0
    row = lambda i: (i, 0)
    fix = lambda i: (0, 0)
    w_bytes = 2 * (2 * d * qk + 2 * d * vw + d * LANES + LANES * qk)
    vmem = 2 * tm * (d + 3 * qk + 2 * vw) * 4 + 2 * w_bytes + tm * d * 2
    return pl.pallas_call(
        _gla_in_body,
        out_shape=[jax.ShapeDtypeStruct((n, qk), BF16), jax.ShapeDtypeStruct((n, qk), BF16),
                   jax.ShapeDtypeStruct((n, vw), BF16), jax.ShapeDtypeStruct((n, vw), BF16),
                   jax.ShapeDtypeStruct((n, qk), F32)],
        grid=(n // tm,),
        in_specs=[
            pl.BlockSpec((tm, d), row), pl.BlockSpec((1, d), fix),
            pl.BlockSpec((d, qk), fix), pl.BlockSpec((d, qk), fix),
            pl.BlockSpec((d, vw), fix), pl.BlockSpec((d, vw), fix),
            pl.BlockSpec((d, LANES), fix), pl.BlockSpec((LANES, qk), fix), pl.BlockSpec((1, qk), fix),
        ],
        out_specs=[pl.BlockSpec((tm, qk), row), pl.BlockSpec((tm, qk), row),
                   pl.BlockSpec((tm, vw), row), pl.BlockSpec((tm, vw), row),
                   pl.BlockSpec((tm, qk), row)],
        compiler_params=_cparams(("parallel",), vmem + (8 << 20)),
        name="gla_in",
    )(x, g.reshape(1, d), wq, wk, wv, wr, wgl, wg2, bg.reshape(1, qk))


def _gla_body(q_ref, k_ref, v_ref, r_ref, la_ref, s0_ref, gout_ref, tri_ref,
              og_ref, sfin_ref, st_ref, *, chunk, n_chunks, heads, dk, dv):
    t = pl.program_id(1)

    @pl.when(t == 0)
    def _():
        st_ref[...] = s0_ref[...]

    scale = dk ** -0.5
    row = lax.broadcasted_iota(jnp.int32, (chunk, chunk), 0)
    col = lax.broadcasted_iota(jnp.int32, (chunk, chunk), 1)
    causal = col <= row
    tri = tri_ref[...]
    gout = gout_ref[...]

    for c in range(n_chunks):
        sl = slice(c * chunk, (c + 1) * chunk)
        la_hi, la_lo = _split2(la_ref[sl, :])
        b = _dot(tri, la_hi) + _dot(tri, la_lo)
        b_last = b[chunk - 1:chunk, :]
        q = q_ref[sl, :].astype(F32)
        k = k_ref[sl, :].astype(F32)
        qe = (q * scale * jnp.exp(b)).astype(BF16)
        ke = (k * jnp.exp(-b)).astype(BF16)
        kd = (k * jnp.exp(b_last - b)).astype(BF16)
        dec = jnp.exp(b_last)
        for h in range(heads):
            ks = slice(h * dk, (h + 1) * dk)
            vs = slice(h * dv, (h + 1) * dv)
            vh = v_ref[sl, vs]
            att = lax.dot_general(qe[:, ks], ke[:, ks], NT_DIMS, preferred_element_type=F32)
            att = jnp.where(causal, att, 0.0).astype(BF16)
            st = st_ref[h]
            o = _dot(att, vh) + lax.dot_general(qe[:, ks], st.astype(BF16), NT_DIMS,
                                                preferred_element_type=F32)
            st_ref[h] = st * dec[:, ks] + lax.dot_general(vh, kd[:, ks], TN_DIMS,
                                                          preferred_element_type=F32)
            on = _rms(o, gout)
            rh = r_ref[sl, vs].astype(F32)
            og_ref[sl, vs] = (on * (rh * jax.nn.sigmoid(rh))).astype(BF16)

    @pl.when(t == pl.num_programs(1) - 1)
    def _():
        sfin_ref[...] = st_ref[...]


def _gla(q, k, v, r, la, s0t, gout, *, chunk, tb):
    bsz, t, qk = q.shape
    vw = v.shape[2]
    heads = s0t.shape[1]
    dk, dv = qk // heads, vw // heads
    tb = min(tb, t)
    assert t % tb == 0 and tb % chunk == 0
    tri = jnp.tril(jnp.ones((chunk, chunk), F32)).astype(BF16)
    blk = lambda w: pl.BlockSpec((None, tb, w), lambda b, i: (b, i, 0))
    st_spec = pl.BlockSpec((None, heads, dv, dk), lambda b, i: (b, 0, 0, 0))
    vmem = 2 * tb * (3 * qk + 2 * vw) * 4 + 2 * tb * vw * 2 + 5 * heads * dv * dk * 4
    return pl.pallas_call(
        functools.partial(_gla_body, chunk=chunk, n_chunks=tb // chunk, heads=heads, dk=dk, dv=dv),
        out_shape=[jax.ShapeDtypeStruct((bsz, t, vw), BF16),
                   jax.ShapeDtypeStruct((bsz, heads, dv, dk), F32)],
        grid=(bsz, t // tb),
        in_specs=[blk(qk), blk(qk), blk(vw), blk(vw), blk(qk), st_spec,
                  pl.BlockSpec((1, dv), lambda b, i: (0, 0)),
                  pl.BlockSpec((chunk, chunk), lambda b, i: (0, 0))],
        out_specs=[blk(vw), st_spec],
        scratch_shapes=[pltpu.VMEM((heads, dv, dk), F32)],
        compiler_params=_cparams(("parallel", "arbitrary"), vmem + (8 << 20)),
        name="gla_chunks",
    )(q, k, v, r, la, s0t, gout.reshape(1, dv), tri)


def _store_pairs(ref, x):
    for p in range(ref.shape[0]):
        ref[p] = x[:, p * LANES:(p + 1) * LANES]


def _kv_body(x_ref, g_ref, wk_ref, wv_ref, wf_ref, bf_ref, gk_ref, gsum_ref, gexp_ref,
             k_ref, v_ref, lf_ref, k16_ref, v16_ref, *, vt_block):
    xn = _rms(x_ref[...], g_ref[...]).astype(BF16)
    k = _head_norm(_dot(xn, wk_ref[...]), gsum_ref, gexp_ref, gk_ref[...], FOX_HEAD_DIM)
    k_ref[...] = k
    _store_pairs(k16_ref, k.astype(BF16))
    lf_ref[...] = _log_sigmoid(_dot(xn, wf_ref[...]) + bf_ref[...])
    if vt_block is None:
        v = _dot(xn, wv_ref[...])
        v_ref[...] = v
        _store_pairs(v16_ref, v.astype(BF16))
    else:
        vt = lax.dot_general(wv_ref[...], xn, NT_DIMS, preferred_element_type=F32)
        v_ref[...] = vt.T
        vt16 = vt.astype(BF16)
        for c in range(v16_ref.shape[0]):
            v16_ref[c] = vt16[:, c * vt_block:(c + 1) * vt_block]


def _shared_kv(x, g, wk, wv, wf, bf, gk, gsum, gexp, *, tm, vt_block=None):
    n, d = x.shape
    tm = min(tm, n)
    assert n % tm == 0
    row = lambda i: (i, 0)
    fix = lambda i: (0, 0)
    pair_shape = (d // LANES, n, LANES)
    pair_spec = pl.BlockSpec((d // LANES, tm, LANES), lambda i: (0, i, 0))
    if vt_block is None:
        v16_shape, v16_spec = pair_shape, pair_spec
    else:
        assert tm % vt_block == 0
        v16_shape = (n // vt_block, d, vt_block)
        v16_spec = pl.BlockSpec((tm // vt_block, d, vt_block), lambda i: (i, 0, 0))
    vmem = 2 * tm * d * (4 + 4 + 4 + 2 + 2) + 2 * tm * LANES * 4 + 2 * 2 * (2 * d * d + 3 * d * LANES) \
        + 6 * tm * d * 4
    return pl.pallas_call(
        functools.partial(_kv_body, vt_block=vt_block),
        out_shape=[jax.ShapeDtypeStruct((n, d), F32), jax.ShapeDtypeStruct((n, d), F32),
                   jax.ShapeDtypeStruct((n, LANES), F32),
                   jax.ShapeDtypeStruct(pair_shape, BF16), jax.ShapeDtypeStruct(v16_shape, BF16)],
        grid=(n // tm,),
        in_specs=[
            pl.BlockSpec((tm, d), row), pl.BlockSpec((1, d), fix),
            pl.BlockSpec((d, d), fix), pl.BlockSpec((d, d), fix), pl.BlockSpec((d, LANES), fix),
            pl.BlockSpec((1, LANES), fix), pl.BlockSpec((1, d), fix),
            pl.BlockSpec((d, LANES), fix), pl.BlockSpec((LANES, d), fix),
        ],
        out_specs=[pl.BlockSpec((tm, d), row), pl.BlockSpec((tm, d), row),
                   pl.BlockSpec((tm, LANES), row),
                   pair_spec, v16_spec],
        compiler_params=_cparams(("parallel",), vmem),
        name="shared_kv",
    )(x, g.reshape(1, d), wk, wv, wf, bf, gk, gsum, gexp)


def _fox_q_body(x_ref, g_ref, wq_ref, wg_ref, gq_ref, gsum_ref, gexp_ref, q16_ref, sg_ref):
    xn = _rms(x_ref[...], g_ref[...]).astype(BF16)
    q = _head_norm(_dot(xn, wq_ref[...]), gsum_ref, gexp_ref, gq_ref[...], FOX_HEAD_DIM)
    q16_ref[...] = (q * (FOX_HEAD_DIM ** -0.5 * LOG2E)).astype(BF16)
    sg_ref[...] = jax.nn.sigmoid(_dot(xn, wg_ref[...])).astype(sg_ref.dtype)


def _fox_q(x, g, wq, wg, gq, gsum, gexp, *, tm):
    n, d = x.shape
    tm = min(tm, n)
    assert n % tm == 0
    row = lambda i: (i, 0)
    fix = lambda i: (0, 0)
    vmem = 2 * tm * d * (4 + 2 + 4) + 2 * 2 * (2 * d * d + 2 * d * LANES) + 6 * tm * d * 4
    return pl.pallas_call(
        _fox_q_body,
        out_shape=[jax.ShapeDtypeStruct((n, d), BF16), jax.ShapeDtypeStruct((n, d), BF16)],
        grid=(n // tm,),
        in_specs=[
            pl.BlockSpec((tm, d), row), pl.BlockSpec((1, d), fix),
            pl.BlockSpec((d, d), fix), pl.BlockSpec((d, d), fix), pl.BlockSpec((1, d), fix),
            pl.BlockSpec((d, LANES), fix), pl.BlockSpec((LANES, d), fix),
        ],
        out_specs=[pl.BlockSpec((tm, d), row), pl.BlockSpec((tm, d), row)],
        compiler_params=_cparams(("parallel",), vmem),
        name="fox_q",
    )(x, g.reshape(1, d), wq, wg, gq, gsum, gexp)


def _pack_parts(x):
    p1, p2, p3 = (p.astype(F32) for p in _split3(x))
    packed = p1 + pltpu.roll(p2, FOX_HEADS, 1) + pltpu.roll(p3, 2 * FOX_HEADS, 1)
    return packed.astype(BF16)


def _bias_body(lf_ref, tri_ref, sel_ref, one_ref, qb_ref, kb_ref, carry_ref):
    @pl.when(pl.program_id(1) == 0)
    def _():
        carry_ref[...] = jnp.zeros_like(carry_ref)

    lf = lf_ref[...]
    tb = lf.shape[0]
    head_lanes = lax.broadcasted_iota(jnp.int32, lf.shape, 1) < FOX_HEADS
    cp = _dot(tri_ref[...], _pack_parts(jnp.where(head_lanes, lf, 0.0)))
    c = cp + pltpu.roll(cp, LANES - FOX_HEADS, 1) + pltpu.roll(cp, LANES - 2 * FOX_HEADS, 1)
    c = jnp.where(head_lanes, c, 0.0) + carry_ref[...]
    carry_ref[...] = c[tb - 1:tb, :]
    qkb = _dot(_pack_parts(c * LOG2E), sel_ref[...]) + one_ref[...]
    wide = FOX_PAIRS * LANES
    for p in range(FOX_PAIRS):
        qb_ref[p] = qkb[:, p * LANES:(p + 1) * LANES].astype(BF16)
        kb_ref[p] = qkb[:, wide + p * LANES:wide + (p + 1) * LANES].astype(BF16)


def _bias_constants():
    wide = FOX_PAIRS * LANES
    sel = np.zeros((LANES, 2 * wide), np.float32)
    one = np.zeros((1, 2 * wide), np.float32)
    for p in range(FOX_PAIRS):
        for e in range(2):
            base = p * LANES + e * BIAS_LANES_PER_HEAD
            for part in range(BIAS_PARTS):
                row = part * FOX_HEADS + 2 * p + e
                sel[row, base + part] = 1.0
                one[0, base + BIAS_PARTS + part] = 1.0
                one[0, wide + base + part] = 1.0
                sel[row, wide + base + BIAS_PARTS + part] = -1.0
    return jnp.asarray(sel, BF16), jnp.asarray(one)


def _fox_bias(lf, *, tb):
    bsz, kp, _ = lf.shape
    assert kp % tb == 0
    sel, one = _bias_constants()
    tri = jnp.tril(jnp.ones((tb, tb), F32)).astype(BF16)
    wide = FOX_PAIRS * LANES
    out_spec = pl.BlockSpec((None, FOX_PAIRS, tb, LANES), lambda b, i: (b, 0, i, 0))
    vmem = 2 * tb * LANES * 4 + 2 * tb * tb * 2 + 2 * LANES * 2 * wide * 2 \
        + 4 * FOX_PAIRS * tb * LANES * 2 + 4 * tb * 2 * wide * 4
    return pl.pallas_call(
        _bias_body,
        out_shape=[jax.ShapeDtypeStruct((bsz, FOX_PAIRS, kp, LANES), BF16)] * 2,
        grid=(bsz, kp // tb),
        in_specs=[
            pl.BlockSpec((None, tb, LANES), lambda b, i: (b, i, 0)),
            pl.BlockSpec((tb, tb), lambda b, i: (0, 0)),
            pl.BlockSpec((LANES, 2 * wide), lambda b, i: (0, 0)),
            pl.BlockSpec((1, 2 * wide), lambda b, i: (0, 0)),
        ],
        out_specs=[out_spec, out_spec],
        scratch_shapes=[pltpu.VMEM((1, LANES), F32)],
        compiler_params=_cparams(("parallel", "arbitrary"), vmem),
        name="fox_bias",
    )(lf, tri, sel, one)


def _q_aug(q, qb, lane):
    zero = jnp.zeros_like(q)
    out = []
    for e in range(2):
        head_lanes = (lane >> 6) == e
        bias_lanes = jnp.logical_and(lane >= e * BIAS_LANES_PER_HEAD,
                                     lane < (e + 1) * BIAS_LANES_PER_HEAD)
        out.append(jnp.concatenate(
            [jnp.where(head_lanes, q, zero), jnp.where(bias_lanes, qb, zero)], axis=1))
    return out


def _attn_body(q_ref, qb_ref, k_ref, kb_ref, v_ref, sg_ref, o_ref, m_ref, l_ref, acc_ref,
               s_ref, *, tq, tk):
    i = pl.program_id(2)
    lane = lax.broadcasted_iota(jnp.int32, (tq, LANES), 1)
    q_aug = _q_aug(q_ref[...], qb_ref[...], lane)

    m_ref[...] = jnp.full(m_ref.shape, NEG_INF, F32)
    l_ref[...] = jnp.zeros(l_ref.shape, F32)
    acc_ref[...] = jnp.zeros(acc_ref.shape, F32)

    def qk_scores(j, qwin=slice(None)):
        ks = pl.multiple_of(j * tk, tk)
        k_aug = jnp.concatenate([k_ref[pl.ds(ks, tk), :], kb_ref[pl.ds(ks, tk), :]], axis=1)
        qs = q_aug if qwin == slice(None) else [qa[qwin] for qa in q_aug]
        return [lax.dot_general(k_aug, qs[e], NT_DIMS, preferred_element_type=F32)
                for e in range(2)]

    def softmax_pv(j, scores, qwin=slice(None)):
        v = v_ref[j]
        if qwin == slice(None):
            at = lambda ref, e: ref.at[e]
        else:
            at = lambda ref, e: ref.at[e, :, qwin]
        for e in range(2):
            s = scores[e]
            m_prev = at(m_ref, e)[...]
            m_new = jnp.maximum(m_prev, jnp.max(s, axis=0, keepdims=True))
            alpha = jnp.exp2(m_prev - m_new)
            p = jnp.exp2(s - m_new)
            at(l_ref, e)[...] = alpha * at(l_ref, e)[...] + jnp.sum(p, axis=0, keepdims=True)
            p16 = p.astype(BF16)
            at(acc_ref, e)[...] = alpha * at(acc_ref, e)[...] + _dot(v, p16)
            at(m_ref, e)[...] = m_new

    def stash(scores):
        for e in range(2):
            s_ref[e] = scores[e]

    def fetch():
        return [s_ref[0], s_ref[1]]

    stash(qk_scores(0))

    def unmasked_pair(jj, carry):
        j = 2 * jj
        odd = qk_scores(j + 1)
        softmax_pv(j, fetch())
        stash(qk_scores(j + 2))
        softmax_pv(j + 1, odd)
        return carry

    lax.fori_loop(0, i, unmasked_pair, 0)
    early, late = slice(0, tk), slice(tk, tq)
    tmask = jnp.where(lax.broadcasted_iota(jnp.int32, (tk, tk), 0)
                      <= lax.broadcasted_iota(jnp.int32, (tk, tk), 1), 0.0, NEG_INF)
    late_scores = qk_scores(2 * i + 1, late)
    softmax_pv(2 * i, [jnp.concatenate([s[:, early] + tmask, s[:, late]], axis=1)
                       for s in fetch()])
    softmax_pv(2 * i + 1, [s + tmask for s in late_scores], late)

    o0 = acc_ref[0] * (1.0 / l_ref[0])
    o1 = acc_ref[1] * (1.0 / l_ref[1])
    feature = lax.broadcasted_iota(jnp.int32, (LANES, tq), 0)
    o = jnp.where((feature >> 6) == 0, o0, o1).T
    o_ref[...] = (o * sg_ref[...].astype(F32)).astype(BF16)


def _fox_attn(q16, qb, k16, kb, vt16, sg, *, tq, tk):
    bsz, t, d = q16.shape
    assert t % tq == 0 and tq == 2 * tk and k16.shape[2] == t
    qspec = pl.BlockSpec((None, tq, LANES), lambda b, p, i: (b, i, p))
    vmem = 2 * (3 * t * LANES * 2 + tq * LANES * (2 + 2 + 2 + 2)) + 3 * tq * LANES * 4 \
        + 2 * tk * 2 * LANES * 2 + 10 * tq * tk * 4
    return pl.pallas_call(
        functools.partial(_attn_body, tq=tq, tk=tk),
        out_shape=jax.ShapeDtypeStruct((bsz, t, d), BF16),
        grid=(bsz, FOX_PAIRS, t // tq),
        in_specs=[
            qspec,
            pl.BlockSpec((None, None, tq, LANES), lambda b, p, i: (b, p, i, 0)),
            pl.BlockSpec((None, None, t, LANES), lambda b, p, i: (p, b, 0, 0)),
            pl.BlockSpec((None, None, t, LANES), lambda b, p, i: (b, p, 0, 0)),
            pl.BlockSpec((t // tk, LANES, tk), lambda b, p, i: (b, p, 0)),
            qspec,
        ],
        out_specs=qspec,
        scratch_shapes=[pltpu.VMEM((2, 1, tq), F32), pltpu.VMEM((2, 1, tq), F32),
                        pltpu.VMEM((2, LANES, tq), F32),
                        pltpu.VMEM((2, tk, tq), F32)],
        compiler_params=_cparams(("parallel", "parallel", "arbitrary"), vmem + (8 << 20)),
        name="fox_attn",
    )(q16, qb, k16, kb, vt16, sg)


def _cache_attn_body(q_ref, qb_ref, ck_ref, cv_ref, kn_ref, vn_ref, kb_ref, sg_ref,
                     o_ref, m_ref, l_ref, acc_ref, *, tk):
    t = q_ref.shape[0]
    p_len = ck_ref.shape[0]
    lane = lax.broadcasted_iota(jnp.int32, (t, LANES), 1)
    q_aug = _q_aug(q_ref[...], qb_ref[...], lane)
    m_ref[...] = jnp.full(m_ref.shape, NEG_INF, F32)
    l_ref[...] = jnp.zeros(l_ref.shape, F32)
    acc_ref[...] = jnp.zeros(acc_ref.shape, F32)

    def fold(k, kb, v, mask):
        k_aug = jnp.concatenate([k, kb], axis=1)
        scores = [lax.dot_general(q_aug[e], k_aug, NT_DIMS, preferred_element_type=F32)
                  for e in range(2)]
        for e in range(2):
            s = scores[e] if mask is None else scores[e] + mask
            m_prev = m_ref[e]
            m_new = jnp.maximum(m_prev, jnp.max(s, axis=1, keepdims=True))
            alpha = jnp.exp2(m_prev - m_new)
            p = jnp.exp2(s - m_new)
            l_ref[e] = alpha * l_ref[e] + jnp.sum(p, axis=1, keepdims=True)
            acc_ref[e] = alpha * acc_ref[e] + _dot(p.astype(BF16), v)
            m_ref[e] = m_new

    for j in range(p_len // tk):
        rows = slice(j * tk, (j + 1) * tk)
        fold(ck_ref[rows, :], kb_ref[rows, :], cv_ref[rows, :], None)
    causal = jnp.where(lax.broadcasted_iota(jnp.int32, (t, t), 1)
                       <= lax.broadcasted_iota(jnp.int32, (t, t), 0), 0.0, NEG_INF)
    fold(kn_ref[...], kb_ref[p_len:p_len + t, :], vn_ref[...], causal)

    o = jnp.where((lane >> 6) == 0, acc_ref[0] * (1.0 / l_ref[0]), acc_ref[1] * (1.0 / l_ref[1]))
    o_ref[...] = (o * sg_ref[...].astype(F32)).astype(BF16)


def _fox_attn_cached(q16, qb, kb, ck16, cv16, k16, v16, sg, *, tk):
    bsz, t, d = q16.shape
    p_len = ck16.shape[2]
    assert p_len % tk == 0 and p_len % t == 0 and kb.shape[2] == p_len + t
    qspec = pl.BlockSpec((None, t, LANES), lambda b, p: (b, 0, p))
    cspec = pl.BlockSpec((None, None, p_len, LANES), lambda b, p: (p, b, 0, 0))
    nspec = pl.BlockSpec((None, t, LANES), lambda b, p: (p, b, 0))
    vmem = 2 * (3 * (p_len + t) * LANES * 2 + t * LANES * 8) + 6 * t * LANES * 4 + 8 * t * tk * 4 \
        + 4 * tk * LANES * 2
    return pl.pallas_call(
        functools.partial(_cache_attn_body, tk=tk),
        out_shape=jax.ShapeDtypeStruct((bsz, t, d), BF16),
        grid=(bsz, FOX_PAIRS),
        in_specs=[
            qspec,
            pl.BlockSpec((None, None, t, LANES), lambda b, p: (b, p, p_len // t, 0)),
            cspec, cspec, nspec, nspec,
            pl.BlockSpec((None, None, p_len + t, LANES), lambda b, p: (b, p, 0, 0)),
            qspec,
        ],
        out_specs=qspec,
        scratch_shapes=[pltpu.VMEM((2, t, 1), F32), pltpu.VMEM((2, t, 1), F32),
                        pltpu.VMEM((2, t, LANES), F32)],
        compiler_params=_cparams(("parallel", "parallel"), vmem + (8 << 20)),
        name="fox_attn_cached",
    )(q16, qb, ck16, cv16, k16, v16, kb, sg)


TM = 1024
FFN_TF = 256
REPACK_ROWS = 128
GLA_TB = 256
ATTN_TQ = 1024
ATTN_TK = 512
CACHE_TK = 4096
BIAS_TB_CHOICES = (832, 512, 384, 320, 256, 192, 128, 64)


def _round_up(x, m):
    return (x + m - 1) // m * m


def _prep_weights(ffn_norm, w_ffn_gu, w_ffn_down, mix_norm, a_w_in, a_w_g2, a_b_g, a_g_out, a_w_o,
                  kv_norm, w_kvf, b_f, g_k, b_w_qg, b_g_q, b_w_o):
    d = w_ffn_gu.shape[2]
    rank = a_w_g2.shape[1]
    qk = a_w_g2.shape[2]
    vw = a_w_o.shape[1]
    w_in = a_w_in[0]
    pad_cols = lambda w: jnp.pad(w, ((0, 0), (0, LANES - w.shape[1])))
    groups = np.arange(d) // FOX_HEAD_DIM
    gsum = (groups[:, None] == np.arange(LANES)[None, :]).astype(np.float32)
    return dict(
        ffn_norm=ffn_norm, mix_norm=mix_norm, kv_norm=kv_norm,
        w_gu=_tile_cols(w_ffn_gu.reshape((-1,) + w_ffn_gu.shape[2:]), FFN_TF, rows=REPACK_ROWS),
        w_down=w_ffn_down.astype(BF16),
        a_wq=w_in[:, :qk].astype(BF16), a_wk=w_in[:, qk:2 * qk].astype(BF16),
        a_wv=w_in[:, 2 * qk:2 * qk + vw].astype(BF16),
        a_wr=w_in[:, 2 * qk + vw:2 * qk + 2 * vw].astype(BF16),
        a_wgl=pad_cols(w_in[:, 2 * qk + 2 * vw:]).astype(BF16),
        a_wg2=jnp.pad(a_w_g2[0], ((0, LANES - rank), (0, 0))).astype(BF16),
        a_bg=a_b_g[0], a_gout=a_g_out[0], a_wo=a_w_o[0].astype(BF16),
        kv_wk=w_kvf[:, :d].astype(BF16), kv_wv=w_kvf[:, d:2 * d].astype(BF16),
        kv_wv_t=w_kvf[:, d:2 * d].T.astype(BF16),
        kv_wf=pad_cols(w_kvf[:, 2 * d:]).astype(BF16),
        kv_bf=jnp.pad(b_f, (0, LANES - b_f.shape[0])).reshape(1, LANES),
        gk=jnp.tile(g_k, FOX_HEADS).reshape(1, d), gq=jnp.tile(b_g_q[0], FOX_HEADS).reshape(1, d),
        b_wq=b_w_qg[0][:, :d].astype(BF16), b_wg=b_w_qg[0][:, d:].astype(BF16),
        b_wo=b_w_o[0].astype(BF16),
        gsum=jnp.asarray(gsum, BF16), gexp=jnp.asarray(gsum.T, BF16),
    )


def _trunk(x, s0, past, w):
    bsz, t, d = x.shape
    n = bsz * t
    h = x.reshape(n, d)
    ffn = lambda h_, layer, half: _ffn(h_, w["ffn_norm"][layer, half], w["w_gu"][2 * layer + half],
                                       w["w_down"][layer, half], tm=TM)
    h = ffn(h, 0, 0)
    q, k, v, r, la = _gla_in(h, w["mix_norm"][0], w["a_wq"], w["a_wk"], w["a_wv"], w["a_wr"],
                             w["a_wgl"], w["a_wg2"], w["a_bg"], tm=TM)
    s3 = lambda z: z.reshape(bsz, t, z.shape[1])
    chunk = 64
    og, st_fin = _gla(s3(q), s3(k), s3(v), s3(r), s3(la), jnp.swapaxes(s0, 2, 3), w["a_gout"],
                      chunk=chunk, tb=GLA_TB)
    h = _proj_res(h, og.reshape(n, -1), w["a_wo"], tm=TM)
    h = ffn(h, 0, 1)
    tq = min(ATTN_TQ, t)
    tk = min(ATTN_TK, tq // 2)
    k_new, v_new, lf, k16, v16 = _shared_kv(
        h, w["kv_norm"], w["kv_wk"], w["kv_wv_t"] if past is None else w["kv_wv"], w["kv_wf"],
        w["kv_bf"], w["gk"], w["gsum"], w["gexp"], tm=TM, vt_block=tk if past is None else None)
    h = ffn(h, 1, 0)
    q16, sg = _fox_q(h, w["mix_norm"][1], w["b_wq"], w["b_wg"], w["gq"], w["gsum"], w["gexp"], tm=TM)
    pairs4 = lambda z: z.reshape(FOX_PAIRS, bsz, t, LANES)
    if past is None:
        lf_all = s3(lf)
    else:
        past_k, past_v, past_lf = past
        p_len = past_k.shape[1]
        past_pairs = lambda a: jnp.transpose(
            a.astype(BF16).reshape(bsz, p_len, FOX_PAIRS, LANES), (2, 0, 1, 3))
        lf_all = jnp.concatenate(
            [jnp.pad(past_lf, ((0, 0), (0, 0), (0, LANES - past_lf.shape[2]))), s3(lf)], axis=1)
    bias_tb = next(c for c in BIAS_TB_CHOICES if lf_all.shape[1] % c == 0)
    qb, kb = _fox_bias(lf_all, tb=bias_tb)
    if past is None:
        og = _fox_attn(s3(q16), qb, pairs4(k16), kb, v16, s3(sg), tq=tq, tk=tk)
    else:
        og = _fox_attn_cached(s3(q16), qb, kb, past_pairs(past_k), past_pairs(past_v), k16, v16,
                              s3(sg), tk=min(CACHE_TK, p_len))
    h = _proj_res(h, og.reshape(n, d), w["b_wo"], tm=TM)
    h = ffn(h, 1, 1)
    heads4 = lambda z: z.reshape(bsz, t, FOX_HEADS, FOX_HEAD_DIM)
    return (h.reshape(bsz, t, d), jnp.swapaxes(st_fin, 2, 3)[:, None], heads4(k_new), heads4(v_new),
            lf[:, :FOX_HEADS].reshape(bsz, t, FOX_HEADS))


def kernel(x_prompt, x_sample, state_gla, cache_k, cache_v, cache_logf, ffn_norm, w_ffn_gu, w_ffn_down, mix_norm, a_w_in, a_w_g2, a_b_g, a_g_out, a_w_o, kv_norm, w_kvf, b_f, g_k, b_w_qg, b_g_q, b_w_o):
    w = _prep_weights(ffn_norm, w_ffn_gu, w_ffn_down, mix_norm, a_w_in, a_w_g2, a_b_g, a_g_out,
                      a_w_o, kv_norm, w_kvf, b_f, g_k, b_w_qg, b_g_q, b_w_o)
    s0_prompt = jnp.zeros((x_prompt.shape[0],) + state_gla.shape[2:], F32)
    y_p, gla_p, k_p, v_p, lf_p = _trunk(x_prompt, s0_prompt, None, w)
    y_s, gla_s, k_s, v_s, lf_s = _trunk(x_sample, state_gla[:, 0], (cache_k, cache_v, cache_logf), w)
    return (y_p, y_s, gla_p, gla_s, k_p, v_p, lf_p, k_s, v_s, lf_s)
```

```python
import functools

import jax
import jax.numpy as jnp
import numpy as np
from jax import lax
from jax.experimental import pallas as pl
from jax.experimental.pallas import tpu as pltpu

F32 = jnp.float32
BF16 = jnp.bfloat16

EPS = 1e-6
NEG_INF = -1e30
LOG2E = 1.4426950408889634

LANES = 128
V7X_VMEM_BYTES = 64 * 1024 * 1024
V7X_SCOPED_VMEM_CAP = 60000 * 1024

GLA_HEADS = 4
GLA_GATE_TAU = 16.0
FOX_HEADS = 16
FOX_HEAD_DIM = 64
FOX_PAIRS = FOX_HEADS // 2
BIAS_PARTS = 3
BIAS_LANES_PER_HEAD = 2 * BIAS_PARTS

NT_DIMS = (((1,), (1,)), ((), ()))
TN_DIMS = (((0,), (0,)), ((), ()))


def _cparams(semantics, vmem_bytes):
    limit = int(min(max(vmem_bytes, 16 * 1024 * 1024), V7X_SCOPED_VMEM_CAP))
    return pltpu.CompilerParams(dimension_semantics=semantics, vmem_limit_bytes=limit)


def _dot(a, b):
    return jnp.dot(a, b, preferred_element_type=F32)


def _rms(x, g):
    ms = jnp.mean(x * x, axis=-1, keepdims=True)
    return x * lax.rsqrt(ms + EPS) * g


def _split2(x):
    hi = x.astype(BF16)
    lo = (x - hi.astype(F32)).astype(BF16)
    return hi, lo


def _split3(x):
    p1 = x.astype(BF16)
    r1 = x - p1.astype(F32)
    p2 = r1.astype(BF16)
    p3 = (r1 - p2.astype(F32)).astype(BF16)
    return p1, p2, p3


def _log_sigmoid(x):
    return jnp.minimum(x, 0.0) - jnp.log(1.0 + jnp.exp(-jnp.abs(x)))


def _head_norm(x, gsum_ref, gexp_ref, gain, head_dim):
    ssq = _dot((x * x).astype(BF16), gsum_ref[...])
    inv = lax.rsqrt(ssq * (1.0 / head_dim) + EPS)
    ihi, ilo = _split2(inv)
    inv_full = _dot(ihi, gexp_ref[...]) + _dot(ilo, gexp_ref[...])
    return x * inv_full * gain


def _ffn_body(x_ref, g_ref, wg_ref, wu_ref, wd_ref, o_ref, xn_ref):
    j = pl.program_id(1)

    @pl.when(j == 0)
    def _():
        x = x_ref[...]
        xn_ref[...] = _rms(x, g_ref[...]).astype(BF16)
        o_ref[...] = x

    xn = xn_ref[...]
    g = _dot(xn, wg_ref[...])
    u = _dot(xn, wu_ref[...])
    h = (0.5 * (g * jax.nn.sigmoid(g) * u)).astype(BF16)
    o_ref[...] += _dot(h, wd_ref[...])


def _tile_cols_body(w_ref, o_ref):
    tf = o_ref.shape[2]
    for c in range(o_ref.shape[0]):
        o_ref[c] = w_ref[:, c * tf:(c + 1) * tf].astype(o_ref.dtype)


def _tile_cols(w, tf, *, rows):
    g, d, n = w.shape
    assert n % tf == 0 and d % rows == 0
    return pl.pallas_call(
        _tile_cols_body,
        out_shape=jax.ShapeDtypeStruct((g, n // tf, d, tf), BF16),
        grid=(g, d // rows),
        in_specs=[pl.BlockSpec((None, rows, n), lambda a, r: (a, r, 0))],
        out_specs=pl.BlockSpec((None, n // tf, rows, tf), lambda a, r: (a, 0, r, 0)),
        compiler_params=_cparams(("parallel", "parallel"), 2 * rows * n * (4 + 2) + (4 << 20)),
        name="tile_cols",
    )(w)


def _ffn(x, g, w_gu, w_down, *, tm):
    n, d = x.shape
    d_ff = w_down.shape[0]
    tf = w_gu.shape[2]
    tm = min(tm, n)
    nf = d_ff // tf
    assert n % tm == 0 and w_gu.shape[0] == 2 * nf
    vmem = 4 * tm * d * 4 + tm * d * 2 + 2 * 3 * d * tf * 2 + 3 * tm * tf * 4 + tm * d * 4
    return pl.pallas_call(
        _ffn_body,
        out_shape=jax.ShapeDtypeStruct((n, d), F32),
        grid=(n // tm, nf),
        in_specs=[
            pl.BlockSpec((tm, d), lambda i, j: (i, 0)),
            pl.BlockSpec((1, d), lambda i, j: (0, 0)),
            pl.BlockSpec((None, d, tf), lambda i, j: (j, 0, 0)),
            pl.BlockSpec((None, d, tf), lambda i, j: (j + nf, 0, 0)),
            pl.BlockSpec((tf, d), lambda i, j: (j, 0)),
        ],
        out_specs=pl.BlockSpec((tm, d), lambda i, j: (i, 0)),
        scratch_shapes=[pltpu.VMEM((tm, d), BF16)],
        compiler_params=_cparams(("parallel", "arbitrary"), vmem + (8 << 20)),
        name="ffn",
    )(x, g.reshape(1, d), w_gu, w_gu, w_down)


def _proj_res_body(h_ref, a_ref, w_ref, o_ref):
    o_ref[...] = h_ref[...] + _dot(a_ref[...], w_ref[...])


def _proj_res(h, a, w, *, tm):
    n, d = h.shape
    k = a.shape[1]
    tm = min(tm, n)
    assert n % tm == 0
    vmem = 2 * (2 * tm * d * 4 + tm * k * 2 + k * d * 2) + tm * d * 4
    return pl.pallas_call(
        _proj_res_body,
        out_shape=jax.ShapeDtypeStruct((n, d), F32),
        grid=(n // tm,),
        in_specs=[
            pl.BlockSpec((tm, d), lambda i: (i, 0)),
            pl.BlockSpec((tm, k), lambda i: (i, 0)),
            pl.BlockSpec((k, d), lambda i: (0, 0)),
        ],
        out_specs=pl.BlockSpec((tm, d), lambda i: (i, 0)),
        compiler_params=_cparams(("parallel",), vmem + (4 << 20)),
        name="proj_res",
    )(h, a, w)


def _gla_in_body(x_ref, g_ref, wq_ref, wk_ref, wv_ref, wr_ref, wgl_ref, wg2_ref, bg_ref,
                 q_ref, k_ref, v_ref, r_ref, la_ref):
    xn = _rms(x_ref[...], g_ref[...]).astype(BF16)
    q_ref[...] = _dot(xn, wq_ref[...]).astype(q_ref.dtype)
    k_ref[...] = _dot(xn, wk_ref[...]).astype(k_ref.dtype)
    v_ref[...] = _dot(xn, wv_ref[...]).astype(v_ref.dtype)
    r_ref[...] = _dot(xn, wr_ref[...]).astype(r_ref.dtype)
    gl = _dot(xn, wgl_ref[...]).astype(BF16)
    z = _dot(gl, wg2_ref[...]) + bg_ref[...]
    la_ref[...] = _log_sigmoid(z) * (1.0 / GLA_GATE_TAU)


def _gla_in(x, g, wq, wk, wv, wr, wgl, wg2, bg, *, tm):
    n, d = x.shape
    qk, vw = wq.shape[1], wv.shape[1]
    tm = min(tm, n)
    assert n % tm == 0
    row = lambda i: (i, 0)
    fix = lambda i: (0, 0)
    w_bytes = 2 * (2 * d * qk + 2 * d * vw + d * LANES + LANES * qk)
    vmem = 2 * tm * (d + 3 * qk + 2 * vw) * 4 + 2 * w_bytes + tm * d * 2
    return pl.pallas_call(
        _gla_in_body,
        out_shape=[jax.ShapeDtypeStruct((n, qk), BF16), jax.ShapeDtypeStruct((n, qk), BF16),
                   jax.ShapeDtypeStruct((n, vw), BF16), jax.ShapeDtypeStruct((n, vw), BF16),
                   jax.ShapeDtypeStruct((n, qk), F32)],
        grid=(n // tm,),
        in_specs=[
            pl.BlockSpec((tm, d), row), pl.BlockSpec((1, d), fix),
            pl.BlockSpec((d, qk), fix), pl.BlockSpec((d, qk), fix),
            pl.BlockSpec((d, vw), fix), pl.BlockSpec((d, vw), fix),
            pl.BlockSpec((d, LANES), fix), pl.BlockSpec((LANES, qk), fix), pl.BlockSpec((1, qk), fix),
        ],
        out_specs=[pl.BlockSpec((tm, qk), row), pl.BlockSpec((tm, qk), row),
                   pl.BlockSpec((tm, vw), row), pl.BlockSpec((tm, vw), row),
                   pl.BlockSpec((tm, qk), row)],
        compiler_params=_cparams(("parallel",), vmem + (8 << 20)),
        name="gla_in",
    )(x, g.reshape(1, d), wq, wk, wv, wr, wgl, wg2, bg.reshape(1, qk))


def _gla_body(q_ref, k_ref, v_ref, r_ref, la_ref, s0_ref, gout_ref, tri_ref,
              og_ref, sfin_ref, st_ref, *, chunk, n_chunks, heads, dk, dv):
    t = pl.program_id(1)

    @pl.when(t == 0)
    def _():
        st_ref[...] = s0_ref[...]

    scale = dk ** -0.5
    row = lax.broadcasted_iota(jnp.int32, (chunk, chunk), 0)
    col = lax.broadcasted_iota(jnp.int32, (chunk, chunk), 1)
    causal = col <= row
    tri = tri_ref[...]
    gout = gout_ref[...]

    for c in range(n_chunks):
        sl = slice(c * chunk, (c + 1) * chunk)
        la_hi, la_lo = _split2(la_ref[sl, :])
        b = _dot(tri, la_hi) + _dot(tri, la_lo)
        b_last = b[chunk - 1:chunk, :]
        q = q_ref[sl, :].astype(F32)
        k = k_ref[sl, :].astype(F32)
        qe = (q * scale * jnp.exp(b)).astype(BF16)
        ke = (k * jnp.exp(-b)).astype(BF16)
        kd = (k * jnp.exp(b_last - b)).astype(BF16)
        dec = jnp.exp(b_last)
        for h in range(heads):
            ks = slice(h * dk, (h + 1) * dk)
            vs = slice(h * dv, (h + 1) * dv)
            vh = v_ref[sl, vs]
            att = lax.dot_general(qe[:, ks], ke[:, ks], NT_DIMS, preferred_element_type=F32)
            att = jnp.where(causal, att, 0.0).astype(BF16)
            st = st_ref[h]
            o = _dot(att, vh) + lax.dot_general(qe[:, ks], st.astype(BF16), NT_DIMS,
                                                preferred_element_type=F32)
            st_ref[h] = st * dec[:, ks] + lax.dot_general(vh, kd[:, ks], TN_DIMS,
                                                          preferred_element_type=F32)
            on = _rms(o, gout)
            rh = r_ref[sl, vs].astype(F32)
            og_ref[sl, vs] = (on * (rh * jax.nn.sigmoid(rh))).astype(BF16)

    @pl.when(t == pl.num_programs(1) - 1)
    def _():
        sfin_ref[...] = st_ref[...]


def _gla(q, k, v, r, la, s0t, gout, *, chunk, tb):
    bsz, t, qk = q.shape
    vw = v.shape[2]
    heads = s0t.shape[1]
    dk, dv = qk // heads, vw // heads
    tb = min(tb, t)
    assert t % tb == 0 and tb % chunk == 0
    tri = jnp.tril(jnp.ones((chunk, chunk), F32)).astype(BF16)
    blk = lambda w: pl.BlockSpec((None, tb, w), lambda b, i: (b, i, 0))
    st_spec = pl.BlockSpec((None, heads, dv, dk), lambda b, i: (b, 0, 0, 0))
    vmem = 2 * tb * (3 * qk + 2 * vw) * 4 + 2 * tb * vw * 2 + 5 * heads * dv * dk * 4
    return pl.pallas_call(
        functools.partial(_gla_body, chunk=chunk, n_chunks=tb // chunk, heads=heads, dk=dk, dv=dv),
        out_shape=[jax.ShapeDtypeStruct((bsz, t, vw), BF16),
                   jax.ShapeDtypeStruct((bsz, heads, dv, dk), F32)],
        grid=(bsz, t // tb),
        in_specs=[blk(qk), blk(qk), blk(vw), blk(vw), blk(qk), st_spec,
                  pl.BlockSpec((1, dv), lambda b, i: (0, 0)),
                  pl.BlockSpec((chunk, chunk), lambda b, i: (0, 0))],
        out_specs=[blk(vw), st_spec],
        scratch_shapes=[pltpu.VMEM((heads, dv, dk), F32)],
        compiler_params=_cparams(("parallel", "arbitrary"), vmem + (8 << 20)),
        name="gla_chunks",
    )(q, k, v, r, la, s0t, gout.reshape(1, dv), tri)


def _store_pairs(ref, x):
    for p in range(ref.shape[0]):
        ref[p] = x[:, p * LANES:(p + 1) * LANES]


def _kv_body(x_ref, g_ref, wk_ref, wv_ref, wf_ref, bf_ref, gk_ref, gsum_ref, gexp_ref,
             k_ref, v_ref, lf_ref, k16_ref, v16_ref, *, vt_block):
    xn = _rms(x_ref[...], g_ref[...]).astype(BF16)
    k = _head_norm(_dot(xn, wk_ref[...]), gsum_ref, gexp_ref, gk_ref[...], FOX_HEAD_DIM)
    k_ref[...] = k.reshape(k_ref.shape)
    _store_pairs(k16_ref, k.astype(BF16))
    lf_ref[...] = _log_sigmoid(_dot(xn, wf_ref[...]) + bf_ref[...])
    if vt_block is None:
        v = _dot(xn, wv_ref[...])
        v_ref[...] = v.reshape(v_ref.shape)
        _store_pairs(v16_ref, v.astype(BF16))
    else:
        vt = lax.dot_general(wv_ref[...], xn, NT_DIMS, preferred_element_type=F32)
        v_ref[...] = vt.T.reshape(v_ref.shape)
        vt16 = vt.astype(BF16)
        for c in range(v16_ref.shape[0]):
            v16_ref[c] = vt16[:, c * vt_block:(c + 1) * vt_block]


def _shared_kv(x, g, wk, wv, wf, bf, gk, gsum, gexp, *, tm, vt_block=None):
    n, d = x.shape
    tm = min(tm, n)
    assert n % tm == 0
    row = lambda i: (i, 0)
    fix = lambda i: (0, 0)
    heads_shape = (n, FOX_HEADS, FOX_HEAD_DIM)
    heads_spec = pl.BlockSpec((tm, FOX_HEADS, FOX_HEAD_DIM), lambda i: (i, 0, 0))
    pair_shape = (d // LANES, n, LANES)
    pair_spec = pl.BlockSpec((d // LANES, tm, LANES), lambda i: (0, i, 0))
    if vt_block is None:
        v16_shape, v16_spec = pair_shape, pair_spec
    else:
        assert tm % vt_block == 0
        v16_shape = (n // vt_block, d, vt_block)
        v16_spec = pl.BlockSpec((tm // vt_block, d, vt_block), lambda i: (i, 0, 0))
    vmem = 2 * tm * d * (4 + 8 + 8 + 2 + 2) + 2 * tm * LANES * 4 + 2 * 2 * (2 * d * d + 3 * d * LANES) \
        + 8 * tm * d * 4
    return pl.pallas_call(
        functools.partial(_kv_body, vt_block=vt_block),
        out_shape=[jax.ShapeDtypeStruct(heads_shape, F32), jax.ShapeDtypeStruct(heads_shape, F32),
                   jax.ShapeDtypeStruct((n, LANES), F32),
                   jax.ShapeDtypeStruct(pair_shape, BF16), jax.ShapeDtypeStruct(v16_shape, BF16)],
        grid=(n // tm,),
        in_specs=[
            pl.BlockSpec((tm, d), row), pl.BlockSpec((1, d), fix),
            pl.BlockSpec((d, d), fix), pl.BlockSpec((d, d), fix), pl.BlockSpec((d, LANES), fix),
            pl.BlockSpec((1, LANES), fix), pl.BlockSpec((1, d), fix),
            pl.BlockSpec((d, LANES), fix), pl.BlockSpec((LANES, d), fix),
        ],
        out_specs=[heads_spec, heads_spec,
                   pl.BlockSpec((tm, LANES), row),
                   pair_spec, v16_spec],
        compiler_params=_cparams(("parallel",), vmem),
        name="shared_kv",
    )(x, g.reshape(1, d), wk, wv, wf, bf, gk, gsum, gexp)


def _fox_q_body(x_ref, g_ref, wq_ref, wg_ref, gq_ref, gsum_ref, gexp_ref, q16_ref, sg_ref):
    xn = _rms(x_ref[...], g_ref[...]).astype(BF16)
    q = _head_norm(_dot(xn, wq_ref[...]), gsum_ref, gexp_ref, gq_ref[...], FOX_HEAD_DIM)
    q16_ref[...] = (q * (FOX_HEAD_DIM ** -0.5 * LOG2E)).astype(BF16)
    sg_ref[...] = jax.nn.sigmoid(_dot(xn, wg_ref[...])).astype(sg_ref.dtype)


def _fox_q(x, g, wq, wg, gq, gsum, gexp, *, tm):
    n, d = x.shape
    tm = min(tm, n)
    assert n % tm == 0
    row = lambda i: (i, 0)
    fix = lambda i: (0, 0)
    vmem = 2 * tm * d * (4 + 2 + 4) + 2 * 2 * (2 * d * d + 2 * d * LANES) + 6 * tm * d * 4
    return pl.pallas_call(
        _fox_q_body,
        out_shape=[jax.ShapeDtypeStruct((n, d), BF16), jax.ShapeDtypeStruct((n, d), BF16)],
        grid=(n // tm,),
        in_specs=[
            pl.BlockSpec((tm, d), row), pl.BlockSpec((1, d), fix),
            pl.BlockSpec((d, d), fix), pl.BlockSpec((d, d), fix), pl.BlockSpec((1, d), fix),
            pl.BlockSpec((d, LANES), fix), pl.BlockSpec((LANES, d), fix),
        ],
        out_specs=[pl.BlockSpec((tm, d), row), pl.BlockSpec((tm, d), row)],
        compiler_params=_cparams(("parallel",), vmem),
        name="fox_q",
    )(x, g.reshape(1, d), wq, wg, gq, gsum, gexp)


def _pack_parts(x):
    p1, p2, p3 = (p.astype(F32) for p in _split3(x))
    packed = p1 + pltpu.roll(p2, FOX_HEADS, 1) + pltpu.roll(p3, 2 * FOX_HEADS, 1)
    return packed.astype(BF16)


def _bias_body(lf_ref, tri_ref, sel_ref, one_ref, qb_ref, kb_ref, carry_ref):
    @pl.when(pl.program_id(1) == 0)
    def _():
        carry_ref[...] = jnp.zeros_like(carry_ref)

    lf = lf_ref[...]
    tb = lf.shape[0]
    head_lanes = lax.broadcasted_iota(jnp.int32, lf.shape, 1) < FOX_HEADS
    cp = _dot(tri_ref[...], _pack_parts(jnp.where(head_lanes, lf, 0.0)))
    c = cp + pltpu.roll(cp, LANES - FOX_HEADS, 1) + pltpu.roll(cp, LANES - 2 * FOX_HEADS, 1)
    c = jnp.where(head_lanes, c, 0.0) + carry_ref[...]
    carry_ref[...] = c[tb - 1:tb, :]
    qkb = _dot(_pack_parts(c * LOG2E), sel_ref[...]) + one_ref[...]
    wide = FOX_PAIRS * LANES
    for p in range(FOX_PAIRS):
        qb_ref[p] = qkb[:, p * LANES:(p + 1) * LANES].astype(BF16)
        kb_ref[p] = qkb[:, wide + p * LANES:wide + (p + 1) * LANES].astype(BF16)


def _bias_constants():
    wide = FOX_PAIRS * LANES
    sel = np.zeros((LANES, 2 * wide), np.float32)
    one = np.zeros((1, 2 * wide), np.float32)
    for p in range(FOX_PAIRS):
        for e in range(2):
            base = p * LANES + e * BIAS_LANES_PER_HEAD
            for part in range(BIAS_PARTS):
                row = part * FOX_HEADS + 2 * p + e
                sel[row, base + part] = 1.0
                one[0, base + BIAS_PARTS + part] = 1.0
                one[0, wide + base + part] = 1.0
                sel[row, wide + base + BIAS_PARTS + part] = -1.0
    return jnp.asarray(sel, BF16), jnp.asarray(one)


def _fox_bias(lf, *, tb):
    bsz, kp, _ = lf.shape
    assert kp % tb == 0
    sel, one = _bias_constants()
    tri = jnp.tril(jnp.ones((tb, tb), F32)).astype(BF16)
    wide = FOX_PAIRS * LANES
    out_spec = pl.BlockSpec((None, FOX_PAIRS, tb, LANES), lambda b, i: (b, 0, i, 0))
    vmem = 2 * tb * LANES * 4 + 2 * tb * tb * 2 + 2 * LANES * 2 * wide * 2 \
        + 4 * FOX_PAIRS * tb * LANES * 2 + 4 * tb * 2 * wide * 4
    return pl.pallas_call(
        _bias_body,
        out_shape=[jax.ShapeDtypeStruct((bsz, FOX_PAIRS, kp, LANES), BF16)] * 2,
        grid=(bsz, kp // tb),
        in_specs=[
            pl.BlockSpec((None, tb, LANES), lambda b, i: (b, i, 0)),
            pl.BlockSpec((tb, tb), lambda b, i: (0, 0)),
            pl.BlockSpec((LANES, 2 * wide), lambda b, i: (0, 0)),
            pl.BlockSpec((1, 2 * wide), lambda b, i: (0, 0)),
        ],
        out_specs=[out_spec, out_spec],
        scratch_shapes=[pltpu.VMEM((1, LANES), F32)],
        compiler_params=_cparams(("parallel", "arbitrary"), vmem),
        name="fox_bias",
    )(lf, tri, sel, one)


def _q_aug(q, qb, lane):
    zero = jnp.zeros_like(q)
    out = []
    for e in range(2):
        head_lanes = (lane >> 6) == e
        bias_lanes = jnp.logical_and(lane >= e * BIAS_LANES_PER_HEAD,
                                     lane < (e + 1) * BIAS_LANES_PER_HEAD)
        out.append(jnp.concatenate(
            [jnp.where(head_lanes, q, zero), jnp.where(bias_lanes, qb, zero)], axis=1))
    return out


def _attn_body(q_ref, qb_ref, k_ref, kb_ref, v_ref, sg_ref, o_ref, m_ref, l_ref, acc_ref,
               s_ref, *, tq, tk):
    i = pl.program_id(2)
    lane = lax.broadcasted_iota(jnp.int32, (tq, LANES), 1)
    q_aug = _q_aug(q_ref[...], qb_ref[...], lane)

    m_ref[...] = jnp.full(m_ref.shape, NEG_INF, F32)
    l_ref[...] = jnp.zeros(l_ref.shape, F32)
    acc_ref[...] = jnp.zeros(acc_ref.shape, F32)

    def qk_scores(j, qwin=slice(None)):
        ks = pl.multiple_of(j * tk, tk)
        k_aug = jnp.concatenate([k_ref[pl.ds(ks, tk), :], kb_ref[pl.ds(ks, tk), :]], axis=1)
        qs = q_aug if qwin == slice(None) else [qa[qwin] for qa in q_aug]
        return [lax.dot_general(k_aug, qs[e], NT_DIMS, preferred_element_type=F32)
                for e in range(2)]

    def softmax_pv(j, scores, qwin=slice(None)):
        v = v_ref[j]
        if qwin == slice(None):
            at = lambda ref, e: ref.at[e]
        else:
            at = lambda ref, e: ref.at[e, :, qwin]
        for e in range(2):
            s = scores[e]
            m_prev = at(m_ref, e)[...]
            m_new = jnp.maximum(m_prev, jnp.max(s, axis=0, keepdims=True))
            alpha = jnp.exp2(m_prev - m_new)
            p = jnp.exp2(s - m_new)
            at(l_ref, e)[...] = alpha * at(l_ref, e)[...] + jnp.sum(p, axis=0, keepdims=True)
            p16 = p.astype(BF16)
            at(acc_ref, e)[...] = alpha * at(acc_ref, e)[...] + _dot(v, p16)
            at(m_ref, e)[...] = m_new

    def stash(scores):
        for e in range(2):
            s_ref[e] = scores[e]

    def fetch():
        return [s_ref[0], s_ref[1]]

    stash(qk_scores(0))

    def unmasked_pair(jj, carry):
        j = 2 * jj
        odd = qk_scores(j + 1)
        softmax_pv(j, fetch())
        stash(qk_scores(j + 2))
        softmax_pv(j + 1, odd)
        return carry

    lax.fori_loop(0, i, unmasked_pair, 0)
    early, late = slice(0, tk), slice(tk, tq)
    tmask = jnp.where(lax.broadcasted_iota(jnp.int32, (tk, tk), 0)
                      <= lax.broadcasted_iota(jnp.int32, (tk, tk), 1), 0.0, NEG_INF)
    late_scores = qk_scores(2 * i + 1, late)
    softmax_pv(2 * i, [jnp.concatenate([s[:, early] + tmask, s[:, late]], axis=1)
                       for s in fetch()])
    softmax_pv(2 * i + 1, [s + tmask for s in late_scores], late)

    o0 = acc_ref[0] * (1.0 / l_ref[0])
    o1 = acc_ref[1] * (1.0 / l_ref[1])
    feature = lax.broadcasted_iota(jnp.int32, (LANES, tq), 0)
    o = jnp.where((feature >> 6) == 0, o0, o1).T
    o_ref[...] = (o * sg_ref[...].astype(F32)).astype(BF16)


def _fox_attn(q16, qb, k16, kb, vt16, sg, *, tq, tk):
    bsz, t, d = q16.shape
    assert t % tq == 0 and tq == 2 * tk and k16.shape[2] == t
    qspec = pl.BlockSpec((None, tq, LANES), lambda b, p, i: (b, i, p))
    vmem = 2 * (3 * t * LANES * 2 + tq * LANES * (2 + 2 + 2 + 2)) + 3 * tq * LANES * 4 \
        + 2 * tk * 2 * LANES * 2 + 10 * tq * tk * 4
    return pl.pallas_call(
        functools.partial(_attn_body, tq=tq, tk=tk),
        out_shape=jax.ShapeDtypeStruct((bsz, t, d), BF16),
        grid=(bsz, FOX_PAIRS, t // tq),
        in_specs=[
            qspec,
            pl.BlockSpec((None, None, tq, LANES), lambda b, p, i: (b, p, i, 0)),
            pl.BlockSpec((None, None, t, LANES), lambda b, p, i: (p, b, 0, 0)),
            pl.BlockSpec((None, None, t, LANES), lambda b, p, i: (b, p, 0, 0)),
            pl.BlockSpec((t // tk, LANES, tk), lambda b, p, i: (b, p, 0)),
            qspec,
        ],
        out_specs=qspec,
        scratch_shapes=[pltpu.VMEM((2, 1, tq), F32), pltpu.VMEM((2, 1, tq), F32),
                        pltpu.VMEM((2, LANES, tq), F32),
                        pltpu.VMEM((2, tk, tq), F32)],
        compiler_params=_cparams(("parallel", "parallel", "arbitrary"), vmem + (8 << 20)),
        name="fox_attn",
    )(q16, qb, k16, kb, vt16, sg)


def _cache_attn_body(q_ref, qb_ref, ck_ref, cv_ref, kn_ref, vn_ref, kb_ref, sg_ref,
                     o_ref, m_ref, l_ref, acc_ref, *, tk):
    t = q_ref.shape[0]
    p_len = ck_ref.shape[0]
    lane = lax.broadcasted_iota(jnp.int32, (t, LANES), 1)
    q_aug = _q_aug(q_ref[...], qb_ref[...], lane)
    m_ref[...] = jnp.full(m_ref.shape, NEG_INF, F32)
    l_ref[...] = jnp.zeros(l_ref.shape, F32)
    acc_ref[...] = jnp.zeros(acc_ref.shape, F32)

    def fold(k, kb, v, mask):
        k_aug = jnp.concatenate([k, kb], axis=1)
        scores = [lax.dot_general(q_aug[e], k_aug, NT_DIMS, preferred_element_type=F32)
                  for e in range(2)]
        for e in range(2):
            s = scores[e] if mask is None else scores[e] + mask
            m_prev = m_ref[e]
            m_new = jnp.maximum(m_prev, jnp.max(s, axis=1, keepdims=True))
            alpha = jnp.exp2(m_prev - m_new)
            p = jnp.exp2(s - m_new)
            l_ref[e] = alpha * l_ref[e] + jnp.sum(p, axis=1, keepdims=True)
            acc_ref[e] = alpha * acc_ref[e] + _dot(p.astype(BF16), v)
            m_ref[e] = m_new

    for j in range(p_len // tk):
        rows = slice(j * tk, (j + 1) * tk)
        fold(ck_ref[rows, :], kb_ref[rows, :], cv_ref[rows, :], None)
    causal = jnp.where(lax.broadcasted_iota(jnp.int32, (t, t), 1)
                       <= lax.broadcasted_iota(jnp.int32, (t, t), 0), 0.0, NEG_INF)
    fold(kn_ref[...], kb_ref[p_len:p_len + t, :], vn_ref[...], causal)

    o = jnp.where((lane >> 6) == 0, acc_ref[0] * (1.0 / l_ref[0]), acc_ref[1] * (1.0 / l_ref[1]))
    o_ref[...] = (o * sg_ref[...].astype(F32)).astype(BF16)


def _fox_attn_cached(q16, qb, kb, ck16, cv16, k16, v16, sg, *, tk):
    bsz, t, d = q16.shape
    p_len = ck16.shape[2]
    assert p_len % tk == 0 and p_len % t == 0 and kb.shape[2] == p_len + t
    qspec = pl.BlockSpec((None, t, LANES), lambda b, p: (b, 0, p))
    cspec = pl.BlockSpec((None, None, p_len, LANES), lambda b, p: (p, b, 0, 0))
    nspec = pl.BlockSpec((None, t, LANES), lambda b, p: (p, b, 0))
    vmem = 2 * (3 * (p_len + t) * LANES * 2 + t * LANES * 8) + 6 * t * LANES * 4 + 8 * t * tk * 4 \
        + 4 * tk * LANES * 2
    return pl.pallas_call(
        functools.partial(_cache_attn_body, tk=tk),
        out_shape=jax.ShapeDtypeStruct((bsz, t, d), BF16),
        grid=(bsz, FOX_PAIRS),
        in_specs=[
            qspec,
            pl.BlockSpec((None, None, t, LANES), lambda b, p: (b, p, p_len // t, 0)),
            cspec, cspec, nspec, nspec,
            pl.BlockSpec((None, None, p_len + t, LANES), lambda b, p: (b, p, 0, 0)),
            qspec,
        ],
        out_specs=qspec,
        scratch_shapes=[pltpu.VMEM((2, t, 1), F32), pltpu.VMEM((2, t, 1), F32),
                        pltpu.VMEM((2, t, LANES), F32)],
        compiler_params=_cparams(("parallel", "parallel"), vmem + (8 << 20)),
        name="fox_attn_cached",
    )(q16, qb, ck16, cv16, k16, v16, kb, sg)


TM = 1024
KV_TM = 512
FFN_TM = 1024
FFN_TF = 256
REPACK_ROWS = 128
GLA_TB = 256
ATTN_TQ = 1024
ATTN_TK = 512
CACHE_TK = 4096
BIAS_TB_CHOICES = (832, 512, 384, 320, 256, 192, 128, 64)


def _round_up(x, m):
    return (x + m - 1) // m * m


def _prep_weights(ffn_norm, w_ffn_gu, w_ffn_down, mix_norm, a_w_in, a_w_g2, a_b_g, a_g_out, a_w_o,
                  kv_norm, w_kvf, b_f, g_k, b_w_qg, b_g_q, b_w_o):
    d = w_ffn_gu.shape[2]
    rank = a_w_g2.shape[1]
    qk = a_w_g2.shape[2]
    vw = a_w_o.shape[1]
    w_in = a_w_in[0]
    pad_cols = lambda w: jnp.pad(w, ((0, 0), (0, LANES - w.shape[1])))
    groups = np.arange(d) // FOX_HEAD_DIM
    gsum = (groups[:, None] == np.arange(LANES)[None, :]).astype(np.float32)
    return dict(
        ffn_norm=ffn_norm, mix_norm=mix_norm, kv_norm=kv_norm,
        w_gu=_tile_cols(w_ffn_gu.reshape((-1,) + w_ffn_gu.shape[2:]), FFN_TF, rows=REPACK_ROWS),
        w_down=w_ffn_down.astype(BF16),
        a_wq=w_in[:, :qk].astype(BF16), a_wk=w_in[:, qk:2 * qk].astype(BF16),
        a_wv=w_in[:, 2 * qk:2 * qk + vw].astype(BF16),
        a_wr=w_in[:, 2 * qk + vw:2 * qk + 2 * vw].astype(BF16),
        a_wgl=pad_cols(w_in[:, 2 * qk + 2 * vw:]).astype(BF16),
        a_wg2=jnp.pad(a_w_g2[0], ((0, LANES - rank), (0, 0))).astype(BF16),
        a_bg=a_b_g[0], a_gout=a_g_out[0], a_wo=a_w_o[0].astype(BF16),
        kv_wk=w_kvf[:, :d].astype(BF16), kv_wv=w_kvf[:, d:2 * d].astype(BF16),
        kv_wv_t=w_kvf[:, d:2 * d].T.astype(BF16),
        kv_wf=pad_cols(w_kvf[:, 2 * d:]).astype(BF16),
        kv_bf=jnp.pad(b_f, (0, LANES - b_f.shape[0])).reshape(1, LANES),
        gk=jnp.tile(g_k, FOX_HEADS).reshape(1, d), gq=jnp.tile(b_g_q[0], FOX_HEADS).reshape(1, d),
        b_wq=b_w_qg[0][:, :d].astype(BF16), b_wg=b_w_qg[0][:, d:].astype(BF16),
        b_wo=b_w_o[0].astype(BF16),
        gsum=jnp.asarray(gsum, BF16), gexp=jnp.asarray(gsum.T, BF16),
    )


def _trunk(x, s0, past, w):
    bsz, t, d = x.shape
    n = bsz * t
    h = x.reshape(n, d)
    ffn = lambda h_, layer, half: _ffn(h_, w["ffn_norm"][layer, half], w["w_gu"][2 * layer + half],
                                       w["w_down"][layer, half], tm=FFN_TM)
    h = ffn(h, 0, 0)
    q, k, v, r, la = _gla_in(h, w["mix_norm"][0], w["a_wq"], w["a_wk"], w["a_wv"], w["a_wr"],
                             w["a_wgl"], w["a_wg2"], w["a_bg"], tm=TM)
    s3 = lambda z: z.reshape(bsz, t, z.shape[1])
    chunk = 64
    og, st_fin = _gla(s3(q), s3(k), s3(v), s3(r), s3(la), jnp.swapaxes(s0, 2, 3), w["a_gout"],
                      chunk=chunk, tb=GLA_TB)
    h = _proj_res(h, og.reshape(n, -1), w["a_wo"], tm=TM)
    h = ffn(h, 0, 1)
    tq = min(ATTN_TQ, t)
    tk = min(ATTN_TK, tq // 2)
    k_new, v_new, lf, k16, v16 = _shared_kv(
        h, w["kv_norm"], w["kv_wk"], w["kv_wv_t"] if past is None else w["kv_wv"], w["kv_wf"],
        w["kv_bf"], w["gk"], w["gsum"], w["gexp"], tm=KV_TM, vt_block=tk if past is None else None)
    h = ffn(h, 1, 0)
    q16, sg = _fox_q(h, w["mix_norm"][1], w["b_wq"], w["b_wg"], w["gq"], w["gsum"], w["gexp"], tm=TM)
    pairs4 = lambda z: z.reshape(FOX_PAIRS, bsz, t, LANES)
    if past is None:
        lf_all = s3(lf)
    else:
        past_k, past_v, past_lf = past
        p_len = past_k.shape[1]
        past_pairs = lambda a: jnp.transpose(
            a.astype(BF16).reshape(bsz, p_len, FOX_PAIRS, LANES), (2, 0, 1, 3))
        lf_all = jnp.concatenate(
            [jnp.pad(past_lf, ((0, 0), (0, 0), (0, LANES - past_lf.shape[2]))), s3(lf)], axis=1)
    bias_tb = next(c for c in BIAS_TB_CHOICES if lf_all.shape[1] % c == 0)
    qb, kb = _fox_bias(lf_all, tb=bias_tb)
    if past is None:
        og = _fox_attn(s3(q16), qb, pairs4(k16), kb, v16, s3(sg), tq=tq, tk=tk)
    else:
        og = _fox_attn_cached(s3(q16), qb, kb, past_pairs(past_k), past_pairs(past_v), k16, v16,
                              s3(sg), tk=min(CACHE_TK, p_len))
    h = _proj_res(h, og.reshape(n, d), w["b_wo"], tm=TM)
    h = ffn(h, 1, 1)
    heads4 = lambda z: z.reshape(bsz, t, FOX_HEADS, FOX_HEAD_DIM)
    return (h.reshape(bsz, t, d), jnp.swapaxes(st_fin, 2, 3)[:, None], heads4(k_new), heads4(v_new),
            lf[:, :FOX_HEADS].reshape(bsz, t, FOX_HEADS))


def kernel(x_prompt, x_sample, state_gla, cache_k, cache_v, cache_logf, ffn_norm, w_ffn_gu, w_ffn_down, mix_norm, a_w_in, a_w_g2, a_b_g, a_g_out, a_w_o, kv_norm, w_kvf, b_f, g_k, b_w_qg, b_g_q, b_w_o):
    w = _prep_weights(ffn_norm, w_ffn_gu, w_ffn_down, mix_norm, a_w_in, a_w_g2, a_b_g, a_g_out,
                      a_w_o, kv_norm, w_kvf, b_f, g_k, b_w_qg, b_g_q, b_w_o)
    s0_prompt = jnp.zeros((x_prompt.shape[0],) + state_gla.shape[2:], F32)
    y_p, gla_p, k_p, v_p, lf_p = _trunk(x_prompt, s0_prompt, None, w)
    y_s, gla_s, k_s, v_s, lf_s = _trunk(x_sample, state_gla[:, 0], (cache_k, cache_v, cache_logf), w)
    return (y_p, y_s, gla_p, gla_s, k_p, v_p, lf_p, k_s, v_s, lf_s)
```

```python
import functools

import jax
import jax.numpy as jnp
import numpy as np
from jax import lax
from jax.experimental import pallas as pl
from jax.experimental.pallas import tpu as pltpu

F32 = jnp.float32
BF16 = jnp.bfloat16

EPS = 1e-6
NEG_INF = -1e30
LOG2E = 1.4426950408889634

LANES = 128
V7X_VMEM_BYTES = 64 * 1024 * 1024
V7X_SCOPED_VMEM_CAP = 60000 * 1024

GLA_HEADS = 4
GLA_GATE_TAU = 16.0
FOX_HEADS = 16
FOX_HEAD_DIM = 64
FOX_PAIRS = FOX_HEADS // 2
BIAS_PARTS = 3
BIAS_LANES_PER_HEAD = 2 * BIAS_PARTS

NT_DIMS = (((1,), (1,)), ((), ()))
TN_DIMS = (((0,), (0,)), ((), ()))


def _cparams(semantics, vmem_bytes):
    limit = int(min(max(vmem_bytes, 16 * 1024 * 1024), V7X_SCOPED_VMEM_CAP))
    return pltpu.CompilerParams(dimension_semantics=semantics, vmem_limit_bytes=limit)


def _dot(a, b):
    return jnp.dot(a, b, preferred_element_type=F32)


def _rms(x, g):
    ms = jnp.mean(x * x, axis=-1, keepdims=True)
    return x * lax.rsqrt(ms + EPS) * g


def _split2(x):
    hi = x.astype(BF16)
    lo = (x - hi.astype(F32)).astype(BF16)
    return hi, lo


def _split3(x):
    p1 = x.astype(BF16)
    r1 = x - p1.astype(F32)
    p2 = r1.astype(BF16)
    p3 = (r1 - p2.astype(F32)).astype(BF16)
    return p1, p2, p3


def _log_sigmoid(x):
    return jnp.minimum(x, 0.0) - jnp.log(1.0 + jnp.exp(-jnp.abs(x)))


def _head_norm(x, gsum_ref, gexp_ref, gain, head_dim):
    ssq = _dot((x * x).astype(BF16), gsum_ref[...])
    inv = lax.rsqrt(ssq * (1.0 / head_dim) + EPS)
    ihi, ilo = _split2(inv)
    inv_full = _dot(ihi, gexp_ref[...]) + _dot(ilo, gexp_ref[...])
    return x * inv_full * gain


def _ffn_body(x_ref, g_ref, wg_ref, wu_ref, wd_ref, o_ref, xn_ref):
    j = pl.program_id(1)

    @pl.when(j == 0)
    def _():
        x = x_ref[...]
        xn_ref[...] = _rms(x, g_ref[...]).astype(BF16)
        o_ref[...] = x

    xn = xn_ref[...]
    g = _dot(xn, wg_ref[...])
    u = _dot(xn, wu_ref[...])
    h = (0.5 * (g * jax.nn.sigmoid(g) * u)).astype(BF16)
    o_ref[...] += _dot(h, wd_ref[...])


def _tile_cols_body(w_ref, o_ref):
    tf = o_ref.shape[2]
    for c in range(o_ref.shape[0]):
        o_ref[c] = w_ref[:, c * tf:(c + 1) * tf].astype(o_ref.dtype)


def _tile_cols(w, tf, *, rows):
    g, d, n = w.shape
    assert n % tf == 0 and d % rows == 0
    return pl.pallas_call(
        _tile_cols_body,
        out_shape=jax.ShapeDtypeStruct((g, n // tf, d, tf), BF16),
        grid=(g, d // rows),
        in_specs=[pl.BlockSpec((None, rows, n), lambda a, r: (a, r, 0))],
        out_specs=pl.BlockSpec((None, n // tf, rows, tf), lambda a, r: (a, 0, r, 0)),
        compiler_params=_cparams(("parallel", "parallel"), 2 * rows * n * (4 + 2) + (4 << 20)),
        name="tile_cols",
    )(w)


def _ffn(x, g, w_gu, w_down, *, tm):
    n, d = x.shape
    d_ff = w_down.shape[0]
    tf = w_gu.shape[2]
    tm = min(tm, n)
    nf = d_ff // tf
    assert n % tm == 0 and w_gu.shape[0] == 2 * nf
    vmem = 4 * tm * d * 4 + tm * d * 2 + 2 * 3 * d * tf * 2 + 3 * tm * tf * 4 + tm * d * 4
    return pl.pallas_call(
        _ffn_body,
        out_shape=jax.ShapeDtypeStruct((n, d), F32),
        grid=(n // tm, nf),
        in_specs=[
            pl.BlockSpec((tm, d), lambda i, j: (i, 0)),
            pl.BlockSpec((1, d), lambda i, j: (0, 0)),
            pl.BlockSpec((None, d, tf), lambda i, j: (j, 0, 0)),
            pl.BlockSpec((None, d, tf), lambda i, j: (j + nf, 0, 0)),
            pl.BlockSpec((tf, d), lambda i, j: (j, 0)),
        ],
        out_specs=pl.BlockSpec((tm, d), lambda i, j: (i, 0)),
        scratch_shapes=[pltpu.VMEM((tm, d), BF16)],
        compiler_params=_cparams(("parallel", "arbitrary"), vmem + (8 << 20)),
        name="ffn",
    )(x, g.reshape(1, d), w_gu, w_gu, w_down)


def _proj_res_body(h_ref, a_ref, w_ref, o_ref):
    o_ref[...] = h_ref[...] + _dot(a_ref[...], w_ref[...])


def _proj_res(h, a, w, *, tm):
    n, d = h.shape
    k = a.shape[1]
    tm = min(tm, n)
    assert n % tm == 0
    vmem = 2 * (2 * tm * d * 4 + tm * k * 2 + k * d * 2) + tm * d * 4
    return pl.pallas_call(
        _proj_res_body,
        out_shape=jax.ShapeDtypeStruct((n, d), F32),
        grid=(n // tm,),
        in_specs=[
            pl.BlockSpec((tm, d), lambda i: (i, 0)),
            pl.BlockSpec((tm, k), lambda i: (i, 0)),
            pl.BlockSpec((k, d), lambda i: (0, 0)),
        ],
        out_specs=pl.BlockSpec((tm, d), lambda i: (i, 0)),
        compiler_params=_cparams(("parallel",), vmem + (4 << 20)),
        name="proj_res",
    )(h, a, w)


def _gla_in_body(x_ref, g_ref, wq_ref, wk_ref, wv_ref, wr_ref, wgl_ref, wg2_ref, bg_ref,
                 q_ref, k_ref, v_ref, r_ref, la_ref):
    xn = _rms(x_ref[...], g_ref[...]).astype(BF16)
    q_ref[...] = _dot(xn, wq_ref[...]).astype(q_ref.dtype)
    k_ref[...] = _dot(xn, wk_ref[...]).astype(k_ref.dtype)
    v_ref[...] = _dot(xn, wv_ref[...]).astype(v_ref.dtype)
    r_ref[...] = _dot(xn, wr_ref[...]).astype(r_ref.dtype)
    gl = _dot(xn, wgl_ref[...]).astype(BF16)
    z = _dot(gl, wg2_ref[...]) + bg_ref[...]
    la_ref[...] = _log_sigmoid(z) * (1.0 / GLA_GATE_TAU)


def _gla_in(x, g, wq, wk, wv, wr, wgl, wg2, bg, *, tm):
    n, d = x.shape
    qk, vw = wq.shape[1], wv.shape[1]
    tm = min(tm, n)
    assert n % tm == 0
    row = lambda i: (i, 0)
    fix = lambda i: (0, 0)
    w_bytes = 2 * (2 * d * qk + 2 * d * vw + d * LANES + LANES * qk)
    vmem = 2 * tm * (d + 3 * qk + 2 * vw) * 4 + 2 * w_bytes + tm * d * 2
    return pl.pallas_call(
        _gla_in_body,
        out_shape=[jax.ShapeDtypeStruct((n, qk), BF16), jax.ShapeDtypeStruct((n, qk), BF16),
                   jax.ShapeDtypeStruct((n, vw), BF16), jax.ShapeDtypeStruct((n, vw), BF16),
                   jax.ShapeDtypeStruct((n, qk), F32)],
        grid=(n // tm,),
        in_specs=[
            pl.BlockSpec((tm, d), row), pl.BlockSpec((1, d), fix),
            pl.BlockSpec((d, qk), fix), pl.BlockSpec((d, qk), fix),
            pl.BlockSpec((d, vw), fix), pl.BlockSpec((d, vw), fix),
            pl.BlockSpec((d, LANES), fix), pl.BlockSpec((LANES, qk), fix), pl.BlockSpec((1, qk), fix),
        ],
        out_specs=[pl.BlockSpec((tm, qk), row), pl.BlockSpec((tm, qk), row),
                   pl.BlockSpec((tm, vw), row), pl.BlockSpec((tm, vw), row),
                   pl.BlockSpec((tm, qk), row)],
        compiler_params=_cparams(("parallel",), vmem + (8 << 20)),
        name="gla_in",
    )(x, g.reshape(1, d), wq, wk, wv, wr, wgl, wg2, bg.reshape(1, qk))


def _gla_body(q_ref, k_ref, v_ref, r_ref, la_ref, s0_ref, gout_ref, tri_ref,
              og_ref, sfin_ref, st_ref, *, chunk, n_chunks, heads, dk, dv):
    t = pl.program_id(1)

    @pl.when(t == 0)
    def _():
        st_ref[...] = s0_ref[...]

    scale = dk ** -0.5
    row = lax.broadcasted_iota(jnp.int32, (chunk, chunk), 0)
    col = lax.broadcasted_iota(jnp.int32, (chunk, chunk), 1)
    causal = col <= row
    tri = tri_ref[...]
    gout = gout_ref[...]

    for c in range(n_chunks):
        sl = slice(c * chunk, (c + 1) * chunk)
        la_hi, la_lo = _split2(la_ref[sl, :])
        b = _dot(tri, la_hi) + _dot(tri, la_lo)
        b_last = b[chunk - 1:chunk, :]
        q = q_ref[sl, :].astype(F32)
        k = k_ref[sl, :].astype(F32)
        qe = (q * scale * jnp.exp(b)).astype(BF16)
        ke = (k * jnp.exp(-b)).astype(BF16)
        kd = (k * jnp.exp(b_last - b)).astype(BF16)
        dec = jnp.exp(b_last)
        for h in range(heads):
            ks = slice(h * dk, (h + 1) * dk)
            vs = slice(h * dv, (h + 1) * dv)
            vh = v_ref[sl, vs]
            att = lax.dot_general(qe[:, ks], ke[:, ks], NT_DIMS, preferred_element_type=F32)
            att = jnp.where(causal, att, 0.0).astype(BF16)
            st = st_ref[h]
            o = _dot(att, vh) + lax.dot_general(qe[:, ks], st.astype(BF16), NT_DIMS,
                                                preferred_element_type=F32)
            st_ref[h] = st * dec[:, ks] + lax.dot_general(vh, kd[:, ks], TN_DIMS,
                                                          preferred_element_type=F32)
            on = _rms(o, gout)
            rh = r_ref[sl, vs].astype(F32)
            og_ref[sl, vs] = (on * (rh * jax.nn.sigmoid(rh))).astype(BF16)

    @pl.when(t == pl.num_programs(1) - 1)
    def _():
        sfin_ref[...] = st_ref[...]


def _gla(q, k, v, r, la, s0t, gout, *, chunk, tb):
    bsz, t, qk = q.shape
    vw = v.shape[2]
    heads = s0t.shape[1]
    dk, dv = qk // heads, vw // heads
    tb = min(tb, t)
    assert t % tb == 0 and tb % chunk == 0
    tri = jnp.tril(jnp.ones((chunk, chunk), F32)).astype(BF16)
    blk = lambda w: pl.BlockSpec((None, tb, w), lambda b, i: (b, i, 0))
    st_spec = pl.BlockSpec((None, heads, dv, dk), lambda b, i: (b, 0, 0, 0))
    vmem = 2 * tb * (3 * qk + 2 * vw) * 4 + 2 * tb * vw * 2 + 5 * heads * dv * dk * 4
    return pl.pallas_call(
        functools.partial(_gla_body, chunk=chunk, n_chunks=tb // chunk, heads=heads, dk=dk, dv=dv),
        out_shape=[jax.ShapeDtypeStruct((bsz, t, vw), BF16),
                   jax.ShapeDtypeStruct((bsz, heads, dv, dk), F32)],
        grid=(bsz, t // tb),
        in_specs=[blk(qk), blk(qk), blk(vw), blk(vw), blk(qk), st_spec,
                  pl.BlockSpec((1, dv), lambda b, i: (0, 0)),
                  pl.BlockSpec((chunk, chunk), lambda b, i: (0, 0))],
        out_specs=[blk(vw), st_spec],
        scratch_shapes=[pltpu.VMEM((heads, dv, dk), F32)],
        compiler_params=_cparams(("parallel", "arbitrary"), vmem + (8 << 20)),
        name="gla_chunks",
    )(q, k, v, r, la, s0t, gout.reshape(1, dv), tri)


def _store_pairs(ref, x):
    for p in range(ref.shape[0]):
        ref[p] = x[:, p * LANES:(p + 1) * LANES]


def _kv_body(x_ref, g_ref, wk_ref, wv_ref, wf_ref, bf_ref, gk_ref, gsum_ref, gexp_ref,
             k_ref, v_ref, lf_ref, k16_ref, v16_ref, *, vt_block):
    xn = _rms(x_ref[...], g_ref[...]).astype(BF16)
    k = _head_norm(_dot(xn, wk_ref[...]), gsum_ref, gexp_ref, gk_ref[...], FOX_HEAD_DIM)
    k_ref[...] = k.reshape(k_ref.shape)
    _store_pairs(k16_ref, k.astype(BF16))
    lf_ref[...] = _log_sigmoid(_dot(xn, wf_ref[...]) + bf_ref[...])
    if vt_block is None:
        v = _dot(xn, wv_ref[...])
        v_ref[...] = v.reshape(v_ref.shape)
        _store_pairs(v16_ref, v.astype(BF16))
    else:
        vt = lax.dot_general(wv_ref[...], xn, NT_DIMS, preferred_element_type=F32)
        v_ref[...] = vt.T.reshape(v_ref.shape)
        vt16 = vt.astype(BF16)
        for c in range(v16_ref.shape[0]):
            v16_ref[c] = vt16[:, c * vt_block:(c + 1) * vt_block]


def _shared_kv(x, g, wk, wv, wf, bf, gk, gsum, gexp, *, tm, vt_block=None):
    n, d = x.shape
    tm = min(tm, n)
    assert n % tm == 0
    row = lambda i: (i, 0)
    fix = lambda i: (0, 0)
    heads_shape = (n, FOX_HEADS, FOX_HEAD_DIM)
    heads_spec = pl.BlockSpec((tm, FOX_HEADS, FOX_HEAD_DIM), lambda i: (i, 0, 0))
    pair_shape = (d // LANES, n, LANES)
    pair_spec = pl.BlockSpec((d // LANES, tm, LANES), lambda i: (0, i, 0))
    if vt_block is None:
        v16_shape, v16_spec = pair_shape, pair_spec
    else:
        assert tm % vt_block == 0
        v16_shape = (n // vt_block, d, vt_block)
        v16_spec = pl.BlockSpec((tm // vt_block, d, vt_block), lambda i: (i, 0, 0))
    vmem = 2 * tm * d * (4 + 8 + 8 + 2 + 2) + 2 * tm * LANES * 4 + 2 * 2 * (2 * d * d + 3 * d * LANES) \
        + 8 * tm * d * 4
    return pl.pallas_call(
        functools.partial(_kv_body, vt_block=vt_block),
        out_shape=[jax.ShapeDtypeStruct(heads_shape, F32), jax.ShapeDtypeStruct(heads_shape, F32),
                   jax.ShapeDtypeStruct((n, LANES), F32),
                   jax.ShapeDtypeStruct(pair_shape, BF16), jax.ShapeDtypeStruct(v16_shape, BF16)],
        grid=(n // tm,),
        in_specs=[
            pl.BlockSpec((tm, d), row), pl.BlockSpec((1, d), fix),
            pl.BlockSpec((d, d), fix), pl.BlockSpec((d, d), fix), pl.BlockSpec((d, LANES), fix),
            pl.BlockSpec((1, LANES), fix), pl.BlockSpec((1, d), fix),
            pl.BlockSpec((d, LANES), fix), pl.BlockSpec((LANES, d), fix),
        ],
        out_specs=[heads_spec, heads_spec,
                   pl.BlockSpec((tm, LANES), row),
                   pair_spec, v16_spec],
        compiler_params=_cparams(("parallel",), vmem),
        name="shared_kv",
    )(x, g.reshape(1, d), wk, wv, wf, bf, gk, gsum, gexp)


def _fox_q_body(x_ref, g_ref, wq_ref, wg_ref, gq_ref, gsum_ref, gexp_ref, q16_ref, sg_ref):
    xn = _rms(x_ref[...], g_ref[...]).astype(BF16)
    q = _head_norm(_dot(xn, wq_ref[...]), gsum_ref, gexp_ref, gq_ref[...], FOX_HEAD_DIM)
    q16_ref[...] = (q * (FOX_HEAD_DIM ** -0.5 * LOG2E)).astype(BF16)
    sg_ref[...] = jax.nn.sigmoid(_dot(xn, wg_ref[...])).astype(sg_ref.dtype)


def _fox_q(x, g, wq, wg, gq, gsum, gexp, *, tm):
    n, d = x.shape
    tm = min(tm, n)
    assert n % tm == 0
    row = lambda i: (i, 0)
    fix = lambda i: (0, 0)
    vmem = 2 * tm * d * (4 + 2 + 4) + 2 * 2 * (2 * d * d + 2 * d * LANES) + 6 * tm * d * 4
    return pl.pallas_call(
        _fox_q_body,
        out_shape=[jax.ShapeDtypeStruct((n, d), BF16), jax.ShapeDtypeStruct((n, d), BF16)],
        grid=(n // tm,),
        in_specs=[
            pl.BlockSpec((tm, d), row), pl.BlockSpec((1, d), fix),
            pl.BlockSpec((d, d), fix), pl.BlockSpec((d, d), fix), pl.BlockSpec((1, d), fix),
            pl.BlockSpec((d, LANES), fix), pl.BlockSpec((LANES, d), fix),
        ],
        out_specs=[pl.BlockSpec((tm, d), row), pl.BlockSpec((tm, d), row)],
        compiler_params=_cparams(("parallel",), vmem),
        name="fox_q",
    )(x, g.reshape(1, d), wq, wg, gq, gsum, gexp)


def _pack_parts(x):
    p1, p2, p3 = (p.astype(F32) for p in _split3(x))
    packed = p1 + pltpu.roll(p2, FOX_HEADS, 1) + pltpu.roll(p3, 2 * FOX_HEADS, 1)
    return packed.astype(BF16)


def _bias_body(lf_ref, tri_ref, sel_ref, one_ref, qb_ref, kb_ref, carry_ref):
    @pl.when(pl.program_id(1) == 0)
    def _():
        carry_ref[...] = jnp.zeros_like(carry_ref)

    lf = lf_ref[...]
    tb = lf.shape[0]
    head_lanes = lax.broadcasted_iota(jnp.int32, lf.shape, 1) < FOX_HEADS
    cp = _dot(tri_ref[...], _pack_parts(jnp.where(head_lanes, lf, 0.0)))
    c = cp + pltpu.roll(cp, LANES - FOX_HEADS, 1) + pltpu.roll(cp, LANES - 2 * FOX_HEADS, 1)
    c = jnp.where(head_lanes, c, 0.0) + carry_ref[...]
    carry_ref[...] = c[tb - 1:tb, :]
    qkb = _dot(_pack_parts(c * LOG2E), sel_ref[...]) + one_ref[...]
    wide = FOX_PAIRS * LANES
    for p in range(FOX_PAIRS):
        qb_ref[p] = qkb[:, p * LANES:(p + 1) * LANES].astype(BF16)
        kb_ref[p] = qkb[:, wide + p * LANES:wide + (p + 1) * LANES].astype(BF16)


def _bias_constants():
    wide = FOX_PAIRS * LANES
    sel = np.zeros((LANES, 2 * wide), np.float32)
    one = np.zeros((1, 2 * wide), np.float32)
    for p in range(FOX_PAIRS):
        for e in range(2):
            base = p * LANES + e * BIAS_LANES_PER_HEAD
            for part in range(BIAS_PARTS):
                row = part * FOX_HEADS + 2 * p + e
                sel[row, base + part] = 1.0
                one[0, base + BIAS_PARTS + part] = 1.0
                one[0, wide + base + part] = 1.0
                sel[row, wide + base + BIAS_PARTS + part] = -1.0
    return jnp.asarray(sel, BF16), jnp.asarray(one)


def _fox_bias(lf, *, tb):
    bsz, kp, _ = lf.shape
    assert kp % tb == 0
    sel, one = _bias_constants()
    tri = jnp.tril(jnp.ones((tb, tb), F32)).astype(BF16)
    wide = FOX_PAIRS * LANES
    out_spec = pl.BlockSpec((None, FOX_PAIRS, tb, LANES), lambda b, i: (b, 0, i, 0))
    vmem = 2 * tb * LANES * 4 + 2 * tb * tb * 2 + 2 * LANES * 2 * wide * 2 \
        + 4 * FOX_PAIRS * tb * LANES * 2 + 4 * tb * 2 * wide * 4
    return pl.pallas_call(
        _bias_body,
        out_shape=[jax.ShapeDtypeStruct((bsz, FOX_PAIRS, kp, LANES), BF16)] * 2,
        grid=(bsz, kp // tb),
        in_specs=[
            pl.BlockSpec((None, tb, LANES), lambda b, i: (b, i, 0)),
            pl.BlockSpec((tb, tb), lambda b, i: (0, 0)),
            pl.BlockSpec((LANES, 2 * wide), lambda b, i: (0, 0)),
            pl.BlockSpec((1, 2 * wide), lambda b, i: (0, 0)),
        ],
        out_specs=[out_spec, out_spec],
        scratch_shapes=[pltpu.VMEM((1, LANES), F32)],
        compiler_params=_cparams(("parallel", "arbitrary"), vmem),
        name="fox_bias",
    )(lf, tri, sel, one)


def _q_aug(q, qb, lane):
    zero = jnp.zeros_like(q)
    out = []
    for e in range(2):
        head_lanes = (lane >> 6) == e
        bias_lanes = jnp.logical_and(lane >= e * BIAS_LANES_PER_HEAD,
                                     lane < (e + 1) * BIAS_LANES_PER_HEAD)
        out.append(jnp.concatenate(
            [jnp.where(head_lanes, q, zero), jnp.where(bias_lanes, qb, zero)], axis=1))
    return out


def _attn_body(q_ref, qb_ref, k_ref, kb_ref, v_ref, sg_ref, o_ref, m_ref, l_ref, acc_ref,
               *, tq, tk, n_qblocks):
    i = pl.program_id(2)
    lane = lax.broadcasted_iota(jnp.int32, (tq, LANES), 1)
    q_aug = _q_aug(q_ref[...], qb_ref[...], lane)

    m_ref[...] = jnp.full(m_ref.shape, NEG_INF, F32)
    l_ref[...] = jnp.zeros(l_ref.shape, F32)
    acc_ref[...] = jnp.zeros(acc_ref.shape, F32)

    def qk_scores(j, qwin=slice(None)):
        rows = slice(j * tk, (j + 1) * tk)
        k_aug = jnp.concatenate([k_ref[rows, :], kb_ref[rows, :]], axis=1)
        qs = q_aug if qwin == slice(None) else [qa[qwin] for qa in q_aug]
        return [lax.dot_general(k_aug, qs[e], NT_DIMS, preferred_element_type=F32)
                for e in range(2)]

    def softmax_pv(j, scores, qwin=slice(None)):
        v = v_ref[j]
        if qwin == slice(None):
            at = lambda ref, e: ref.at[e]
        else:
            at = lambda ref, e: ref.at[e, :, qwin]
        for e in range(2):
            s = scores[e]
            m_prev = at(m_ref, e)[...]
            m_new = jnp.maximum(m_prev, jnp.max(s, axis=0, keepdims=True))
            alpha = jnp.exp2(m_prev - m_new)
            p = jnp.exp2(s - m_new)
            at(l_ref, e)[...] = alpha * at(l_ref, e)[...] + jnp.sum(p, axis=0, keepdims=True)
            p16 = p.astype(BF16)
            at(acc_ref, e)[...] = alpha * at(acc_ref, e)[...] + _dot(v, p16)
            at(m_ref, e)[...] = m_new

    early, late = slice(0, tk), slice(tk, tq)

    def schedule(n_visible):
        ahead = qk_scores(0)
        for j in range(n_visible):
            cur, ahead = ahead, qk_scores(j + 1)
            softmax_pv(j, cur)
        tmask = jnp.where(lax.broadcasted_iota(jnp.int32, (tk, tk), 0)
                          <= lax.broadcasted_iota(jnp.int32, (tk, tk), 1), 0.0, NEG_INF)
        late_scores = qk_scores(n_visible + 1, late)
        softmax_pv(n_visible, [jnp.concatenate([s[:, early] + tmask, s[:, late]], axis=1)
                               for s in ahead])
        softmax_pv(n_visible + 1, [s + tmask for s in late_scores], late)

    lax.switch(i, [functools.partial(schedule, 2 * n) for n in range(n_qblocks)])

    o0 = acc_ref[0] * (1.0 / l_ref[0])
    o1 = acc_ref[1] * (1.0 / l_ref[1])
    feature = lax.broadcasted_iota(jnp.int32, (LANES, tq), 0)
    o = jnp.where((feature >> 6) == 0, o0, o1).T
    o_ref[...] = (o * sg_ref[...].astype(F32)).astype(BF16)


def _fox_attn(q16, qb, k16, kb, vt16, sg, *, tq, tk):
    bsz, t, d = q16.shape
    assert t % tq == 0 and tq == 2 * tk and k16.shape[2] == t
    qspec = pl.BlockSpec((None, tq, LANES), lambda b, p, i: (b, i, p))
    vmem = 2 * (3 * t * LANES * 2 + tq * LANES * (2 + 2 + 2 + 2)) + 3 * tq * LANES * 4 \
        + 2 * tk * 2 * LANES * 2 + 10 * tq * tk * 4
    return pl.pallas_call(
        functools.partial(_attn_body, tq=tq, tk=tk, n_qblocks=t // tq),
        out_shape=jax.ShapeDtypeStruct((bsz, t, d), BF16),
        grid=(bsz, FOX_PAIRS, t // tq),
        in_specs=[
            qspec,
            pl.BlockSpec((None, None, tq, LANES), lambda b, p, i: (b, p, i, 0)),
            pl.BlockSpec((None, None, t, LANES), lambda b, p, i: (p, b, 0, 0)),
            pl.BlockSpec((None, None, t, LANES), lambda b, p, i: (b, p, 0, 0)),
            pl.BlockSpec((t // tk, LANES, tk), lambda b, p, i: (b, p, 0)),
            qspec,
        ],
        out_specs=qspec,
        scratch_shapes=[pltpu.VMEM((2, 1, tq), F32), pltpu.VMEM((2, 1, tq), F32),
                        pltpu.VMEM((2, LANES, tq), F32)],
        compiler_params=_cparams(("parallel", "parallel", "arbitrary"), vmem + (8 << 20)),
        name="fox_attn",
    )(q16, qb, k16, kb, vt16, sg)


def _cache_attn_body(q_ref, qb_ref, ck_ref, cv_ref, kn_ref, vn_ref, kb_ref, sg_ref,
                     o_ref, m_ref, l_ref, acc_ref, *, tk):
    t = q_ref.shape[0]
    p_len = ck_ref.shape[0]
    lane = lax.broadcasted_iota(jnp.int32, (t, LANES), 1)
    q_aug = _q_aug(q_ref[...], qb_ref[...], lane)
    m_ref[...] = jnp.full(m_ref.shape, NEG_INF, F32)
    l_ref[...] = jnp.zeros(l_ref.shape, F32)
    acc_ref[...] = jnp.zeros(acc_ref.shape, F32)

    def fold(k, kb, v, mask):
        k_aug = jnp.concatenate([k, kb], axis=1)
        scores = [lax.dot_general(q_aug[e], k_aug, NT_DIMS, preferred_element_type=F32)
                  for e in range(2)]
        for e in range(2):
            s = scores[e] if mask is None else scores[e] + mask
            m_prev = m_ref[e]
            m_new = jnp.maximum(m_prev, jnp.max(s, axis=1, keepdims=True))
            alpha = jnp.exp2(m_prev - m_new)
            p = jnp.exp2(s - m_new)
            l_ref[e] = alpha * l_ref[e] + jnp.sum(p, axis=1, keepdims=True)
            acc_ref[e] = alpha * acc_ref[e] + _dot(p.astype(BF16), v)
            m_ref[e] = m_new

    for j in range(p_len // tk):
        rows = slice(j * tk, (j + 1) * tk)
        fold(ck_ref[rows, :], kb_ref[rows, :], cv_ref[rows, :], None)
    causal = jnp.where(lax.broadcasted_iota(jnp.int32, (t, t), 1)
                       <= lax.broadcasted_iota(jnp.int32, (t, t), 0), 0.0, NEG_INF)
    fold(kn_ref[...], kb_ref[p_len:p_len + t, :], vn_ref[...], causal)

    o = jnp.where((lane >> 6) == 0, acc_ref[0] * (1.0 / l_ref[0]), acc_ref[1] * (1.0 / l_ref[1]))
    o_ref[...] = (o * sg_ref[...].astype(F32)).astype(BF16)


def _fox_attn_cached(q16, qb, kb, ck16, cv16, k16, v16, sg, *, tk):
    bsz, t, d = q16.shape
    p_len = ck16.shape[2]
    assert p_len % tk == 0 and p_len % t == 0 and kb.shape[2] == p_len + t
    qspec = pl.BlockSpec((None, t, LANES), lambda b, p: (b, 0, p))
    cspec = pl.BlockSpec((None, None, p_len, LANES), lambda b, p: (p, b, 0, 0))
    nspec = pl.BlockSpec((None, t, LANES), lambda b, p: (p, b, 0))
    vmem = 2 * (3 * (p_len + t) * LANES * 2 + t * LANES * 8) + 6 * t * LANES * 4 + 8 * t * tk * 4 \
        + 4 * tk * LANES * 2
    return pl.pallas_call(
        functools.partial(_cache_attn_body, tk=tk),
        out_shape=jax.ShapeDtypeStruct((bsz, t, d), BF16),
        grid=(bsz, FOX_PAIRS),
        in_specs=[
            qspec,
            pl.BlockSpec((None, None, t, LANES), lambda b, p: (b, p, p_len // t, 0)),
            cspec, cspec, nspec, nspec,
            pl.BlockSpec((None, None, p_len + t, LANES), lambda b, p: (b, p, 0, 0)),
            qspec,
        ],
        out_specs=qspec,
        scratch_shapes=[pltpu.VMEM((2, t, 1), F32), pltpu.VMEM((2, t, 1), F32),
                        pltpu.VMEM((2, t, LANES), F32)],
        compiler_params=_cparams(("parallel", "parallel"), vmem + (8 << 20)),
        name="fox_attn_cached",
    )(q16, qb, ck16, cv16, k16, v16, kb, sg)


TM = 1024
KV_TM = 512
FFN_TM = 1024
FFN_TF = 256
REPACK_ROWS = 128
GLA_TB = 256
ATTN_TQ = 1024
ATTN_TK = 512
CACHE_TK = 4096
BIAS_TB_CHOICES = (832, 512, 384, 320, 256, 192, 128, 64)


def _round_up(x, m):
    return (x + m - 1) // m * m


def _prep_weights(ffn_norm, w_ffn_gu, w_ffn_down, mix_norm, a_w_in, a_w_g2, a_b_g, a_g_out, a_w_o,
                  kv_norm, w_kvf, b_f, g_k, b_w_qg, b_g_q, b_w_o):
    d = w_ffn_gu.shape[2]
    rank = a_w_g2.shape[1]
    qk = a_w_g2.shape[2]
    vw = a_w_o.shape[1]
    w_in = a_w_in[0]
    pad_cols = lambda w: jnp.pad(w, ((0, 0), (0, LANES - w.shape[1])))
    groups = np.arange(d) // FOX_HEAD_DIM
    gsum = (groups[:, None] == np.arange(LANES)[None, :]).astype(np.float32)
    return dict(
        ffn_norm=ffn_norm, mix_norm=mix_norm, kv_norm=kv_norm,
        w_gu=_tile_cols(w_ffn_gu.reshape((-1,) + w_ffn_gu.shape[2:]), FFN_TF, rows=REPACK_ROWS),
        w_down=w_ffn_down.astype(BF16),
        a_wq=w_in[:, :qk].astype(BF16), a_wk=w_in[:, qk:2 * qk].astype(BF16),
        a_wv=w_in[:, 2 * qk:2 * qk + vw].astype(BF16),
        a_wr=w_in[:, 2 * qk + vw:2 * qk + 2 * vw].astype(BF16),
        a_wgl=pad_cols(w_in[:, 2 * qk + 2 * vw:]).astype(BF16),
        a_wg2=jnp.pad(a_w_g2[0], ((0, LANES - rank), (0, 0))).astype(BF16),
        a_bg=a_b_g[0], a_gout=a_g_out[0], a_wo=a_w_o[0].astype(BF16),
        kv_wk=w_kvf[:, :d].astype(BF16), kv_wv=w_kvf[:, d:2 * d].astype(BF16),
        kv_wv_t=w_kvf[:, d:2 * d].T.astype(BF16),
        kv_wf=pad_cols(w_kvf[:, 2 * d:]).astype(BF16),
        kv_bf=jnp.pad(b_f, (0, LANES - b_f.shape[0])).reshape(1, LANES),
        gk=jnp.tile(g_k, FOX_HEADS).reshape(1, d), gq=jnp.tile(b_g_q[0], FOX_HEADS).reshape(1, d),
        b_wq=b_w_qg[0][:, :d].astype(BF16), b_wg=b_w_qg[0][:, d:].astype(BF16),
        b_wo=b_w_o[0].astype(BF16),
        gsum=jnp.asarray(gsum, BF16), gexp=jnp.asarray(gsum.T, BF16),
    )


def _trunk(x, s0, past, w):
    bsz, t, d = x.shape
    n = bsz * t
    h = x.reshape(n, d)
    ffn = lambda h_, layer, half: _ffn(h_, w["ffn_norm"][layer, half], w["w_gu"][2 * layer + half],
                                       w["w_down"][layer, half], tm=FFN_TM)
    h = ffn(h, 0, 0)
    q, k, v, r, la = _gla_in(h, w["mix_norm"][0], w["a_wq"], w["a_wk"], w["a_wv"], w["a_wr"],
                             w["a_wgl"], w["a_wg2"], w["a_bg"], tm=TM)
    s3 = lambda z: z.reshape(bsz, t, z.shape[1])
    chunk = 64
    og, st_fin = _gla(s3(q), s3(k), s3(v), s3(r), s3(la), jnp.swapaxes(s0, 2, 3), w["a_gout"],
                      chunk=chunk, tb=GLA_TB)
    h = _proj_res(h, og.reshape(n, -1), w["a_wo"], tm=TM)
    h = ffn(h, 0, 1)
    tq = min(ATTN_TQ, t)
    tk = min(ATTN_TK, tq // 2)
    k_new, v_new, lf, k16, v16 = _shared_kv(
        h, w["kv_norm"], w["kv_wk"], w["kv_wv_t"] if past is None else w["kv_wv"], w["kv_wf"],
        w["kv_bf"], w["gk"], w["gsum"], w["gexp"], tm=KV_TM, vt_block=tk if past is None else None)
    h = ffn(h, 1, 0)
    q16, sg = _fox_q(h, w["mix_norm"][1], w["b_wq"], w["b_wg"], w["gq"], w["gsum"], w["gexp"], tm=TM)
    pairs4 = lambda z: z.reshape(FOX_PAIRS, bsz, t, LANES)
    if past is None:
        lf_all = s3(lf)
    else:
        past_k, past_v, past_lf = past
        p_len = past_k.shape[1]
        past_pairs = lambda a: jnp.transpose(
            a.astype(BF16).reshape(bsz, p_len, FOX_PAIRS, LANES), (2, 0, 1, 3))
        lf_all = jnp.concatenate(
            [jnp.pad(past_lf, ((0, 0), (0, 0), (0, LANES - past_lf.shape[2]))), s3(lf)], axis=1)
    bias_tb = next(c for c in BIAS_TB_CHOICES if lf_all.shape[1] % c == 0)
    qb, kb = _fox_bias(lf_all, tb=bias_tb)
    if past is None:
        og = _fox_attn(s3(q16), qb, pairs4(k16), kb, v16, s3(sg), tq=tq, tk=tk)
    else:
        og = _fox_attn_cached(s3(q16), qb, kb, past_pairs(past_k), past_pairs(past_v), k16, v16,
                              s3(sg), tk=min(CACHE_TK, p_len))
    h = _proj_res(h, og.reshape(n, d), w["b_wo"], tm=TM)
    h = ffn(h, 1, 1)
    heads4 = lambda z: z.reshape(bsz, t, FOX_HEADS, FOX_HEAD_DIM)
    return (h.reshape(bsz, t, d), jnp.swapaxes(st_fin, 2, 3)[:, None], heads4(k_new), heads4(v_new),
            lf[:, :FOX_HEADS].reshape(bsz, t, FOX_HEADS))


def kernel(x_prompt, x_sample, state_gla, cache_k, cache_v, cache_logf, ffn_norm, w_ffn_gu, w_ffn_down, mix_norm, a_w_in, a_w_g2, a_b_g, a_g_out, a_w_o, kv_norm, w_kvf, b_f, g_k, b_w_qg, b_g_q, b_w_o):
    w = _prep_weights(ffn_norm, w_ffn_gu, w_ffn_down, mix_norm, a_w_in, a_w_g2, a_b_g, a_g_out,
                      a_w_o, kv_norm, w_kvf, b_f, g_k, b_w_qg, b_g_q, b_w_o)
    s0_prompt = jnp.zeros((x_prompt.shape[0],) + state_gla.shape[2:], F32)
    y_p, gla_p, k_p, v_p, lf_p = _trunk(x_prompt, s0_prompt, None, w)
    y_s, gla_s, k_s, v_s, lf_s = _trunk(x_sample, state_gla[:, 0], (cache_k, cache_v, cache_logf), w)
    return (y_p, y_s, gla_p, gla_s, k_p, v_p, lf_p, k_s, v_s, lf_s)
```

```python
import functools

import jax
import jax.numpy as jnp
import numpy as np
from jax import lax
from jax.experimental import pallas as pl
from jax.experimental.pallas import tpu as pltpu

F32 = jnp.float32
BF16 = jnp.bfloat16

EPS = 1e-6
NEG_INF = -1e30
LOG2E = 1.4426950408889634

LANES = 128
V7X_VMEM_BYTES = 64 * 1024 * 1024
V7X_SCOPED_VMEM_CAP = 60000 * 1024

GLA_HEADS = 4
GLA_GATE_TAU = 16.0
FOX_HEADS = 16
FOX_HEAD_DIM = 64
FOX_PAIRS = FOX_HEADS // 2
BIAS_PARTS = 3
BIAS_LANES_PER_HEAD = 2 * BIAS_PARTS

NT_DIMS = (((1,), (1,)), ((), ()))
TN_DIMS = (((0,), (0,)), ((), ()))


def _cparams(semantics, vmem_bytes):
    limit = int(min(max(vmem_bytes, 16 * 1024 * 1024), V7X_SCOPED_VMEM_CAP))
    return pltpu.CompilerParams(dimension_semantics=semantics, vmem_limit_bytes=limit)


def _dot(a, b):
    return jnp.dot(a, b, preferred_element_type=F32)


def _rms(x, g):
    ms = jnp.mean(x * x, axis=-1, keepdims=True)
    return x * lax.rsqrt(ms + EPS) * g


def _split2(x):
    hi = x.astype(BF16)
    lo = (x - hi.astype(F32)).astype(BF16)
    return hi, lo


def _split3(x):
    p1 = x.astype(BF16)
    r1 = x - p1.astype(F32)
    p2 = r1.astype(BF16)
    p3 = (r1 - p2.astype(F32)).astype(BF16)
    return p1, p2, p3


def _log_sigmoid(x):
    return jnp.minimum(x, 0.0) - jnp.log(1.0 + jnp.exp(-jnp.abs(x)))


def _head_norm(x, gsum_ref, gexp_ref, gain, head_dim):
    ssq = _dot((x * x).astype(BF16), gsum_ref[...])
    inv = lax.rsqrt(ssq * (1.0 / head_dim) + EPS)
    ihi, ilo = _split2(inv)
    inv_full = _dot(ihi, gexp_ref[...]) + _dot(ilo, gexp_ref[...])
    return x * inv_full * gain


def _ffn_body(x_ref, g_ref, wg_ref, wu_ref, wd_ref, *rest, mixer_out):
    if mixer_out:
        a_ref, wo_ref, o_ref, xn_ref = rest
    else:
        o_ref, xn_ref = rest
    j = pl.program_id(1)

    @pl.when(j == 0)
    def _():
        x = x_ref[...]
        if mixer_out:
            x = x + _dot(a_ref[...], wo_ref[...])
        xn_ref[...] = _rms(x, g_ref[...]).astype(BF16)
        o_ref[...] = x

    xn = xn_ref[...]
    g = _dot(xn, wg_ref[...])
    u = _dot(xn, wu_ref[...])
    h = (0.5 * (g * jax.nn.sigmoid(g) * u)).astype(BF16)
    o_ref[...] += _dot(h, wd_ref[...])


def _tile_cols_body(w_ref, o_ref):
    tf = o_ref.shape[2]
    for c in range(o_ref.shape[0]):
        o_ref[c] = w_ref[:, c * tf:(c + 1) * tf].astype(o_ref.dtype)


def _tile_cols(w, tf, *, rows):
    g, d, n = w.shape
    assert n % tf == 0 and d % rows == 0
    return pl.pallas_call(
        _tile_cols_body,
        out_shape=jax.ShapeDtypeStruct((g, n // tf, d, tf), BF16),
        grid=(g, d // rows),
        in_specs=[pl.BlockSpec((None, rows, n), lambda a, r: (a, r, 0))],
        out_specs=pl.BlockSpec((None, n // tf, rows, tf), lambda a, r: (a, 0, r, 0)),
        compiler_params=_cparams(("parallel", "parallel"), 2 * rows * n * (4 + 2) + (4 << 20)),
        name="tile_cols",
    )(w)


def _ffn(x, g, w_gu, w_down, *, tm, mixer_out=None):
    n, d = x.shape
    d_ff = w_down.shape[0]
    tf = w_gu.shape[2]
    tm = min(tm, n)
    nf = d_ff // tf
    assert n % tm == 0 and w_gu.shape[0] == 2 * nf
    vmem = 4 * tm * d * 4 + tm * d * 2 + 2 * 3 * d * tf * 2 + 3 * tm * tf * 4 + tm * d * 4
    extra, extra_specs = (), []
    if mixer_out is not None:
        a, wo = mixer_out
        extra = (a, wo)
        extra_specs = [pl.BlockSpec((tm, a.shape[1]), lambda i, j: (i, 0)),
                       pl.BlockSpec(wo.shape, lambda i, j: (0, 0))]
        vmem += 2 * (tm * a.shape[1] + wo.size) * 2 + tm * d * 4
    return pl.pallas_call(
        functools.partial(_ffn_body, mixer_out=mixer_out is not None),
        out_shape=jax.ShapeDtypeStruct((n, d), F32),
        grid=(n // tm, nf),
        in_specs=[
            pl.BlockSpec((tm, d), lambda i, j: (i, 0)),
            pl.BlockSpec((1, d), lambda i, j: (0, 0)),
            pl.BlockSpec((None, d, tf), lambda i, j: (j, 0, 0)),
            pl.BlockSpec((None, d, tf), lambda i, j: (j + nf, 0, 0)),
            pl.BlockSpec((tf, d), lambda i, j: (j, 0)),
        ] + extra_specs,
        out_specs=pl.BlockSpec((tm, d), lambda i, j: (i, 0)),
        scratch_shapes=[pltpu.VMEM((tm, d), BF16)],
        compiler_params=_cparams(("parallel", "arbitrary"), vmem + (8 << 20)),
        name="ffn",
    )(x, g.reshape(1, d), w_gu, w_gu, w_down, *extra)


def _gla_in_body(x_ref, g_ref, wq_ref, wk_ref, wv_ref, wr_ref, wgl_ref, wg2_ref, bg_ref,
                 q_ref, k_ref, v_ref, r_ref, la_ref):
    xn = _rms(x_ref[...], g_ref[...]).astype(BF16)
    q_ref[...] = _dot(xn, wq_ref[...]).astype(q_ref.dtype)
    k_ref[...] = _dot(xn, wk_ref[...]).astype(k_ref.dtype)
    v_ref[...] = _dot(xn, wv_ref[...]).astype(v_ref.dtype)
    r_ref[...] = _dot(xn, wr_ref[...]).astype(r_ref.dtype)
    gl = _dot(xn, wgl_ref[...]).astype(BF16)
    z = _dot(gl, wg2_ref[...]) + bg_ref[...]
    la_ref[...] = _log_sigmoid(z) * (1.0 / GLA_GATE_TAU)


def _gla_in(x, g, wq, wk, wv, wr, wgl, wg2, bg, *, tm):
    n, d = x.shape
    qk, vw = wq.shape[1], wv.shape[1]
    tm = min(tm, n)
    assert n % tm == 0
    row = lambda i: (i, 0)
    fix = lambda i: (0, 0)
    w_bytes = 2 * (2 * d * qk + 2 * d * vw + d * LANES + LANES * qk)
    vmem = 2 * tm * (d + 3 * qk + 2 * vw) * 4 + 2 * w_bytes + tm * d * 2
    return pl.pallas_call(
        _gla_in_body,
        out_shape=[jax.ShapeDtypeStruct((n, qk), BF16), jax.ShapeDtypeStruct((n, qk), BF16),
                   jax.ShapeDtypeStruct((n, vw), BF16), jax.ShapeDtypeStruct((n, vw), BF16),
                   jax.ShapeDtypeStruct((n, qk), F32)],
        grid=(n // tm,),
        in_specs=[
            pl.BlockSpec((tm, d), row), pl.BlockSpec((1, d), fix),
            pl.BlockSpec((d, qk), fix), pl.BlockSpec((d, qk), fix),
            pl.BlockSpec((d, vw), fix), pl.BlockSpec((d, vw), fix),
            pl.BlockSpec((d, LANES), fix), pl.BlockSpec((LANES, qk), fix), pl.BlockSpec((1, qk), fix),
        ],
        out_specs=[pl.BlockSpec((tm, qk), row), pl.BlockSpec((tm, qk), row),
                   pl.BlockSpec((tm, vw), row), pl.BlockSpec((tm, vw), row),
                   pl.BlockSpec((tm, qk), row)],
        compiler_params=_cparams(("parallel",), vmem + (8 << 20)),
        name="gla_in",
    )(x, g.reshape(1, d), wq, wk, wv, wr, wgl, wg2, bg.reshape(1, qk))


def _gla_body(q_ref, k_ref, v_ref, r_ref, la_ref, s0_ref, gout_ref, tri_ref,
              og_ref, sfin_ref, st_ref, *, chunk, n_chunks, heads, dk, dv):
    t = pl.program_id(1)

    @pl.when(t == 0)
    def _():
        st_ref[...] = s0_ref[...]

    scale = dk ** -0.5
    row = lax.broadcasted_iota(jnp.int32, (chunk, chunk), 0)
    col = lax.broadcasted_iota(jnp.int32, (chunk, chunk), 1)
    causal = col <= row
    tri = tri_ref[...]
    gout = gout_ref[...]

    for c in range(n_chunks):
        sl = slice(c * chunk, (c + 1) * chunk)
        la_hi, la_lo = _split2(la_ref[sl, :])
        b = _dot(tri, la_hi) + _dot(tri, la_lo)
        b_last = b[chunk - 1:chunk, :]
        q = q_ref[sl, :].astype(F32)
        k = k_ref[sl, :].astype(F32)
        qe = (q * scale * jnp.exp(b)).astype(BF16)
        ke = (k * jnp.exp(-b)).astype(BF16)
        kd = (k * jnp.exp(b_last - b)).astype(BF16)
        dec = jnp.exp(b_last)
        for h in range(heads):
            ks = slice(h * dk, (h + 1) * dk)
            vs = slice(h * dv, (h + 1) * dv)
            vh = v_ref[sl, vs]
            att = lax.dot_general(qe[:, ks], ke[:, ks], NT_DIMS, preferred_element_type=F32)
            att = jnp.where(causal, att, 0.0).astype(BF16)
            st = st_ref[h]
            o = _dot(att, vh) + lax.dot_general(qe[:, ks], st.astype(BF16), NT_DIMS,
                                                preferred_element_type=F32)
            st_ref[h] = st * dec[:, ks] + lax.dot_general(vh, kd[:, ks], TN_DIMS,
                                                          preferred_element_type=F32)
            on = _rms(o, gout)
            rh = r_ref[sl, vs].astype(F32)
            og_ref[sl, vs] = (on * (rh * jax.nn.sigmoid(rh))).astype(BF16)

    @pl.when(t == pl.num_programs(1) - 1)
    def _():
        sfin_ref[...] = st_ref[...]


def _gla(q, k, v, r, la, s0t, gout, *, chunk, tb):
    bsz, t, qk = q.shape
    vw = v.shape[2]
    heads = s0t.shape[1]
    dk, dv = qk // heads, vw // heads
    tb = min(tb, t)
    assert t % tb == 0 and tb % chunk == 0
    tri = jnp.tril(jnp.ones((chunk, chunk), F32)).astype(BF16)
    blk = lambda w: pl.BlockSpec((None, tb, w), lambda b, i: (b, i, 0))
    st_spec = pl.BlockSpec((None, heads, dv, dk), lambda b, i: (b, 0, 0, 0))
    vmem = 2 * tb * (3 * qk + 2 * vw) * 4 + 2 * tb * vw * 2 + 5 * heads * dv * dk * 4
    return pl.pallas_call(
        functools.partial(_gla_body, chunk=chunk, n_chunks=tb // chunk, heads=heads, dk=dk, dv=dv),
        out_shape=[jax.ShapeDtypeStruct((bsz, t, vw), BF16),
                   jax.ShapeDtypeStruct((bsz, heads, dv, dk), F32)],
        grid=(bsz, t // tb),
        in_specs=[blk(qk), blk(qk), blk(vw), blk(vw), blk(qk), st_spec,
                  pl.BlockSpec((1, dv), lambda b, i: (0, 0)),
                  pl.BlockSpec((chunk, chunk), lambda b, i: (0, 0))],
        out_specs=[blk(vw), st_spec],
        scratch_shapes=[pltpu.VMEM((heads, dv, dk), F32)],
        compiler_params=_cparams(("parallel", "arbitrary"), vmem + (8 << 20)),
        name="gla_chunks",
    )(q, k, v, r, la, s0t, gout.reshape(1, dv), tri)


def _store_pairs(ref, x):
    for p in range(ref.shape[0]):
        ref[p] = x[:, p * LANES:(p + 1) * LANES]


def _kv_body(x_ref, g_ref, wk_ref, wv_ref, wf_ref, bf_ref, gk_ref, gsum_ref, gexp_ref,
             k_ref, v_ref, lf_ref, k16_ref, v16_ref, *, vt_block):
    xn = _rms(x_ref[...], g_ref[...]).astype(BF16)
    k = _head_norm(_dot(xn, wk_ref[...]), gsum_ref, gexp_ref, gk_ref[...], FOX_HEAD_DIM)
    k_ref[...] = k.reshape(k_ref.shape)
    _store_pairs(k16_ref, k.astype(BF16))
    lf_ref[...] = _log_sigmoid(_dot(xn, wf_ref[...]) + bf_ref[...])
    if vt_block is None:
        v = _dot(xn, wv_ref[...])
        v_ref[...] = v.reshape(v_ref.shape)
        _store_pairs(v16_ref, v.astype(BF16))
    else:
        vt = lax.dot_general(wv_ref[...], xn, NT_DIMS, preferred_element_type=F32)
        v_ref[...] = vt.T.reshape(v_ref.shape)
        vt16 = vt.astype(BF16)
        for c in range(v16_ref.shape[0]):
            v16_ref[c] = vt16[:, c * vt_block:(c + 1) * vt_block]


def _shared_kv(x, g, wk, wv, wf, bf, gk, gsum, gexp, *, tm, vt_block=None):
    n, d = x.shape
    tm = min(tm, n)
    assert n % tm == 0
    row = lambda i: (i, 0)
    fix = lambda i: (0, 0)
    heads_shape = (n, FOX_HEADS, FOX_HEAD_DIM)
    heads_spec = pl.BlockSpec((tm, FOX_HEADS, FOX_HEAD_DIM), lambda i: (i, 0, 0))
    pair_shape = (d // LANES, n, LANES)
    pair_spec = pl.BlockSpec((d // LANES, tm, LANES), lambda i: (0, i, 0))
    if vt_block is None:
        v16_shape, v16_spec = pair_shape, pair_spec
    else:
        assert tm % vt_block == 0
        v16_shape = (n // vt_block, d, vt_block)
        v16_spec = pl.BlockSpec((tm // vt_block, d, vt_block), lambda i: (i, 0, 0))
    vmem = 2 * tm * d * (4 + 8 + 8 + 2 + 2) + 2 * tm * LANES * 4 + 2 * 2 * (2 * d * d + 3 * d * LANES) \
        + 8 * tm * d * 4
    return pl.pallas_call(
        functools.partial(_kv_body, vt_block=vt_block),
        out_shape=[jax.ShapeDtypeStruct(heads_shape, F32), jax.ShapeDtypeStruct(heads_shape, F32),
                   jax.ShapeDtypeStruct((n, LANES), F32),
                   jax.ShapeDtypeStruct(pair_shape, BF16), jax.ShapeDtypeStruct(v16_shape, BF16)],
        grid=(n // tm,),
        in_specs=[
            pl.BlockSpec((tm, d), row), pl.BlockSpec((1, d), fix),
            pl.BlockSpec((d, d), fix), pl.BlockSpec((d, d), fix), pl.BlockSpec((d, LANES), fix),
            pl.BlockSpec((1, LANES), fix), pl.BlockSpec((1, d), fix),
            pl.BlockSpec((d, LANES), fix), pl.BlockSpec((LANES, d), fix),
        ],
        out_specs=[heads_spec, heads_spec,
                   pl.BlockSpec((tm, LANES), row),
                   pair_spec, v16_spec],
        compiler_params=_cparams(("parallel",), vmem),
        name="shared_kv",
    )(x, g.reshape(1, d), wk, wv, wf, bf, gk, gsum, gexp)


def _fox_q_body(x_ref, g_ref, wq_ref, wg_ref, gq_ref, gsum_ref, gexp_ref, q16_ref, sg_ref):
    xn = _rms(x_ref[...], g_ref[...]).astype(BF16)
    q = _head_norm(_dot(xn, wq_ref[...]), gsum_ref, gexp_ref, gq_ref[...], FOX_HEAD_DIM)
    q16_ref[...] = (q * (FOX_HEAD_DIM ** -0.5 * LOG2E)).astype(BF16)
    sg_ref[...] = jax.nn.sigmoid(_dot(xn, wg_ref[...])).astype(sg_ref.dtype)


def _fox_q(x, g, wq, wg, gq, gsum, gexp, *, tm):
    n, d = x.shape
    tm = min(tm, n)
    assert n % tm == 0
    row = lambda i: (i, 0)
    fix = lambda i: (0, 0)
    vmem = 2 * tm * d * (4 + 2 + 4) + 2 * 2 * (2 * d * d + 2 * d * LANES) + 6 * tm * d * 4
    return pl.pallas_call(
        _fox_q_body,
        out_shape=[jax.ShapeDtypeStruct((n, d), BF16), jax.ShapeDtypeStruct((n, d), BF16)],
        grid=(n // tm,),
        in_specs=[
            pl.BlockSpec((tm, d), row), pl.BlockSpec((1, d), fix),
            pl.BlockSpec((d, d), fix), pl.BlockSpec((d, d), fix), pl.BlockSpec((1, d), fix),
            pl.BlockSpec((d, LANES), fix), pl.BlockSpec((LANES, d), fix),
        ],
        out_specs=[pl.BlockSpec((tm, d), row), pl.BlockSpec((tm, d), row)],
        compiler_params=_cparams(("parallel",), vmem),
        name="fox_q",
    )(x, g.reshape(1, d), wq, wg, gq, gsum, gexp)


def _pack_parts(x):
    p1, p2, p3 = (p.astype(F32) for p in _split3(x))
    packed = p1 + pltpu.roll(p2, FOX_HEADS, 1) + pltpu.roll(p3, 2 * FOX_HEADS, 1)
    return packed.astype(BF16)


def _bias_body(lf_ref, tri_ref, sel_ref, one_ref, qb_ref, kb_ref, carry_ref):
    @pl.when(pl.program_id(1) == 0)
    def _():
        carry_ref[...] = jnp.zeros_like(carry_ref)

    lf = lf_ref[...]
    tb = lf.shape[0]
    head_lanes = lax.broadcasted_iota(jnp.int32, lf.shape, 1) < FOX_HEADS
    cp = _dot(tri_ref[...], _pack_parts(jnp.where(head_lanes, lf, 0.0)))
    c = cp + pltpu.roll(cp, LANES - FOX_HEADS, 1) + pltpu.roll(cp, LANES - 2 * FOX_HEADS, 1)
    c = jnp.where(head_lanes, c, 0.0) + carry_ref[...]
    carry_ref[...] = c[tb - 1:tb, :]
    qkb = _dot(_pack_parts(c * LOG2E), sel_ref[...]) + one_ref[...]
    wide = FOX_PAIRS * LANES
    for p in range(FOX_PAIRS):
        qb_ref[p] = qkb[:, p * LANES:(p + 1) * LANES].astype(BF16)
        kb_ref[p] = qkb[:, wide + p * LANES:wide + (p + 1) * LANES].astype(BF16)


def _bias_constants():
    wide = FOX_PAIRS * LANES
    sel = np.zeros((LANES, 2 * wide), np.float32)
    one = np.zeros((1, 2 * wide), np.float32)
    for p in range(FOX_PAIRS):
        for e in range(2):
            base = p * LANES + e * BIAS_LANES_PER_HEAD
            for part in range(BIAS_PARTS):
                row = part * FOX_HEADS + 2 * p + e
                sel[row, base + part] = 1.0
                one[0, base + BIAS_PARTS + part] = 1.0
                one[0, wide + base + part] = 1.0
                sel[row, wide + base + BIAS_PARTS + part] = -1.0
    return jnp.asarray(sel, BF16), jnp.asarray(one)


def _fox_bias(lf, *, tb):
    bsz, kp, _ = lf.shape
    assert kp % tb == 0
    sel, one = _bias_constants()
    tri = jnp.tril(jnp.ones((tb, tb), F32)).astype(BF16)
    wide = FOX_PAIRS * LANES
    out_spec = pl.BlockSpec((None, FOX_PAIRS, tb, LANES), lambda b, i: (b, 0, i, 0))
    vmem = 2 * tb * LANES * 4 + 2 * tb * tb * 2 + 2 * LANES * 2 * wide * 2 \
        + 4 * FOX_PAIRS * tb * LANES * 2 + 4 * tb * 2 * wide * 4
    return pl.pallas_call(
        _bias_body,
        out_shape=[jax.ShapeDtypeStruct((bsz, FOX_PAIRS, kp, LANES), BF16)] * 2,
        grid=(bsz, kp // tb),
        in_specs=[
            pl.BlockSpec((None, tb, LANES), lambda b, i: (b, i, 0)),
            pl.BlockSpec((tb, tb), lambda b, i: (0, 0)),
            pl.BlockSpec((LANES, 2 * wide), lambda b, i: (0, 0)),
            pl.BlockSpec((1, 2 * wide), lambda b, i: (0, 0)),
        ],
        out_specs=[out_spec, out_spec],
        scratch_shapes=[pltpu.VMEM((1, LANES), F32)],
        compiler_params=_cparams(("parallel", "arbitrary"), vmem),
        name="fox_bias",
    )(lf, tri, sel, one)


def _q_aug(q, qb, lane):
    zero = jnp.zeros_like(q)
    out = []
    for e in range(2):
        head_lanes = (lane >> 6) == e
        bias_lanes = jnp.logical_and(lane >= e * BIAS_LANES_PER_HEAD,
                                     lane < (e + 1) * BIAS_LANES_PER_HEAD)
        out.append(jnp.concatenate(
            [jnp.where(head_lanes, q, zero), jnp.where(bias_lanes, qb, zero)], axis=1))
    return out


def _attn_body(q_ref, qb_ref, k_ref, kb_ref, v_ref, sg_ref, o_ref, m_ref, l_ref, acc_ref,
               *, tq, tk, n_qblocks):
    i = pl.program_id(2)
    lane = lax.broadcasted_iota(jnp.int32, (tq, LANES), 1)
    q_aug = _q_aug(q_ref[...], qb_ref[...], lane)

    m_ref[...] = jnp.full(m_ref.shape, NEG_INF, F32)
    l_ref[...] = jnp.zeros(l_ref.shape, F32)
    acc_ref[...] = jnp.zeros(acc_ref.shape, F32)

    def qk_scores(j, qwin=slice(None)):
        rows = slice(j * tk, (j + 1) * tk)
        k_aug = jnp.concatenate([k_ref[rows, :], kb_ref[rows, :]], axis=1)
        qs = q_aug if qwin == slice(None) else [qa[qwin] for qa in q_aug]
        return [lax.dot_general(k_aug, qs[e], NT_DIMS, preferred_element_type=F32)
                for e in range(2)]

    def softmax_pv(j, scores, qwin=slice(None)):
        v = v_ref[j]
        if qwin == slice(None):
            at = lambda ref, e: ref.at[e]
        else:
            at = lambda ref, e: ref.at[e, :, qwin]
        for e in range(2):
            s = scores[e]
            m_prev = at(m_ref, e)[...]
            m_new = jnp.maximum(m_prev, jnp.max(s, axis=0, keepdims=True))
            alpha = jnp.exp2(m_prev - m_new)
            p = jnp.exp2(s - m_new)
            at(l_ref, e)[...] = alpha * at(l_ref, e)[...] + jnp.sum(p, axis=0, keepdims=True)
            p16 = p.astype(BF16)
            at(acc_ref, e)[...] = alpha * at(acc_ref, e)[...] + _dot(v, p16)
            at(m_ref, e)[...] = m_new

    early, late = slice(0, tk), slice(tk, tq)

    def schedule(n_visible):
        ahead = qk_scores(0)
        for j in range(n_visible):
            cur, ahead = ahead, qk_scores(j + 1)
            softmax_pv(j, cur)
        tmask = jnp.where(lax.broadcasted_iota(jnp.int32, (tk, tk), 0)
                          <= lax.broadcasted_iota(jnp.int32, (tk, tk), 1), 0.0, NEG_INF)
        late_scores = qk_scores(n_visible + 1, late)
        softmax_pv(n_visible, [jnp.concatenate([s[:, early] + tmask, s[:, late]], axis=1)
                               for s in ahead])
        softmax_pv(n_visible + 1, [s + tmask for s in late_scores], late)

    lax.switch(i, [functools.partial(schedule, 2 * n) for n in range(n_qblocks)])

    o0 = acc_ref[0] * (1.0 / l_ref[0])
    o1 = acc_ref[1] * (1.0 / l_ref[1])
    feature = lax.broadcasted_iota(jnp.int32, (LANES, tq), 0)
    o = jnp.where((feature >> 6) == 0, o0, o1).T
    o_ref[...] = (o * sg_ref[...].astype(F32)).astype(BF16)


def _fox_attn(q16, qb, k16, kb, vt16, sg, *, tq, tk):
    bsz, t, d = q16.shape
    assert t % tq == 0 and tq == 2 * tk and k16.shape[2] == t
    qspec = pl.BlockSpec((None, tq, LANES), lambda b, p, i: (b, i, p))
    vmem = 2 * (3 * t * LANES * 2 + tq * LANES * (2 + 2 + 2 + 2)) + 3 * tq * LANES * 4 \
        + 2 * tk * 2 * LANES * 2 + 10 * tq * tk * 4
    return pl.pallas_call(
        functools.partial(_attn_body, tq=tq, tk=tk, n_qblocks=t // tq),
        out_shape=jax.ShapeDtypeStruct((bsz, t, d), BF16),
        grid=(bsz, FOX_PAIRS, t // tq),
        in_specs=[
            qspec,
            pl.BlockSpec((None, None, tq, LANES), lambda b, p, i: (b, p, i, 0)),
            pl.BlockSpec((None, None, t, LANES), lambda b, p, i: (p, b, 0, 0)),
            pl.BlockSpec((None, None, t, LANES), lambda b, p, i: (b, p, 0, 0)),
            pl.BlockSpec((t // tk, LANES, tk), lambda b, p, i: (b, p, 0)),
            qspec,
        ],
        out_specs=qspec,
        scratch_shapes=[pltpu.VMEM((2, 1, tq), F32), pltpu.VMEM((2, 1, tq), F32),
                        pltpu.VMEM((2, LANES, tq), F32)],
        compiler_params=_cparams(("parallel", "parallel", "arbitrary"), vmem + (8 << 20)),
        name="fox_attn",
    )(q16, qb, k16, kb, vt16, sg)


def _cache_attn_body(q_ref, qb_ref, ck_ref, cv_ref, kn_ref, vn_ref, kb_ref, sg_ref,
                     o_ref, m_ref, l_ref, acc_ref, *, tk):
    t = q_ref.shape[0]
    p_len = ck_ref.shape[0]
    lane = lax.broadcasted_iota(jnp.int32, (t, LANES), 1)
    q_aug = _q_aug(q_ref[...], qb_ref[...], lane)
    m_ref[...] = jnp.full(m_ref.shape, NEG_INF, F32)
    l_ref[...] = jnp.zeros(l_ref.shape, F32)
    acc_ref[...] = jnp.zeros(acc_ref.shape, F32)

    def fold(k, kb, v, mask):
        k_aug = jnp.concatenate([k, kb], axis=1)
        scores = [lax.dot_general(q_aug[e], k_aug, NT_DIMS, preferred_element_type=F32)
                  for e in range(2)]
        for e in range(2):
            s = scores[e] if mask is None else scores[e] + mask
            m_prev = m_ref[e]
            m_new = jnp.maximum(m_prev, jnp.max(s, axis=1, keepdims=True))
            alpha = jnp.exp2(m_prev - m_new)
            p = jnp.exp2(s - m_new)
            l_ref[e] = alpha * l_ref[e] + jnp.sum(p, axis=1, keepdims=True)
            acc_ref[e] = alpha * acc_ref[e] + _dot(p.astype(BF16), v)
            m_ref[e] = m_new

    for j in range(p_len // tk):
        rows = slice(j * tk, (j + 1) * tk)
        fold(ck_ref[rows, :], kb_ref[rows, :], cv_ref[rows, :], None)
    causal = jnp.where(lax.broadcasted_iota(jnp.int32, (t, t), 1)
                       <= lax.broadcasted_iota(jnp.int32, (t, t), 0), 0.0, NEG_INF)
    fold(kn_ref[...], kb_ref[p_len:p_len + t, :], vn_ref[...], causal)

    o = jnp.where((lane >> 6) == 0, acc_ref[0] * (1.0 / l_ref[0]), acc_ref[1] * (1.0 / l_ref[1]))
    o_ref[...] = (o * sg_ref[...].astype(F32)).astype(BF16)


def _fox_attn_cached(q16, qb, kb, ck16, cv16, k16, v16, sg, *, tk):
    bsz, t, d = q16.shape
    p_len = ck16.shape[2]
    assert p_len % tk == 0 and p_len % t == 0 and kb.shape[2] == p_len + t
    qspec = pl.BlockSpec((None, t, LANES), lambda b, p: (b, 0, p))
    cspec = pl.BlockSpec((None, None, p_len, LANES), lambda b, p: (p, b, 0, 0))
    nspec = pl.BlockSpec((None, t, LANES), lambda b, p: (p, b, 0))
    vmem = 2 * (3 * (p_len + t) * LANES * 2 + t * LANES * 8) + 6 * t * LANES * 4 + 8 * t * tk * 4 \
        + 4 * tk * LANES * 2
    return pl.pallas_call(
        functools.partial(_cache_attn_body, tk=tk),
        out_shape=jax.ShapeDtypeStruct((bsz, t, d), BF16),
        grid=(bsz, FOX_PAIRS),
        in_specs=[
            qspec,
            pl.BlockSpec((None, None, t, LANES), lambda b, p: (b, p, p_len // t, 0)),
            cspec, cspec, nspec, nspec,
            pl.BlockSpec((None, None, p_len + t, LANES), lambda b, p: (b, p, 0, 0)),
            qspec,
        ],
        out_specs=qspec,
        scratch_shapes=[pltpu.VMEM((2, t, 1), F32), pltpu.VMEM((2, t, 1), F32),
                        pltpu.VMEM((2, t, LANES), F32)],
        compiler_params=_cparams(("parallel", "parallel"), vmem + (8 << 20)),
        name="fox_attn_cached",
    )(q16, qb, ck16, cv16, k16, v16, kb, sg)


TM = 1024
KV_TM = 512
FFN_TM = 1024
FFN_TF = 256
REPACK_ROWS = 128
GLA_TB = 256
ATTN_TQ = 1024
ATTN_TK = 512
CACHE_TK = 4096
BIAS_TB_CHOICES = (832, 512, 384, 320, 256, 192, 128, 64)


def _round_up(x, m):
    return (x + m - 1) // m * m


def _prep_weights(ffn_norm, w_ffn_gu, w_ffn_down, mix_norm, a_w_in, a_w_g2, a_b_g, a_g_out, a_w_o,
                  kv_norm, w_kvf, b_f, g_k, b_w_qg, b_g_q, b_w_o):
    d = w_ffn_gu.shape[2]
    rank = a_w_g2.shape[1]
    qk = a_w_g2.shape[2]
    vw = a_w_o.shape[1]
    w_in = a_w_in[0]
    pad_cols = lambda w: jnp.pad(w, ((0, 0), (0, LANES - w.shape[1])))
    groups = np.arange(d) // FOX_HEAD_DIM
    gsum = (groups[:, None] == np.arange(LANES)[None, :]).astype(np.float32)
    return dict(
        ffn_norm=ffn_norm, mix_norm=mix_norm, kv_norm=kv_norm,
        w_gu=_tile_cols(w_ffn_gu.reshape((-1,) + w_ffn_gu.shape[2:]), FFN_TF, rows=REPACK_ROWS),
        w_down=w_ffn_down.astype(BF16),
        a_wq=w_in[:, :qk].astype(BF16), a_wk=w_in[:, qk:2 * qk].astype(BF16),
        a_wv=w_in[:, 2 * qk:2 * qk + vw].astype(BF16),
        a_wr=w_in[:, 2 * qk + vw:2 * qk + 2 * vw].astype(BF16),
        a_wgl=pad_cols(w_in[:, 2 * qk + 2 * vw:]).astype(BF16),
        a_wg2=jnp.pad(a_w_g2[0], ((0, LANES - rank), (0, 0))).astype(BF16),
        a_bg=a_b_g[0], a_gout=a_g_out[0], a_wo=a_w_o[0].astype(BF16),
        kv_wk=w_kvf[:, :d].astype(BF16), kv_wv=w_kvf[:, d:2 * d].astype(BF16),
        kv_wv_t=w_kvf[:, d:2 * d].T.astype(BF16),
        kv_wf=pad_cols(w_kvf[:, 2 * d:]).astype(BF16),
        kv_bf=jnp.pad(b_f, (0, LANES - b_f.shape[0])).reshape(1, LANES),
        gk=jnp.tile(g_k, FOX_HEADS).reshape(1, d), gq=jnp.tile(b_g_q[0], FOX_HEADS).reshape(1, d),
        b_wq=b_w_qg[0][:, :d].astype(BF16), b_wg=b_w_qg[0][:, d:].astype(BF16),
        b_wo=b_w_o[0].astype(BF16),
        gsum=jnp.asarray(gsum, BF16), gexp=jnp.asarray(gsum.T, BF16),
    )


def _trunk(x, s0, past, w):
    bsz, t, d = x.shape
    n = bsz * t
    h = x.reshape(n, d)
    ffn = lambda h_, layer, half, mixer_out=None: _ffn(
        h_, w["ffn_norm"][layer, half], w["w_gu"][2 * layer + half], w["w_down"][layer, half],
        tm=FFN_TM, mixer_out=mixer_out)
    h = ffn(h, 0, 0)
    q, k, v, r, la = _gla_in(h, w["mix_norm"][0], w["a_wq"], w["a_wk"], w["a_wv"], w["a_wr"],
                             w["a_wgl"], w["a_wg2"], w["a_bg"], tm=TM)
    s3 = lambda z: z.reshape(bsz, t, z.shape[1])
    chunk = 64
    og, st_fin = _gla(s3(q), s3(k), s3(v), s3(r), s3(la), jnp.swapaxes(s0, 2, 3), w["a_gout"],
                      chunk=chunk, tb=GLA_TB)
    h = ffn(h, 0, 1, mixer_out=(og.reshape(n, -1), w["a_wo"]))
    tq = min(ATTN_TQ, t)
    tk = min(ATTN_TK, tq // 2)
    k_new, v_new, lf, k16, v16 = _shared_kv(
        h, w["kv_norm"], w["kv_wk"], w["kv_wv_t"] if past is None else w["kv_wv"], w["kv_wf"],
        w["kv_bf"], w["gk"], w["gsum"], w["gexp"], tm=KV_TM, vt_block=tk if past is None else None)
    h = ffn(h, 1, 0)
    q16, sg = _fox_q(h, w["mix_norm"][1], w["b_wq"], w["b_wg"], w["gq"], w["gsum"], w["gexp"], tm=TM)
    pairs4 = lambda z: z.reshape(FOX_PAIRS, bsz, t, LANES)
    if past is None:
        lf_all = s3(lf)
    else:
        past_k, past_v, past_lf = past
        p_len = past_k.shape[1]
        past_pairs = lambda a: jnp.transpose(
            a.astype(BF16).reshape(bsz, p_len, FOX_PAIRS, LANES), (2, 0, 1, 3))
        lf_all = jnp.concatenate(
            [jnp.pad(past_lf, ((0, 0), (0, 0), (0, LANES - past_lf.shape[2]))), s3(lf)], axis=1)
    bias_tb = next(c for c in BIAS_TB_CHOICES if lf_all.shape[1] % c == 0)
    qb, kb = _fox_bias(lf_all, tb=bias_tb)
    if past is None:
        og = _fox_attn(s3(q16), qb, pairs4(k16), kb, v16, s3(sg), tq=tq, tk=tk)
    else:
        og = _fox_attn_cached(s3(q16), qb, kb, past_pairs(past_k), past_pairs(past_v), k16, v16,
                              s3(sg), tk=min(CACHE_TK, p_len))
    h = ffn(h, 1, 1, mixer_out=(og.reshape(n, d), w["b_wo"]))
    heads4 = lambda z: z.reshape(bsz, t, FOX_HEADS, FOX_HEAD_DIM)
    return (h.reshape(bsz, t, d), jnp.swapaxes(st_fin, 2, 3)[:, None], heads4(k_new), heads4(v_new),
            lf[:, :FOX_HEADS].reshape(bsz, t, FOX_HEADS))


def kernel(x_prompt, x_sample, state_gla, cache_k, cache_v, cache_logf, ffn_norm, w_ffn_gu, w_ffn_down, mix_norm, a_w_in, a_w_g2, a_b_g, a_g_out, a_w_o, kv_norm, w_kvf, b_f, g_k, b_w_qg, b_g_q, b_w_o):
    w = _prep_weights(ffn_norm, w_ffn_gu, w_ffn_down, mix_norm, a_w_in, a_w_g2, a_b_g, a_g_out,
                      a_w_o, kv_norm, w_kvf, b_f, g_k, b_w_qg, b_g_q, b_w_o)
    s0_prompt = jnp.zeros((x_prompt.shape[0],) + state_gla.shape[2:], F32)
    y_p, gla_p, k_p, v_p, lf_p = _trunk(x_prompt, s0_prompt, None, w)
    y_s, gla_s, k_s, v_s, lf_s = _trunk(x_sample, state_gla[:, 0], (cache_k, cache_v, cache_logf), w)
    return (y_p, y_s, gla_p, gla_s, k_p, v_p, lf_p, k_s, v_s, lf_s)
```

```python
import functools

import jax
import jax.numpy as jnp
import numpy as np
from jax import lax
from jax.experimental import pallas as pl
from jax.experimental.pallas import tpu as pltpu

F32 = jnp.float32
BF16 = jnp.bfloat16

EPS = 1e-6
NEG_INF = -1e30
LOG2E = 1.4426950408889634

LANES = 128
V7X_SCOPED_VMEM_CAP = 60000 * 1024

GLA_GATE_TAU = 16.0
FOX_HEADS = 16
FOX_HEAD_DIM = 64
FOX_PAIRS = FOX_HEADS // 2
BIAS_PARTS = 3
BIAS_LANES_PER_HEAD = 2 * BIAS_PARTS

NT_DIMS = (((1,), (1,)), ((), ()))
TN_DIMS = (((0,), (0,)), ((), ()))


def _cparams(semantics, vmem_bytes):
    limit = int(min(max(vmem_bytes, 16 * 1024 * 1024), V7X_SCOPED_VMEM_CAP))
    return pltpu.CompilerParams(dimension_semantics=semantics, vmem_limit_bytes=limit)


def _dot(a, b):
    return jnp.dot(a, b, preferred_element_type=F32)


def _rms(x, g):
    ms = jnp.mean(x * x, axis=-1, keepdims=True)
    return x * lax.rsqrt(ms + EPS) * g


def _split2(x):
    hi = x.astype(BF16)
    lo = (x - hi.astype(F32)).astype(BF16)
    return hi, lo


def _split3(x):
    p1 = x.astype(BF16)
    r1 = x - p1.astype(F32)
    p2 = r1.astype(BF16)
    p3 = (r1 - p2.astype(F32)).astype(BF16)
    return p1, p2, p3


def _log_sigmoid(x):
    return jnp.minimum(x, 0.0) - jnp.log(1.0 + jnp.exp(-jnp.abs(x)))


def _head_norm(x, gsum_ref, gexp_ref, gain, head_dim):
    ssq = _dot((x * x).astype(BF16), gsum_ref[...])
    inv = lax.rsqrt(ssq * (1.0 / head_dim) + EPS)
    ihi, ilo = _split2(inv)
    inv_full = _dot(ihi, gexp_ref[...]) + _dot(ilo, gexp_ref[...])
    return x * inv_full * gain


def _ffn_body(x_ref, g_ref, wg_ref, wu_ref, wd_ref, o_ref, xn_ref):
    j = pl.program_id(1)

    @pl.when(j == 0)
    def _():
        x = x_ref[...]
        xn_ref[...] = _rms(x, g_ref[...]).astype(BF16)
        o_ref[...] = x

    xn = xn_ref[...]
    g = _dot(xn, wg_ref[...])
    u = _dot(xn, wu_ref[...])
    h = (0.5 * (g * jax.nn.sigmoid(g) * u)).astype(BF16)
    o_ref[...] += _dot(h, wd_ref[...])


def _tile_cols_body(w_ref, o_ref):
    tf = o_ref.shape[2]
    for c in range(o_ref.shape[0]):
        o_ref[c] = w_ref[:, c * tf:(c + 1) * tf].astype(o_ref.dtype)


def _tile_cols(w, tf, *, rows):
    g, d, n = w.shape
    assert n % tf == 0 and d % rows == 0
    return pl.pallas_call(
        _tile_cols_body,
        out_shape=jax.ShapeDtypeStruct((g, n // tf, d, tf), BF16),
        grid=(g, d // rows),
        in_specs=[pl.BlockSpec((None, rows, n), lambda a, r: (a, r, 0))],
        out_specs=pl.BlockSpec((None, n // tf, rows, tf), lambda a, r: (a, 0, r, 0)),
        compiler_params=_cparams(("parallel", "parallel"), 2 * rows * n * (4 + 2) + (4 << 20)),
        name="tile_cols",
    )(w)


def _ffn(x, g, w_gu, w_down, which, *, tm):
    n, d = x.shape
    d_ff = w_down.shape[1]
    tf = w_gu.shape[3]
    tm = min(tm, n)
    nf = d_ff // tf
    assert n % tm == 0 and w_gu.shape[1] == 2 * nf
    vmem = 4 * tm * d * 4 + tm * d * 2 + 2 * 3 * d * tf * 2 + 3 * tm * tf * 4 + tm * d * 4
    return pl.pallas_call(
        _ffn_body,
        out_shape=jax.ShapeDtypeStruct((n, d), F32),
        grid=(n // tm, nf),
        in_specs=[
            pl.BlockSpec((tm, d), lambda i, j: (i, 0)),
            pl.BlockSpec((1, d), lambda i, j: (0, 0)),
            pl.BlockSpec((None, None, d, tf), lambda i, j: (which, j, 0, 0)),
            pl.BlockSpec((None, None, d, tf), lambda i, j: (which, j + nf, 0, 0)),
            pl.BlockSpec((None, tf, d), lambda i, j: (which, j, 0)),
        ],
        out_specs=pl.BlockSpec((tm, d), lambda i, j: (i, 0)),
        scratch_shapes=[pltpu.VMEM((tm, d), BF16)],
        compiler_params=_cparams(("parallel", "arbitrary"), vmem + (8 << 20)),
        name="ffn",
    )(x, g.reshape(1, d), w_gu, w_gu, w_down)


def _proj_res_body(h_ref, a_ref, w_ref, o_ref):
    o_ref[...] = h_ref[...] + _dot(a_ref[...], w_ref[...])


def _proj_res(h, a, w, *, tm):
    n, d = h.shape
    k = a.shape[1]
    tm = min(tm, n)
    assert n % tm == 0
    vmem = 2 * (2 * tm * d * 4 + tm * k * 2 + k * d * 2) + tm * d * 4
    return pl.pallas_call(
        _proj_res_body,
        out_shape=jax.ShapeDtypeStruct((n, d), F32),
        grid=(n // tm,),
        in_specs=[
            pl.BlockSpec((tm, d), lambda i: (i, 0)),
            pl.BlockSpec((tm, k), lambda i: (i, 0)),
            pl.BlockSpec((k, d), lambda i: (0, 0)),
        ],
        out_specs=pl.BlockSpec((tm, d), lambda i: (i, 0)),
        compiler_params=_cparams(("parallel",), vmem + (4 << 20)),
        name="proj_res",
    )(h, a, w)


def _gla_in_body(x_ref, g_ref, wq_ref, wk_ref, wv_ref, wr_ref, wgl_ref, wg2_ref, bg_ref,
                 q_ref, k_ref, v_ref, r_ref, la_ref):
    xn = _rms(x_ref[...], g_ref[...]).astype(BF16)
    q_ref[...] = _dot(xn, wq_ref[...]).astype(q_ref.dtype)
    k_ref[...] = _dot(xn, wk_ref[...]).astype(k_ref.dtype)
    v_ref[...] = _dot(xn, wv_ref[...]).astype(v_ref.dtype)
    r_ref[...] = _dot(xn, wr_ref[...]).astype(r_ref.dtype)
    gl = _dot(xn, wgl_ref[...]).astype(BF16)
    z = _dot(gl, wg2_ref[...]) + bg_ref[...]
    la_ref[...] = _log_sigmoid(z) * (1.0 / GLA_GATE_TAU)


def _gla_in(x, g, wq, wk, wv, wr, wgl, wg2, bg, *, tm):
    n, d = x.shape
    qk, vw = wq.shape[1], wv.shape[1]
    tm = min(tm, n)
    assert n % tm == 0
    row = lambda i: (i, 0)
    fix = lambda i: (0, 0)
    w_bytes = 2 * (2 * d * qk + 2 * d * vw + d * LANES + LANES * qk)
    vmem = 2 * tm * (d + 3 * qk + 2 * vw) * 4 + 2 * w_bytes + tm * d * 2
    return pl.pallas_call(
        _gla_in_body,
        out_shape=[jax.ShapeDtypeStruct((n, qk), BF16), jax.ShapeDtypeStruct((n, qk), BF16),
                   jax.ShapeDtypeStruct((n, vw), BF16), jax.ShapeDtypeStruct((n, vw), BF16),
                   jax.ShapeDtypeStruct((n, qk), F32)],
        grid=(n // tm,),
        in_specs=[
            pl.BlockSpec((tm, d), row), pl.BlockSpec((1, d), fix),
            pl.BlockSpec((d, qk), fix), pl.BlockSpec((d, qk), fix),
            pl.BlockSpec((d, vw), fix), pl.BlockSpec((d, vw), fix),
            pl.BlockSpec((d, LANES), fix), pl.BlockSpec((LANES, qk), fix), pl.BlockSpec((1, qk), fix),
        ],
        out_specs=[pl.BlockSpec((tm, qk), row), pl.BlockSpec((tm, qk), row),
                   pl.BlockSpec((tm, vw), row), pl.BlockSpec((tm, vw), row),
                   pl.BlockSpec((tm, qk), row)],
        compiler_params=_cparams(("parallel",), vmem + (8 << 20)),
        name="gla_in",
    )(x, g.reshape(1, d), wq, wk, wv, wr, wgl, wg2, bg.reshape(1, qk))


def _gla_body(q_ref, k_ref, v_ref, r_ref, la_ref, s0_ref, gout_ref, tri_ref,
              og_ref, sfin_ref, st_ref, *, chunk, n_chunks, heads, dk, dv):
    t = pl.program_id(1)

    @pl.when(t == 0)
    def _():
        st_ref[...] = s0_ref[...]

    scale = dk ** -0.5
    row = lax.broadcasted_iota(jnp.int32, (chunk, chunk), 0)
    col = lax.broadcasted_iota(jnp.int32, (chunk, chunk), 1)
    causal = col <= row
    tri = tri_ref[...]
    gout = gout_ref[...]

    for c in range(n_chunks):
        sl = slice(c * chunk, (c + 1) * chunk)
        la_hi, la_lo = _split2(la_ref[sl, :])
        b = _dot(tri, la_hi) + _dot(tri, la_lo)
        b_last = b[chunk - 1:chunk, :]
        q = q_ref[sl, :].astype(F32)
        k = k_ref[sl, :].astype(F32)
        qe = (q * scale * jnp.exp(b)).astype(BF16)
        ke = (k * jnp.exp(-b)).astype(BF16)
        kd = (k * jnp.exp(b_last - b)).astype(BF16)
        dec = jnp.exp(b_last)
        for h in range(heads):
            ks = slice(h * dk, (h + 1) * dk)
            vs = slice(h * dv, (h + 1) * dv)
            vh = v_ref[sl, vs]
            att = lax.dot_general(qe[:, ks], ke[:, ks], NT_DIMS, preferred_element_type=F32)
            att = jnp.where(causal, att, 0.0).astype(BF16)
            st = st_ref[h]
            o = _dot(att, vh) + lax.dot_general(qe[:, ks], st.astype(BF16), NT_DIMS,
                                                preferred_element_type=F32)
            st_ref[h] = st * dec[:, ks] + lax.dot_general(vh, kd[:, ks], TN_DIMS,
                                                          preferred_element_type=F32)
            on = _rms(o, gout)
            rh = r_ref[sl, vs].astype(F32)
            og_ref[sl, vs] = (on * (rh * jax.nn.sigmoid(rh))).astype(BF16)

    @pl.when(t == pl.num_programs(1) - 1)
    def _():
        sfin_ref[...] = st_ref[...]


def _gla(q, k, v, r, la, s0t, gout, *, chunk, tb):
    bsz, t, qk = q.shape
    vw = v.shape[2]
    heads = s0t.shape[1]
    dk, dv = qk // heads, vw // heads
    tb = min(tb, t)
    assert t % tb == 0 and tb % chunk == 0
    tri = jnp.tril(jnp.ones((chunk, chunk), F32)).astype(BF16)
    blk = lambda w: pl.BlockSpec((None, tb, w), lambda b, i: (b, i, 0))
    st_spec = pl.BlockSpec((None, heads, dv, dk), lambda b, i: (b, 0, 0, 0))
    vmem = 2 * tb * (3 * qk + 2 * vw) * 4 + 2 * tb * vw * 2 + 5 * heads * dv * dk * 4
    return pl.pallas_call(
        functools.partial(_gla_body, chunk=chunk, n_chunks=tb // chunk, heads=heads, dk=dk, dv=dv),
        out_shape=[jax.ShapeDtypeStruct((bsz, t, vw), BF16),
                   jax.ShapeDtypeStruct((bsz, heads, dv, dk), F32)],
        grid=(bsz, t // tb),
        in_specs=[blk(qk), blk(qk), blk(vw), blk(vw), blk(qk), st_spec,
                  pl.BlockSpec((1, dv), lambda b, i: (0, 0)),
                  pl.BlockSpec((chunk, chunk), lambda b, i: (0, 0))],
        out_specs=[blk(vw), st_spec],
        scratch_shapes=[pltpu.VMEM((heads, dv, dk), F32)],
        compiler_params=_cparams(("parallel", "arbitrary"), vmem + (8 << 20)),
        name="gla_chunks",
    )(q, k, v, r, la, s0t, gout.reshape(1, dv), tri)


def _store_pairs(ref, x):
    for p in range(ref.shape[0]):
        ref[p] = x[:, p * LANES:(p + 1) * LANES]


def _kv_body(x_ref, g_ref, wk_ref, wv_ref, wf_ref, bf_ref, gk_ref, gsum_ref, gexp_ref,
             k_ref, v_ref, lf_ref, k16_ref, v16_ref, *, vt_block):
    xn = _rms(x_ref[...], g_ref[...]).astype(BF16)
    k = _head_norm(_dot(xn, wk_ref[...]), gsum_ref, gexp_ref, gk_ref[...], FOX_HEAD_DIM)
    k_ref[...] = k.reshape(k_ref.shape)
    _store_pairs(k16_ref, k.astype(BF16))
    lf_ref[...] = _log_sigmoid(_dot(xn, wf_ref[...]) + bf_ref[...])
    if vt_block is None:
        v = _dot(xn, wv_ref[...])
        v_ref[...] = v.reshape(v_ref.shape)
        _store_pairs(v16_ref, v.astype(BF16))
    else:
        vt = lax.dot_general(wv_ref[...], xn, NT_DIMS, preferred_element_type=F32)
        v_ref[...] = vt.T.reshape(v_ref.shape)
        vt16 = vt.astype(BF16)
        for c in range(v16_ref.shape[0]):
            v16_ref[c] = vt16[:, c * vt_block:(c + 1) * vt_block]


def _shared_kv(x, g, wk, wv, wf, bf, gk, gsum, gexp, *, tm, vt_block=None):
    n, d = x.shape
    tm = min(tm, n)
    assert n % tm == 0
    row = lambda i: (i, 0)
    fix = lambda i: (0, 0)
    heads_shape = (n, FOX_HEADS, FOX_HEAD_DIM)
    heads_spec = pl.BlockSpec((tm, FOX_HEADS, FOX_HEAD_DIM), lambda i: (i, 0, 0))
    pair_shape = (d // LANES, n, LANES)
    pair_spec = pl.BlockSpec((d // LANES, tm, LANES), lambda i: (0, i, 0))
    if vt_block is None:
        v16_shape, v16_spec = pair_shape, pair_spec
    else:
        assert tm % vt_block == 0
        v16_shape = (n // vt_block, d, vt_block)
        v16_spec = pl.BlockSpec((tm // vt_block, d, vt_block), lambda i: (i, 0, 0))
    vmem = 2 * tm * d * (4 + 8 + 8 + 2 + 2) + 2 * tm * LANES * 4 + 2 * 2 * (2 * d * d + 3 * d * LANES) \
        + 8 * tm * d * 4
    return pl.pallas_call(
        functools.partial(_kv_body, vt_block=vt_block),
        out_shape=[jax.ShapeDtypeStruct(heads_shape, F32), jax.ShapeDtypeStruct(heads_shape, F32),
                   jax.ShapeDtypeStruct((n, LANES), F32),
                   jax.ShapeDtypeStruct(pair_shape, BF16), jax.ShapeDtypeStruct(v16_shape, BF16)],
        grid=(n // tm,),
        in_specs=[
            pl.BlockSpec((tm, d), row), pl.BlockSpec((1, d), fix),
            pl.BlockSpec((d, d), fix), pl.BlockSpec((d, d), fix), pl.BlockSpec((d, LANES), fix),
            pl.BlockSpec((1, LANES), fix), pl.BlockSpec((1, d), fix),
            pl.BlockSpec((d, LANES), fix), pl.BlockSpec((LANES, d), fix),
        ],
        out_specs=[heads_spec, heads_spec,
                   pl.BlockSpec((tm, LANES), row),
                   pair_spec, v16_spec],
        compiler_params=_cparams(("parallel",), vmem),
        name="shared_kv",
    )(x, g.reshape(1, d), wk, wv, wf, bf, gk, gsum, gexp)


def _fox_q_body(x_ref, g_ref, wq_ref, wg_ref, gq_ref, gsum_ref, gexp_ref, q16_ref, sg_ref):
    xn = _rms(x_ref[...], g_ref[...]).astype(BF16)
    q = _head_norm(_dot(xn, wq_ref[...]), gsum_ref, gexp_ref, gq_ref[...], FOX_HEAD_DIM)
    q16_ref[...] = (q * (FOX_HEAD_DIM ** -0.5 * LOG2E)).astype(BF16)
    sg_ref[...] = jax.nn.sigmoid(_dot(xn, wg_ref[...])).astype(sg_ref.dtype)


def _fox_q(x, g, wq, wg, gq, gsum, gexp, *, tm):
    n, d = x.shape
    tm = min(tm, n)
    assert n % tm == 0
    row = lambda i: (i, 0)
    fix = lambda i: (0, 0)
    vmem = 2 * tm * d * (4 + 2 + 4) + 2 * 2 * (2 * d * d + 2 * d * LANES) + 6 * tm * d * 4
    return pl.pallas_call(
        _fox_q_body,
        out_shape=[jax.ShapeDtypeStruct((n, d), BF16), jax.ShapeDtypeStruct((n, d), BF16)],
        grid=(n // tm,),
        in_specs=[
            pl.BlockSpec((tm, d), row), pl.BlockSpec((1, d), fix),
            pl.BlockSpec((d, d), fix), pl.BlockSpec((d, d), fix), pl.BlockSpec((1, d), fix),
            pl.BlockSpec((d, LANES), fix), pl.BlockSpec((LANES, d), fix),
        ],
        out_specs=[pl.BlockSpec((tm, d), row), pl.BlockSpec((tm, d), row)],
        compiler_params=_cparams(("parallel",), vmem),
        name="fox_q",
    )(x, g.reshape(1, d), wq, wg, gq, gsum, gexp)


def _pack_parts(x):
    p1, p2, p3 = (p.astype(F32) for p in _split3(x))
    packed = p1 + pltpu.roll(p2, FOX_HEADS, 1) + pltpu.roll(p3, 2 * FOX_HEADS, 1)
    return packed.astype(BF16)


def _bias_body(lf_ref, tri_ref, sel_ref, one_ref, qb_ref, kb_ref, carry_ref):
    @pl.when(pl.program_id(1) == 0)
    def _():
        carry_ref[...] = jnp.zeros_like(carry_ref)

    lf = lf_ref[...]
    tb = lf.shape[0]
    head_lanes = lax.broadcasted_iota(jnp.int32, lf.shape, 1) < FOX_HEADS
    cp = _dot(tri_ref[...], _pack_parts(jnp.where(head_lanes, lf, 0.0)))
    c = cp + pltpu.roll(cp, LANES - FOX_HEADS, 1) + pltpu.roll(cp, LANES - 2 * FOX_HEADS, 1)
    c = jnp.where(head_lanes, c, 0.0) + carry_ref[...]
    carry_ref[...] = c[tb - 1:tb, :]
    qkb = _dot(_pack_parts(c * LOG2E), sel_ref[...]) + one_ref[...]
    wide = FOX_PAIRS * LANES
    for p in range(FOX_PAIRS):
        qb_ref[p] = qkb[:, p * LANES:(p + 1) * LANES].astype(BF16)
        kb_ref[p] = qkb[:, wide + p * LANES:wide + (p + 1) * LANES].astype(BF16)


def _bias_constants():
    wide = FOX_PAIRS * LANES
    sel = np.zeros((LANES, 2 * wide), np.float32)
    one = np.zeros((1, 2 * wide), np.float32)
    for p in range(FOX_PAIRS):
        for e in range(2):
            base = p * LANES + e * BIAS_LANES_PER_HEAD
            for part in range(BIAS_PARTS):
                row = part * FOX_HEADS + 2 * p + e
                sel[row, base + part] = 1.0
                one[0, base + BIAS_PARTS + part] = 1.0
                one[0, wide + base + part] = 1.0
                sel[row, wide + base + BIAS_PARTS + part] = -1.0
    return jnp.asarray(sel, BF16), jnp.asarray(one)


def _fox_bias(lf, *, tb):
    bsz, kp, _ = lf.shape
    assert kp % tb == 0
    sel, one = _bias_constants()
    tri = jnp.tril(jnp.ones((tb, tb), F32)).astype(BF16)
    wide = FOX_PAIRS * LANES
    out_spec = pl.BlockSpec((None, FOX_PAIRS, tb, LANES), lambda b, i: (b, 0, i, 0))
    vmem = 2 * tb * LANES * 4 + 2 * tb * tb * 2 + 2 * LANES * 2 * wide * 2 \
        + 4 * FOX_PAIRS * tb * LANES * 2 + 4 * tb * 2 * wide * 4
    return pl.pallas_call(
        _bias_body,
        out_shape=[jax.ShapeDtypeStruct((bsz, FOX_PAIRS, kp, LANES), BF16)] * 2,
        grid=(bsz, kp // tb),
        in_specs=[
            pl.BlockSpec((None, tb, LANES), lambda b, i: (b, i, 0)),
            pl.BlockSpec((tb, tb), lambda b, i: (0, 0)),
            pl.BlockSpec((LANES, 2 * wide), lambda b, i: (0, 0)),
            pl.BlockSpec((1, 2 * wide), lambda b, i: (0, 0)),
        ],
        out_specs=[out_spec, out_spec],
        scratch_shapes=[pltpu.VMEM((1, LANES), F32)],
        compiler_params=_cparams(("parallel", "arbitrary"), vmem),
        name="fox_bias",
    )(lf, tri, sel, one)


def _q_aug(q, qb, lane):
    zero = jnp.zeros_like(q)
    out = []
    for e in range(2):
        head_lanes = (lane >> 6) == e
        bias_lanes = jnp.logical_and(lane >= e * BIAS_LANES_PER_HEAD,
                                     lane < (e + 1) * BIAS_LANES_PER_HEAD)
        out.append(jnp.concatenate(
            [jnp.where(head_lanes, q, zero), jnp.where(bias_lanes, qb, zero)], axis=1))
    return out


def _attn_body(q_ref, qb_ref, k_ref, kb_ref, v_ref, sg_ref, o_ref, m_ref, l_ref, acc_ref,
               *, tq, tk, n_qblocks):
    i = pl.program_id(2)
    lane = lax.broadcasted_iota(jnp.int32, (tq, LANES), 1)
    q_aug = _q_aug(q_ref[...], qb_ref[...], lane)

    m_ref[...] = jnp.full(m_ref.shape, NEG_INF, F32)
    l_ref[...] = jnp.zeros(l_ref.shape, F32)
    acc_ref[...] = jnp.zeros(acc_ref.shape, F32)

    def qk_scores(j, qwin=slice(None)):
        rows = slice(j * tk, (j + 1) * tk)
        k_aug = jnp.concatenate([k_ref[rows, :], kb_ref[rows, :]], axis=1)
        qs = q_aug if qwin == slice(None) else [qa[qwin] for qa in q_aug]
        return [lax.dot_general(k_aug, qs[e], NT_DIMS, preferred_element_type=F32)
                for e in range(2)]

    def softmax_pv(j, scores, qwin=slice(None)):
        v = v_ref[j]
        if qwin == slice(None):
            at = lambda ref, e: ref.at[e]
        else:
            at = lambda ref, e: ref.at[e, :, qwin]
        for e in range(2):
            s = scores[e]
            m_prev = at(m_ref, e)[...]
            m_new = jnp.maximum(m_prev, jnp.max(s, axis=0, keepdims=True))
            alpha = jnp.exp2(m_prev - m_new)
            p = jnp.exp2(s - m_new)
            at(l_ref, e)[...] = alpha * at(l_ref, e)[...] + jnp.sum(p, axis=0, keepdims=True)
            p16 = p.astype(BF16)
            at(acc_ref, e)[...] = alpha * at(acc_ref, e)[...] + _dot(v, p16)
            at(m_ref, e)[...] = m_new

    early, late = slice(0, tk), slice(tk, tq)

    def schedule(n_visible):
        ahead = qk_scores(0)
        for j in range(n_visible):
            cur, ahead = ahead, qk_scores(j + 1)
            softmax_pv(j, cur)
        tmask = jnp.where(lax.broadcasted_iota(jnp.int32, (tk, tk), 0)
                          <= lax.broadcasted_iota(jnp.int32, (tk, tk), 1), 0.0, NEG_INF)
        late_scores = qk_scores(n_visible + 1, late)
        softmax_pv(n_visible, [jnp.concatenate([s[:, early] + tmask, s[:, late]], axis=1)
                               for s in ahead])
        softmax_pv(n_visible + 1, [s + tmask for s in late_scores], late)

    lax.switch(i, [functools.partial(schedule, 2 * n) for n in range(n_qblocks)])

    o0 = acc_ref[0] * (1.0 / l_ref[0])
    o1 = acc_ref[1] * (1.0 / l_ref[1])
    feature = lax.broadcasted_iota(jnp.int32, (LANES, tq), 0)
    o = jnp.where((feature >> 6) == 0, o0, o1).T
    o_ref[...] = (o * sg_ref[...].astype(F32)).astype(BF16)


def _fox_attn(q16, qb, k16, kb, vt16, sg, *, tq, tk):
    bsz, t, d = q16.shape
    assert t % tq == 0 and tq == 2 * tk and k16.shape[2] == t
    qspec = pl.BlockSpec((None, tq, LANES), lambda b, p, i: (b, i, p))
    vmem = 2 * (3 * t * LANES * 2 + tq * LANES * (2 + 2 + 2 + 2)) + 3 * tq * LANES * 4 \
        + 2 * tk * 2 * LANES * 2 + 10 * tq * tk * 4
    return pl.pallas_call(
        functools.partial(_attn_body, tq=tq, tk=tk, n_qblocks=t // tq),
        out_shape=jax.ShapeDtypeStruct((bsz, t, d), BF16),
        grid=(bsz, FOX_PAIRS, t // tq),
        in_specs=[
            qspec,
            pl.BlockSpec((None, None, tq, LANES), lambda b, p, i: (b, p, i, 0)),
            pl.BlockSpec((None, None, t, LANES), lambda b, p, i: (p, b, 0, 0)),
            pl.BlockSpec((None, None, t, LANES), lambda b, p, i: (b, p, 0, 0)),
            pl.BlockSpec((t // tk, LANES, tk), lambda b, p, i: (b, p, 0)),
            qspec,
        ],
        out_specs=qspec,
        scratch_shapes=[pltpu.VMEM((2, 1, tq), F32), pltpu.VMEM((2, 1, tq), F32),
                        pltpu.VMEM((2, LANES, tq), F32)],
        compiler_params=_cparams(("parallel", "parallel", "arbitrary"), vmem + (8 << 20)),
        name="fox_attn",
    )(q16, qb, k16, kb, vt16, sg)


def _cache_attn_body(q_ref, qb_ref, ck_ref, cv_ref, kn_ref, vn_ref, kb_ref, sg_ref,
                     o_ref, m_ref, l_ref, acc_ref, *, tk):
    t = q_ref.shape[0]
    p_len = ck_ref.shape[0]
    lane = lax.broadcasted_iota(jnp.int32, (t, LANES), 1)
    q_aug = _q_aug(q_ref[...], qb_ref[...], lane)
    m_ref[...] = jnp.full(m_ref.shape, NEG_INF, F32)
    l_ref[...] = jnp.zeros(l_ref.shape, F32)
    acc_ref[...] = jnp.zeros(acc_ref.shape, F32)

    def fold(k, kb, v, mask):
        k_aug = jnp.concatenate([k, kb], axis=1)
        scores = [lax.dot_general(q_aug[e], k_aug, NT_DIMS, preferred_element_type=F32)
                  for e in range(2)]
        for e in range(2):
            s = scores[e] if mask is None else scores[e] + mask
            m_prev = m_ref[e]
            m_new = jnp.maximum(m_prev, jnp.max(s, axis=1, keepdims=True))
            alpha = jnp.exp2(m_prev - m_new)
            p = jnp.exp2(s - m_new)
            l_ref[e] = alpha * l_ref[e] + jnp.sum(p, axis=1, keepdims=True)
            acc_ref[e] = alpha * acc_ref[e] + _dot(p.astype(BF16), v)
            m_ref[e] = m_new

    for j in range(p_len // tk):
        rows = slice(j * tk, (j + 1) * tk)
        fold(ck_ref[rows, :], kb_ref[rows, :], cv_ref[rows, :], None)
    causal = jnp.where(lax.broadcasted_iota(jnp.int32, (t, t), 1)
                       <= lax.broadcasted_iota(jnp.int32, (t, t), 0), 0.0, NEG_INF)
    fold(kn_ref[...], kb_ref[p_len:p_len + t, :], vn_ref[...], causal)

    o = jnp.where((lane >> 6) == 0, acc_ref[0] * (1.0 / l_ref[0]), acc_ref[1] * (1.0 / l_ref[1]))
    o_ref[...] = (o * sg_ref[...].astype(F32)).astype(BF16)


def _fox_attn_cached(q16, qb, kb, ck16, cv16, k16, v16, sg, *, tk):
    bsz, t, d = q16.shape
    p_len = ck16.shape[2]
    assert p_len % tk == 0 and p_len % t == 0 and kb.shape[2] == p_len + t
    qspec = pl.BlockSpec((None, t, LANES), lambda b, p: (b, 0, p))
    cspec = pl.BlockSpec((None, None, p_len, LANES), lambda b, p: (p, b, 0, 0))
    nspec = pl.BlockSpec((None, t, LANES), lambda b, p: (p, b, 0))
    vmem = 2 * (3 * (p_len + t) * LANES * 2 + t * LANES * 8) + 6 * t * LANES * 4 + 8 * t * tk * 4 \
        + 4 * tk * LANES * 2
    return pl.pallas_call(
        functools.partial(_cache_attn_body, tk=tk),
        out_shape=jax.ShapeDtypeStruct((bsz, t, d), BF16),
        grid=(bsz, FOX_PAIRS),
        in_specs=[
            qspec,
            pl.BlockSpec((None, None, t, LANES), lambda b, p: (b, p, p_len // t, 0)),
            cspec, cspec, nspec, nspec,
            pl.BlockSpec((None, None, p_len + t, LANES), lambda b, p: (b, p, 0, 0)),
            qspec,
        ],
        out_specs=qspec,
        scratch_shapes=[pltpu.VMEM((2, t, 1), F32), pltpu.VMEM((2, t, 1), F32),
                        pltpu.VMEM((2, t, LANES), F32)],
        compiler_params=_cparams(("parallel", "parallel"), vmem + (8 << 20)),
        name="fox_attn_cached",
    )(q16, qb, ck16, cv16, k16, v16, kb, sg)


TM = 1024
KV_TM = 512
FFN_TM = 1024
FFN_TF = 256
REPACK_ROWS = 128
DOWN_REPACK_ROWS = 704
GLA_TB = 512
ATTN_TQ = 1024
ATTN_TK = 512
CACHE_TK = 4096
BIAS_TB_CHOICES = (832, 512, 384, 320, 256, 192, 128, 64)


def _prep_weights(ffn_norm, w_ffn_gu, w_ffn_down, mix_norm, a_w_in, a_w_g2, a_b_g, a_g_out, a_w_o,
                  kv_norm, w_kvf, b_f, g_k, b_w_qg, b_g_q, b_w_o):
    d = w_ffn_gu.shape[2]
    rank = a_w_g2.shape[1]
    qk = a_w_g2.shape[2]
    vw = a_w_o.shape[1]
    w_in = a_w_in[0]
    pad_cols = lambda w: jnp.pad(w, ((0, 0), (0, LANES - w.shape[1])))
    groups = np.arange(d) // FOX_HEAD_DIM
    gsum = (groups[:, None] == np.arange(LANES)[None, :]).astype(np.float32)
    return dict(
        ffn_norm=ffn_norm, mix_norm=mix_norm, kv_norm=kv_norm,
        w_gu=_tile_cols(w_ffn_gu.reshape((-1,) + w_ffn_gu.shape[2:]), FFN_TF, rows=REPACK_ROWS),
        w_down=_tile_cols(w_ffn_down.reshape((-1,) + w_ffn_down.shape[2:]), d,
                          rows=DOWN_REPACK_ROWS).reshape((-1,) + w_ffn_down.shape[2:]),
        a_wq=w_in[:, :qk].astype(BF16), a_wk=w_in[:, qk:2 * qk].astype(BF16),
        a_wv=w_in[:, 2 * qk:2 * qk + vw].astype(BF16),
        a_wr=w_in[:, 2 * qk + vw:2 * qk + 2 * vw].astype(BF16),
        a_wgl=pad_cols(w_in[:, 2 * qk + 2 * vw:]).astype(BF16),
        a_wg2=jnp.pad(a_w_g2[0], ((0, LANES - rank), (0, 0))).astype(BF16),
        a_bg=a_b_g[0], a_gout=a_g_out[0], a_wo=a_w_o[0].astype(BF16),
        kv_wk=w_kvf[:, :d].astype(BF16), kv_wv=w_kvf[:, d:2 * d].astype(BF16),
        kv_wv_t=w_kvf[:, d:2 * d].T.astype(BF16),
        kv_wf=pad_cols(w_kvf[:, 2 * d:]).astype(BF16),
        kv_bf=jnp.pad(b_f, (0, LANES - b_f.shape[0])).reshape(1, LANES),
        gk=jnp.tile(g_k, FOX_HEADS).reshape(1, d), gq=jnp.tile(b_g_q[0], FOX_HEADS).reshape(1, d),
        b_wq=b_w_qg[0][:, :d].astype(BF16), b_wg=b_w_qg[0][:, d:].astype(BF16),
        b_wo=b_w_o[0].astype(BF16),
        gsum=jnp.asarray(gsum, BF16), gexp=jnp.asarray(gsum.T, BF16),
    )


def _trunk(x, s0, past, w):
    bsz, t, d = x.shape
    n = bsz * t
    h = x.reshape(n, d)
    ffn = lambda h_, layer, half: _ffn(h_, w["ffn_norm"][layer, half], w["w_gu"], w["w_down"],
                                       2 * layer + half, tm=FFN_TM)
    h = ffn(h, 0, 0)
    q, k, v, r, la = _gla_in(h, w["mix_norm"][0], w["a_wq"], w["a_wk"], w["a_wv"], w["a_wr"],
                             w["a_wgl"], w["a_wg2"], w["a_bg"], tm=TM)
    s3 = lambda z: z.reshape(bsz, t, z.shape[1])
    chunk = 64
    og, st_fin = _gla(s3(q), s3(k), s3(v), s3(r), s3(la), jnp.swapaxes(s0, 2, 3), w["a_gout"],
                      chunk=chunk, tb=GLA_TB)
    h = _proj_res(h, og.reshape(n, -1), w["a_wo"], tm=TM)
    h = ffn(h, 0, 1)
    tq = min(ATTN_TQ, t)
    tk = min(ATTN_TK, tq // 2)
    k_new, v_new, lf, k16, v16 = _shared_kv(
        h, w["kv_norm"], w["kv_wk"], w["kv_wv_t"] if past is None else w["kv_wv"], w["kv_wf"],
        w["kv_bf"], w["gk"], w["gsum"], w["gexp"], tm=KV_TM, vt_block=tk if past is None else None)
    h = ffn(h, 1, 0)
    q16, sg = _fox_q(h, w["mix_norm"][1], w["b_wq"], w["b_wg"], w["gq"], w["gsum"], w["gexp"], tm=TM)
    pairs4 = lambda z: z.reshape(FOX_PAIRS, bsz, t, LANES)
    if past is None:
        lf_all = s3(lf)
    else:
        past_k, past_v, past_lf = past
        p_len = past_k.shape[1]
        past_pairs = lambda a: jnp.transpose(
            a.astype(BF16).reshape(bsz, p_len, FOX_PAIRS, LANES), (2, 0, 1, 3))
        lf_all = jnp.concatenate(
            [jnp.pad(past_lf, ((0, 0), (0, 0), (0, LANES - past_lf.shape[2]))), s3(lf)], axis=1)
    bias_tb = next(c for c in BIAS_TB_CHOICES if lf_all.shape[1] % c == 0)
    qb, kb = _fox_bias(lf_all, tb=bias_tb)
    if past is None:
        og = _fox_attn(s3(q16), qb, pairs4(k16), kb, v16, s3(sg), tq=tq, tk=tk)
    else:
        og = _fox_attn_cached(s3(q16), qb, kb, past_pairs(past_k), past_pairs(past_v), k16, v16,
                              s3(sg), tk=min(CACHE_TK, p_len))
    h = _proj_res(h, og.reshape(n, d), w["b_wo"], tm=TM)
    h = ffn(h, 1, 1)
    heads4 = lambda z: z.reshape(bsz, t, FOX_HEADS, FOX_HEAD_DIM)
    return (h.reshape(bsz, t, d), jnp.swapaxes(st_fin, 2, 3)[:, None], heads4(k_new), heads4(v_new),
            lf[:, :FOX_HEADS].reshape(bsz, t, FOX_HEADS))


def kernel(x_prompt, x_sample, state_gla, cache_k, cache_v, cache_logf, ffn_norm, w_ffn_gu, w_ffn_down, mix_norm, a_w_in, a_w_g2, a_b_g, a_g_out, a_w_o, kv_norm, w_kvf, b_f, g_k, b_w_qg, b_g_q, b_w_o):
    w = _prep_weights(ffn_norm, w_ffn_gu, w_ffn_down, mix_norm, a_w_in, a_w_g2, a_b_g, a_g_out,
                      a_w_o, kv_norm, w_kvf, b_f, g_k, b_w_qg, b_g_q, b_w_o)
    s0_prompt = jnp.zeros((x_prompt.shape[0],) + state_gla.shape[2:], F32)
    y_p, gla_p, k_p, v_p, lf_p = _trunk(x_prompt, s0_prompt, None, w)
    y_s, gla_s, k_s, v_s, lf_s = _trunk(x_sample, state_gla[:, 0], (cache_k, cache_v, cache_logf), w)
    return (y_p, y_s, gla_p, gla_s, k_p, v_p, lf_p, k_s, v_s, lf_s)
```

```python
import functools

import jax
import jax.numpy as jnp
import numpy as np
from jax import lax
from jax.experimental import pallas as pl
from jax.experimental.pallas import tpu as pltpu

F32 = jnp.float32
BF16 = jnp.bfloat16

EPS = 1e-6
NEG_INF = -1e30
LOG2E = 1.4426950408889634

LANES = 128
V7X_SCOPED_VMEM_CAP = 60000 * 1024

GLA_GATE_TAU = 16.0
FOX_HEADS = 16
FOX_HEAD_DIM = 64
FOX_PAIRS = FOX_HEADS // 2
BIAS_PARTS = 3
BIAS_LANES_PER_HEAD = 2 * BIAS_PARTS

NT_DIMS = (((1,), (1,)), ((), ()))
TN_DIMS = (((0,), (0,)), ((), ()))


def _cparams(semantics, vmem_bytes):
    limit = int(min(max(vmem_bytes, 16 * 1024 * 1024), V7X_SCOPED_VMEM_CAP))
    return pltpu.CompilerParams(dimension_semantics=semantics, vmem_limit_bytes=limit)


def _dot(a, b):
    return jnp.dot(a, b, preferred_element_type=F32)


def _rms(x, g):
    ms = jnp.mean(x * x, axis=-1, keepdims=True)
    return x * lax.rsqrt(ms + EPS) * g


def _split2(x):
    hi = x.astype(BF16)
    lo = (x - hi.astype(F32)).astype(BF16)
    return hi, lo


def _split3(x):
    p1 = x.astype(BF16)
    r1 = x - p1.astype(F32)
    p2 = r1.astype(BF16)
    p3 = (r1 - p2.astype(F32)).astype(BF16)
    return p1, p2, p3


def _log_sigmoid(x):
    return jnp.minimum(x, 0.0) - jnp.log(1.0 + jnp.exp(-jnp.abs(x)))


def _head_norm(x, gsum_ref, gexp_ref, gain, head_dim):
    ssq = _dot((x * x).astype(BF16), gsum_ref[...])
    inv = lax.rsqrt(ssq * (1.0 / head_dim) + EPS)
    ihi, ilo = _split2(inv)
    inv_full = _dot(ihi, gexp_ref[...]) + _dot(ilo, gexp_ref[...])
    return x * inv_full * gain


def _ffn_body(x_ref, g_ref, wg_ref, wu_ref, wd_ref, o_ref, xn_ref):
    j = pl.program_id(1)

    @pl.when(j == 0)
    def _():
        x = x_ref[...]
        xn_ref[...] = _rms(x, g_ref[...]).astype(BF16)
        o_ref[...] = x

    xn = xn_ref[...]
    g = _dot(xn, wg_ref[...])
    u = _dot(xn, wu_ref[...])
    h = (0.5 * (g * jax.nn.sigmoid(g) * u)).astype(BF16)
    o_ref[...] += _dot(h, wd_ref[...])


def _tile_cols_body(w_ref, o_ref):
    tf = o_ref.shape[2]
    for c in range(o_ref.shape[0]):
        o_ref[c] = w_ref[:, c * tf:(c + 1) * tf].astype(o_ref.dtype)


def _tile_cols(w, tf, *, rows):
    g, d, n = w.shape
    assert n % tf == 0 and d % rows == 0
    return pl.pallas_call(
        _tile_cols_body,
        out_shape=jax.ShapeDtypeStruct((g, n // tf, d, tf), BF16),
        grid=(g, d // rows),
        in_specs=[pl.BlockSpec((None, rows, n), lambda a, r: (a, r, 0))],
        out_specs=pl.BlockSpec((None, n // tf, rows, tf), lambda a, r: (a, 0, r, 0)),
        compiler_params=_cparams(("parallel", "parallel"), 2 * rows * n * (4 + 2) + (4 << 20)),
        name="tile_cols",
    )(w)


def _ffn(x, g, w_gu, w_down, *, tm):
    n, d = x.shape
    d_ff = w_down.shape[0]
    tf = w_gu.shape[2]
    tm = min(tm, n)
    nf = d_ff // tf
    assert n % tm == 0 and w_gu.shape[0] == 2 * nf
    vmem = 4 * tm * d * 4 + tm * d * 2 + 2 * 3 * d * tf * 2 + 3 * tm * tf * 4 + tm * d * 4
    return pl.pallas_call(
        _ffn_body,
        out_shape=jax.ShapeDtypeStruct((n, d), F32),
        grid=(n // tm, nf),
        in_specs=[
            pl.BlockSpec((tm, d), lambda i, j: (i, 0)),
            pl.BlockSpec((1, d), lambda i, j: (0, 0)),
            pl.BlockSpec((None, d, tf), lambda i, j: (j, 0, 0)),
            pl.BlockSpec((None, d, tf), lambda i, j: (j + nf, 0, 0)),
            pl.BlockSpec((tf, d), lambda i, j: (j, 0)),
        ],
        out_specs=pl.BlockSpec((tm, d), lambda i, j: (i, 0)),
        scratch_shapes=[pltpu.VMEM((tm, d), BF16)],
        compiler_params=_cparams(("parallel", "arbitrary"), vmem + (8 << 20)),
        name="ffn",
    )(x, g.reshape(1, d), w_gu, w_gu, w_down)


def _proj_res_body(h_ref, a_ref, w_ref, o_ref):
    o_ref[...] = h_ref[...] + _dot(a_ref[...], w_ref[...])


def _proj_res(h, a, w, *, tm):
    n, d = h.shape
    k = a.shape[1]
    tm = min(tm, n)
    assert n % tm == 0
    vmem = 2 * (2 * tm * d * 4 + tm * k * 2 + k * d * 2) + tm * d * 4
    return pl.pallas_call(
        _proj_res_body,
        out_shape=jax.ShapeDtypeStruct((n, d), F32),
        grid=(n // tm,),
        in_specs=[
            pl.BlockSpec((tm, d), lambda i: (i, 0)),
            pl.BlockSpec((tm, k), lambda i: (i, 0)),
            pl.BlockSpec((k, d), lambda i: (0, 0)),
        ],
        out_specs=pl.BlockSpec((tm, d), lambda i: (i, 0)),
        compiler_params=_cparams(("parallel",), vmem + (4 << 20)),
        name="proj_res",
    )(h, a, w)


def _gla_in_body(x_ref, g_ref, wq_ref, wk_ref, wv_ref, wr_ref, wgl_ref, wg2_ref, bg_ref,
                 q_ref, k_ref, v_ref, r_ref, la_ref):
    xn = _rms(x_ref[...], g_ref[...]).astype(BF16)
    q_ref[...] = _dot(xn, wq_ref[...]).astype(q_ref.dtype)
    k_ref[...] = _dot(xn, wk_ref[...]).astype(k_ref.dtype)
    v_ref[...] = _dot(xn, wv_ref[...]).astype(v_ref.dtype)
    r_ref[...] = _dot(xn, wr_ref[...]).astype(r_ref.dtype)
    gl = _dot(xn, wgl_ref[...]).astype(BF16)
    z = _dot(gl, wg2_ref[...]) + bg_ref[...]
    la_ref[...] = _log_sigmoid(z) * (1.0 / GLA_GATE_TAU)


def _gla_in(x, g, wq, wk, wv, wr, wgl, wg2, bg, *, tm):
    n, d = x.shape
    qk, vw = wq.shape[1], wv.shape[1]
    tm = min(tm, n)
    assert n % tm == 0
    row = lambda i: (i, 0)
    fix = lambda i: (0, 0)
    w_bytes = 2 * (2 * d * qk + 2 * d * vw + d * LANES + LANES * qk)
    vmem = 2 * tm * (d + 3 * qk + 2 * vw) * 4 + 2 * w_bytes + tm * d * 2
    return pl.pallas_call(
        _gla_in_body,
        out_shape=[jax.ShapeDtypeStruct((n, qk), BF16), jax.ShapeDtypeStruct((n, qk), BF16),
                   jax.ShapeDtypeStruct((n, vw), BF16), jax.ShapeDtypeStruct((n, vw), BF16),
                   jax.ShapeDtypeStruct((n, qk), F32)],
        grid=(n // tm,),
        in_specs=[
            pl.BlockSpec((tm, d), row), pl.BlockSpec((1, d), fix),
            pl.BlockSpec((d, qk), fix), pl.BlockSpec((d, qk), fix),
            pl.BlockSpec((d, vw), fix), pl.BlockSpec((d, vw), fix),
            pl.BlockSpec((d, LANES), fix), pl.BlockSpec((LANES, qk), fix), pl.BlockSpec((1, qk), fix),
        ],
        out_specs=[pl.BlockSpec((tm, qk), row), pl.BlockSpec((tm, qk), row),
                   pl.BlockSpec((tm, vw), row), pl.BlockSpec((tm, vw), row),
                   pl.BlockSpec((tm, qk), row)],
        compiler_params=_cparams(("parallel",), vmem + (8 << 20)),
        name="gla_in",
    )(x, g.reshape(1, d), wq, wk, wv, wr, wgl, wg2, bg.reshape(1, qk))


def _gla_body(q_ref, k_ref, v_ref, r_ref, la_ref, s0_ref, gout_ref, tri_ref,
              og_ref, sfin_ref, st_ref, *, chunk, n_chunks, heads, dk, dv):
    t = pl.program_id(1)

    @pl.when(t == 0)
    def _():
        st_ref[...] = s0_ref[...]

    scale = dk ** -0.5
    row = lax.broadcasted_iota(jnp.int32, (chunk, chunk), 0)
    col = lax.broadcasted_iota(jnp.int32, (chunk, chunk), 1)
    causal = col <= row
    tri = tri_ref[...]
    gout = gout_ref[...]

    for c in range(n_chunks):
        sl = slice(c * chunk, (c + 1) * chunk)
        la_hi, la_lo = _split2(la_ref[sl, :])
        b = _dot(tri, la_hi) + _dot(tri, la_lo)
        b_last = b[chunk - 1:chunk, :]
        q = q_ref[sl, :].astype(F32)
        k = k_ref[sl, :].astype(F32)
        qe = (q * scale * jnp.exp(b)).astype(BF16)
        ke = (k * jnp.exp(-b)).astype(BF16)
        kd = (k * jnp.exp(b_last - b)).astype(BF16)
        dec = jnp.exp(b_last)
        for h in range(heads):
            ks = slice(h * dk, (h + 1) * dk)
            vs = slice(h * dv, (h + 1) * dv)
            vh = v_ref[sl, vs]
            att = lax.dot_general(qe[:, ks], ke[:, ks], NT_DIMS, preferred_element_type=F32)
            att = jnp.where(causal, att, 0.0).astype(BF16)
            st = st_ref[h]
            o = _dot(att, vh) + lax.dot_general(qe[:, ks], st.astype(BF16), NT_DIMS,
                                                preferred_element_type=F32)
            st_ref[h] = st * dec[:, ks] + lax.dot_general(vh, kd[:, ks], TN_DIMS,
                                                          preferred_element_type=F32)
            on = _rms(o, gout)
            rh = r_ref[sl, vs].astype(F32)
            og_ref[sl, vs] = (on * (rh * jax.nn.sigmoid(rh))).astype(BF16)

    @pl.when(t == pl.num_programs(1) - 1)
    def _():
        sfin_ref[...] = st_ref[...]


def _gla(q, k, v, r, la, s0t, gout, *, chunk, tb):
    bsz, t, qk = q.shape
    vw = v.shape[2]
    heads = s0t.shape[1]
    dk, dv = qk // heads, vw // heads
    tb = min(tb, t)
    assert t % tb == 0 and tb % chunk == 0
    tri = jnp.tril(jnp.ones((chunk, chunk), F32)).astype(BF16)
    blk = lambda w: pl.BlockSpec((None, tb, w), lambda b, i: (b, i, 0))
    st_spec = pl.BlockSpec((None, heads, dv, dk), lambda b, i: (b, 0, 0, 0))
    vmem = 2 * tb * (3 * qk + 2 * vw) * 4 + 2 * tb * vw * 2 + 5 * heads * dv * dk * 4
    return pl.pallas_call(
        functools.partial(_gla_body, chunk=chunk, n_chunks=tb // chunk, heads=heads, dk=dk, dv=dv),
        out_shape=[jax.ShapeDtypeStruct((bsz, t, vw), BF16),
                   jax.ShapeDtypeStruct((bsz, heads, dv, dk), F32)],
        grid=(bsz, t // tb),
        in_specs=[blk(qk), blk(qk), blk(vw), blk(vw), blk(qk), st_spec,
                  pl.BlockSpec((1, dv), lambda b, i: (0, 0)),
                  pl.BlockSpec((chunk, chunk), lambda b, i: (0, 0))],
        out_specs=[blk(vw), st_spec],
        scratch_shapes=[pltpu.VMEM((heads, dv, dk), F32)],
        compiler_params=_cparams(("parallel", "arbitrary"), vmem + (8 << 20)),
        name="gla_chunks",
    )(q, k, v, r, la, s0t, gout.reshape(1, dv), tri)


def _store_pairs(ref, x):
    for p in range(ref.shape[0]):
        ref[p] = x[:, p * LANES:(p + 1) * LANES]


def _kv_body(x_ref, g_ref, wk_ref, wv_ref, wf_ref, bf_ref, gk_ref, gsum_ref, gexp_ref,
             k_ref, v_ref, lf_ref, k16_ref, v16_ref, *, vt_block):
    xn = _rms(x_ref[...], g_ref[...]).astype(BF16)
    k = _head_norm(_dot(xn, wk_ref[...]), gsum_ref, gexp_ref, gk_ref[...], FOX_HEAD_DIM)
    k_ref[...] = k.reshape(k_ref.shape)
    _store_pairs(k16_ref, k.astype(BF16))
    lf_ref[...] = _log_sigmoid(_dot(xn, wf_ref[...]) + bf_ref[...])
    if vt_block is None:
        v = _dot(xn, wv_ref[...])
        v_ref[...] = v.reshape(v_ref.shape)
        _store_pairs(v16_ref, v.astype(BF16))
    else:
        vt = lax.dot_general(wv_ref[...], xn, NT_DIMS, preferred_element_type=F32)
        v_ref[...] = vt.T.reshape(v_ref.shape)
        vt16 = vt.astype(BF16)
        for c in range(v16_ref.shape[0]):
            v16_ref[c] = vt16[:, c * vt_block:(c + 1) * vt_block]


def _shared_kv(x, g, wk, wv, wf, bf, gk, gsum, gexp, *, tm, vt_block=None):
    n, d = x.shape
    tm = min(tm, n)
    assert n % tm == 0
    row = lambda i: (i, 0)
    fix = lambda i: (0, 0)
    heads_shape = (n, FOX_HEADS, FOX_HEAD_DIM)
    heads_spec = pl.BlockSpec((tm, FOX_HEADS, FOX_HEAD_DIM), lambda i: (i, 0, 0))
    pair_shape = (d // LANES, n, LANES)
    pair_spec = pl.BlockSpec((d // LANES, tm, LANES), lambda i: (0, i, 0))
    if vt_block is None:
        v16_shape, v16_spec = pair_shape, pair_spec
    else:
        assert tm % vt_block == 0
        v16_shape = (n // vt_block, d, vt_block)
        v16_spec = pl.BlockSpec((tm // vt_block, d, vt_block), lambda i: (i, 0, 0))
    vmem = 2 * tm * d * (4 + 8 + 8 + 2 + 2) + 2 * tm * LANES * 4 + 2 * 2 * (2 * d * d + 3 * d * LANES) \
        + 8 * tm * d * 4
    return pl.pallas_call(
        functools.partial(_kv_body, vt_block=vt_block),
        out_shape=[jax.ShapeDtypeStruct(heads_shape, F32), jax.ShapeDtypeStruct(heads_shape, F32),
                   jax.ShapeDtypeStruct((n, LANES), F32),
                   jax.ShapeDtypeStruct(pair_shape, BF16), jax.ShapeDtypeStruct(v16_shape, BF16)],
        grid=(n // tm,),
        in_specs=[
            pl.BlockSpec((tm, d), row), pl.BlockSpec((1, d), fix),
            pl.BlockSpec((d, d), fix), pl.BlockSpec((d, d), fix), pl.BlockSpec((d, LANES), fix),
            pl.BlockSpec((1, LANES), fix), pl.BlockSpec((1, d), fix),
            pl.BlockSpec((d, LANES), fix), pl.BlockSpec((LANES, d), fix),
        ],
        out_specs=[heads_spec, heads_spec,
                   pl.BlockSpec((tm, LANES), row),
                   pair_spec, v16_spec],
        compiler_params=_cparams(("parallel",), vmem),
        name="shared_kv",
    )(x, g.reshape(1, d), wk, wv, wf, bf, gk, gsum, gexp)


def _fox_q_body(x_ref, g_ref, wq_ref, wg_ref, gq_ref, gsum_ref, gexp_ref, q16_ref, sg_ref):
    xn = _rms(x_ref[...], g_ref[...]).astype(BF16)
    q = _head_norm(_dot(xn, wq_ref[...]), gsum_ref, gexp_ref, gq_ref[...], FOX_HEAD_DIM)
    q16_ref[...] = (q * (FOX_HEAD_DIM ** -0.5 * LOG2E)).astype(BF16)
    sg_ref[...] = jax.nn.sigmoid(_dot(xn, wg_ref[...])).astype(sg_ref.dtype)


def _fox_q(x, g, wq, wg, gq, gsum, gexp, *, tm):
    n, d = x.shape
    tm = min(tm, n)
    assert n % tm == 0
    row = lambda i: (i, 0)
    fix = lambda i: (0, 0)
    vmem = 2 * tm * d * (4 + 2 + 4) + 2 * 2 * (2 * d * d + 2 * d * LANES) + 6 * tm * d * 4
    return pl.pallas_call(
        _fox_q_body,
        out_shape=[jax.ShapeDtypeStruct((n, d), BF16), jax.ShapeDtypeStruct((n, d), BF16)],
        grid=(n // tm,),
        in_specs=[
            pl.BlockSpec((tm, d), row), pl.BlockSpec((1, d), fix),
            pl.BlockSpec((d, d), fix), pl.BlockSpec((d, d), fix), pl.BlockSpec((1, d), fix),
            pl.BlockSpec((d, LANES), fix), pl.BlockSpec((LANES, d), fix),
        ],
        out_specs=[pl.BlockSpec((tm, d), row), pl.BlockSpec((tm, d), row)],
        compiler_params=_cparams(("parallel",), vmem),
        name="fox_q",
    )(x, g.reshape(1, d), wq, wg, gq, gsum, gexp)


def _pack_parts(x):
    p1, p2, p3 = (p.astype(F32) for p in _split3(x))
    packed = p1 + pltpu.roll(p2, FOX_HEADS, 1) + pltpu.roll(p3, 2 * FOX_HEADS, 1)
    return packed.astype(BF16)


def _bias_body(lf_ref, tri_ref, sel_ref, one_ref, qb_ref, kb_ref, carry_ref):
    @pl.when(pl.program_id(1) == 0)
    def _():
        carry_ref[...] = jnp.zeros_like(carry_ref)

    lf = lf_ref[...]
    tb = lf.shape[0]
    head_lanes = lax.broadcasted_iota(jnp.int32, lf.shape, 1) < FOX_HEADS
    cp = _dot(tri_ref[...], _pack_parts(jnp.where(head_lanes, lf, 0.0)))
    c = cp + pltpu.roll(cp, LANES - FOX_HEADS, 1) + pltpu.roll(cp, LANES - 2 * FOX_HEADS, 1)
    c = jnp.where(head_lanes, c, 0.0) + carry_ref[...]
    carry_ref[...] = c[tb - 1:tb, :]
    qkb = _dot(_pack_parts(c * LOG2E), sel_ref[...]) + one_ref[...]
    wide = FOX_PAIRS * LANES
    for p in range(FOX_PAIRS):
        qb_ref[p] = qkb[:, p * LANES:(p + 1) * LANES].astype(BF16)
        kb_ref[p] = qkb[:, wide + p * LANES:wide + (p + 1) * LANES].astype(BF16)


def _bias_constants():
    wide = FOX_PAIRS * LANES
    sel = np.zeros((LANES, 2 * wide), np.float32)
    one = np.zeros((1, 2 * wide), np.float32)
    for p in range(FOX_PAIRS):
        for e in range(2):
            base = p * LANES + e * BIAS_LANES_PER_HEAD
            for part in range(BIAS_PARTS):
                row = part * FOX_HEADS + 2 * p + e
                sel[row, base + part] = 1.0
                one[0, base + BIAS_PARTS + part] = 1.0
                one[0, wide + base + part] = 1.0
                sel[row, wide + base + BIAS_PARTS + part] = -1.0
    return jnp.asarray(sel, BF16), jnp.asarray(one)


def _fox_bias(lf, *, tb):
    bsz, kp, _ = lf.shape
    assert kp % tb == 0
    sel, one = _bias_constants()
    tri = jnp.tril(jnp.ones((tb, tb), F32)).astype(BF16)
    wide = FOX_PAIRS * LANES
    out_spec = pl.BlockSpec((None, FOX_PAIRS, tb, LANES), lambda b, i: (b, 0, i, 0))
    vmem = 2 * tb * LANES * 4 + 2 * tb * tb * 2 + 2 * LANES * 2 * wide * 2 \
        + 4 * FOX_PAIRS * tb * LANES * 2 + 4 * tb * 2 * wide * 4
    return pl.pallas_call(
        _bias_body,
        out_shape=[jax.ShapeDtypeStruct((bsz, FOX_PAIRS, kp, LANES), BF16)] * 2,
        grid=(bsz, kp // tb),
        in_specs=[
            pl.BlockSpec((None, tb, LANES), lambda b, i: (b, i, 0)),
            pl.BlockSpec((tb, tb), lambda b, i: (0, 0)),
            pl.BlockSpec((LANES, 2 * wide), lambda b, i: (0, 0)),
            pl.BlockSpec((1, 2 * wide), lambda b, i: (0, 0)),
        ],
        out_specs=[out_spec, out_spec],
        scratch_shapes=[pltpu.VMEM((1, LANES), F32)],
        compiler_params=_cparams(("parallel", "arbitrary"), vmem),
        name="fox_bias",
    )(lf, tri, sel, one)


def _q_aug(q, qb, lane):
    zero = jnp.zeros_like(q)
    out = []
    for e in range(2):
        head_lanes = (lane >> 6) == e
        bias_lanes = jnp.logical_and(lane >= e * BIAS_LANES_PER_HEAD,
                                     lane < (e + 1) * BIAS_LANES_PER_HEAD)
        out.append(jnp.concatenate(
            [jnp.where(head_lanes, q, zero), jnp.where(bias_lanes, qb, zero)], axis=1))
    return out


def _attn_body(q_ref, qb_ref, k_ref, kb_ref, v_ref, sg_ref, o_ref, m_ref, l_ref, acc_ref,
               *, tq, tk, n_qblocks):
    i = pl.program_id(2)
    lane = lax.broadcasted_iota(jnp.int32, (tq, LANES), 1)
    q_aug = _q_aug(q_ref[...], qb_ref[...], lane)

    m_ref[...] = jnp.full(m_ref.shape, NEG_INF, F32)
    l_ref[...] = jnp.zeros(l_ref.shape, F32)
    acc_ref[...] = jnp.zeros(acc_ref.shape, F32)

    def qk_scores(j, qwin=slice(None)):
        rows = slice(j * tk, (j + 1) * tk)
        k_aug = jnp.concatenate([k_ref[rows, :], kb_ref[rows, :]], axis=1)
        qs = q_aug if qwin == slice(None) else [qa[qwin] for qa in q_aug]
        return [lax.dot_general(k_aug, qs[e], NT_DIMS, preferred_element_type=F32)
                for e in range(2)]

    def softmax_pv(j, scores, qwin=slice(None)):
        v = v_ref[j]
        if qwin == slice(None):
            at = lambda ref, e: ref.at[e]
        else:
            at = lambda ref, e: ref.at[e, :, qwin]
        for e in range(2):
            s = scores[e]
            m_prev = at(m_ref, e)[...]
            m_new = jnp.maximum(m_prev, jnp.max(s, axis=0, keepdims=True))
            alpha = jnp.exp2(m_prev - m_new)
            p = jnp.exp2(s - m_new)
            at(l_ref, e)[...] = alpha * at(l_ref, e)[...] + jnp.sum(p, axis=0, keepdims=True)
            p16 = p.astype(BF16)
            at(acc_ref, e)[...] = alpha * at(acc_ref, e)[...] + _dot(v, p16)
            at(m_ref, e)[...] = m_new

    early, late = slice(0, tk), slice(tk, tq)

    def schedule(n_visible):
        ahead = qk_scores(0)
        for j in range(n_visible):
            cur, ahead = ahead, qk_scores(j + 1)
            softmax_pv(j, cur)
        tmask = jnp.where(lax.broadcasted_iota(jnp.int32, (tk, tk), 0)
                          <= lax.broadcasted_iota(jnp.int32, (tk, tk), 1), 0.0, NEG_INF)
        late_scores = qk_scores(n_visible + 1, late)
        softmax_pv(n_visible, [jnp.concatenate([s[:, early] + tmask, s[:, late]], axis=1)
                               for s in ahead])
        softmax_pv(n_visible + 1, [s + tmask for s in late_scores], late)

    lax.switch(i, [functools.partial(schedule, 2 * n) for n in range(n_qblocks)])

    o0 = acc_ref[0] * (1.0 / l_ref[0])
    o1 = acc_ref[1] * (1.0 / l_ref[1])
    feature = lax.broadcasted_iota(jnp.int32, (LANES, tq), 0)
    o = jnp.where((feature >> 6) == 0, o0, o1).T
    o_ref[...] = (o * sg_ref[...].astype(F32)).astype(BF16)


def _fox_attn(q16, qb, k16, kb, vt16, sg, *, tq, tk):
    bsz, t, d = q16.shape
    assert t % tq == 0 and tq == 2 * tk and k16.shape[2] == t
    qspec = pl.BlockSpec((None, tq, LANES), lambda b, p, i: (b, i, p))
    vmem = 2 * (3 * t * LANES * 2 + tq * LANES * (2 + 2 + 2 + 2)) + 3 * tq * LANES * 4 \
        + 2 * tk * 2 * LANES * 2 + 10 * tq * tk * 4
    return pl.pallas_call(
        functools.partial(_attn_body, tq=tq, tk=tk, n_qblocks=t // tq),
        out_shape=jax.ShapeDtypeStruct((bsz, t, d), BF16),
        grid=(bsz, FOX_PAIRS, t // tq),
        in_specs=[
            qspec,
            pl.BlockSpec((None, None, tq, LANES), lambda b, p, i: (b, p, i, 0)),
            pl.BlockSpec((None, None, t, LANES), lambda b, p, i: (p, b, 0, 0)),
            pl.BlockSpec((None, None, t, LANES), lambda b, p, i: (b, p, 0, 0)),
            pl.BlockSpec((t // tk, LANES, tk), lambda b, p, i: (b, p, 0)),
            qspec,
        ],
        out_specs=qspec,
        scratch_shapes=[pltpu.VMEM((2, 1, tq), F32), pltpu.VMEM((2, 1, tq), F32),
                        pltpu.VMEM((2, LANES, tq), F32)],
        compiler_params=_cparams(("parallel", "parallel", "arbitrary"), vmem + (8 << 20)),
        name="fox_attn",
    )(q16, qb, k16, kb, vt16, sg)


def _cache_attn_body(q_ref, qb_ref, ck_ref, cv_ref, kn_ref, vn_ref, kb_ref, sg_ref,
                     o_ref, m_ref, l_ref, acc_ref, *, tk):
    t = q_ref.shape[0]
    p_len = ck_ref.shape[0]
    lane = lax.broadcasted_iota(jnp.int32, (t, LANES), 1)
    q_aug = _q_aug(q_ref[...], qb_ref[...], lane)
    m_ref[...] = jnp.full(m_ref.shape, NEG_INF, F32)
    l_ref[...] = jnp.zeros(l_ref.shape, F32)
    acc_ref[...] = jnp.zeros(acc_ref.shape, F32)

    def fold(k, kb, v, mask):
        k_aug = jnp.concatenate([k, kb], axis=1)
        scores = [lax.dot_general(q_aug[e], k_aug, NT_DIMS, preferred_element_type=F32)
                  for e in range(2)]
        for e in range(2):
            s = scores[e] if mask is None else scores[e] + mask
            m_prev = m_ref[e]
            m_new = jnp.maximum(m_prev, jnp.max(s, axis=1, keepdims=True))
            alpha = jnp.exp2(m_prev - m_new)
            p = jnp.exp2(s - m_new)
            l_ref[e] = alpha * l_ref[e] + jnp.sum(p, axis=1, keepdims=True)
            acc_ref[e] = alpha * acc_ref[e] + _dot(p.astype(BF16), v)
            m_ref[e] = m_new

    for j in range(p_len // tk):
        rows = slice(j * tk, (j + 1) * tk)
        fold(ck_ref[rows, :], kb_ref[rows, :], cv_ref[rows, :], None)
    causal = jnp.where(lax.broadcasted_iota(jnp.int32, (t, t), 1)
                       <= lax.broadcasted_iota(jnp.int32, (t, t), 0), 0.0, NEG_INF)
    fold(kn_ref[...], kb_ref[p_len:p_len + t, :], vn_ref[...], causal)

    o = jnp.where((lane >> 6) == 0, acc_ref[0] * (1.0 / l_ref[0]), acc_ref[1] * (1.0 / l_ref[1]))
    o_ref[...] = (o * sg_ref[...].astype(F32)).astype(BF16)


def _fox_attn_cached(q16, qb, kb, ck16, cv16, k16, v16, sg, *, tk):
    bsz, t, d = q16.shape
    p_len = ck16.shape[2]
    assert p_len % tk == 0 and p_len % t == 0 and kb.shape[2] == p_len + t
    qspec = pl.BlockSpec((None, t, LANES), lambda b, p: (b, 0, p))
    cspec = pl.BlockSpec((None, None, p_len, LANES), lambda b, p: (p, b, 0, 0))
    nspec = pl.BlockSpec((None, t, LANES), lambda b, p: (p, b, 0))
    vmem = 2 * (3 * (p_len + t) * LANES * 2 + t * LANES * 8) + 6 * t * LANES * 4 + 8 * t * tk * 4 \
        + 4 * tk * LANES * 2
    return pl.pallas_call(
        functools.partial(_cache_attn_body, tk=tk),
        out_shape=jax.ShapeDtypeStruct((bsz, t, d), BF16),
        grid=(bsz, FOX_PAIRS),
        in_specs=[
            qspec,
            pl.BlockSpec((None, None, t, LANES), lambda b, p: (b, p, p_len // t, 0)),
            cspec, cspec, nspec, nspec,
            pl.BlockSpec((None, None, p_len + t, LANES), lambda b, p: (b, p, 0, 0)),
            qspec,
        ],
        out_specs=qspec,
        scratch_shapes=[pltpu.VMEM((2, t, 1), F32), pltpu.VMEM((2, t, 1), F32),
                        pltpu.VMEM((2, t, LANES), F32)],
        compiler_params=_cparams(("parallel", "parallel"), vmem + (8 << 20)),
        name="fox_attn_cached",
    )(q16, qb, ck16, cv16, k16, v16, kb, sg)


TM = 1024
KV_TM = 512
FFN_TM = 1024
FFN_TF = 256
REPACK_ROWS = 128
GLA_TB = 512
ATTN_TQ = 1024
ATTN_TK = 512
CACHE_TK = 4096
BIAS_TB_CHOICES = (832, 512, 384, 320, 256, 192, 128, 64)


def _prep_weights(ffn_norm, w_ffn_gu, w_ffn_down, mix_norm, a_w_in, a_w_g2, a_b_g, a_g_out, a_w_o,
                  kv_norm, w_kvf, b_f, g_k, b_w_qg, b_g_q, b_w_o):
    d = w_ffn_gu.shape[2]
    rank = a_w_g2.shape[1]
    qk = a_w_g2.shape[2]
    vw = a_w_o.shape[1]
    w_in = a_w_in[0]
    pad_cols = lambda w: jnp.pad(w, ((0, 0), (0, LANES - w.shape[1])))
    groups = np.arange(d) // FOX_HEAD_DIM
    gsum = (groups[:, None] == np.arange(LANES)[None, :]).astype(np.float32)
    return dict(
        ffn_norm=ffn_norm, mix_norm=mix_norm, kv_norm=kv_norm,
        w_gu=_tile_cols(w_ffn_gu.reshape((-1,) + w_ffn_gu.shape[2:]), FFN_TF, rows=REPACK_ROWS),
        w_down=w_ffn_down.astype(BF16),
        a_wq=w_in[:, :qk].astype(BF16), a_wk=w_in[:, qk:2 * qk].astype(BF16),
        a_wv=w_in[:, 2 * qk:2 * qk + vw].astype(BF16),
        a_wr=w_in[:, 2 * qk + vw:2 * qk + 2 * vw].astype(BF16),
        a_wgl=pad_cols(w_in[:, 2 * qk + 2 * vw:]).astype(BF16),
        a_wg2=jnp.pad(a_w_g2[0], ((0, LANES - rank), (0, 0))).astype(BF16),
        a_bg=a_b_g[0], a_gout=a_g_out[0], a_wo=a_w_o[0].astype(BF16),
        kv_wk=w_kvf[:, :d].astype(BF16), kv_wv=w_kvf[:, d:2 * d].astype(BF16),
        kv_wv_t=w_kvf[:, d:2 * d].T.astype(BF16),
        kv_wf=pad_cols(w_kvf[:, 2 * d:]).astype(BF16),
        kv_bf=jnp.pad(b_f, (0, LANES - b_f.shape[0])).reshape(1, LANES),
        gk=jnp.tile(g_k, FOX_HEADS).reshape(1, d), gq=jnp.tile(b_g_q[0], FOX_HEADS).reshape(1, d),
        b_wq=b_w_qg[0][:, :d].astype(BF16), b_wg=b_w_qg[0][:, d:].astype(BF16),
        b_wo=b_w_o[0].astype(BF16),
        gsum=jnp.asarray(gsum, BF16), gexp=jnp.asarray(gsum.T, BF16),
    )


def _trunk(x, s0, past, w):
    bsz, t, d = x.shape
    n = bsz * t
    h = x.reshape(n, d)
    ffn = lambda h_, layer, half: _ffn(h_, w["ffn_norm"][layer, half], w["w_gu"][2 * layer + half],
                                       w["w_down"][layer, half], tm=FFN_TM)
    h = ffn(h, 0, 0)
    q, k, v, r, la = _gla_in(h, w["mix_norm"][0], w["a_wq"], w["a_wk"], w["a_wv"], w["a_wr"],
                             w["a_wgl"], w["a_wg2"], w["a_bg"], tm=TM)
    s3 = lambda z: z.reshape(bsz, t, z.shape[1])
    chunk = 64
    og, st_fin = _gla(s3(q), s3(k), s3(v), s3(r), s3(la), jnp.swapaxes(s0, 2, 3), w["a_gout"],
                      chunk=chunk, tb=GLA_TB)
    h = _proj_res(h, og.reshape(n, -1), w["a_wo"], tm=TM)
    h = ffn(h, 0, 1)
    tq = min(ATTN_TQ, t)
    tk = min(ATTN_TK, tq // 2)
    k_new, v_new, lf, k16, v16 = _shared_kv(
        h, w["kv_norm"], w["kv_wk"], w["kv_wv_t"] if past is None else w["kv_wv"], w["kv_wf"],
        w["kv_bf"], w["gk"], w["gsum"], w["gexp"], tm=KV_TM, vt_block=tk if past is None else None)
    h = ffn(h, 1, 0)
    q16, sg = _fox_q(h, w["mix_norm"][1], w["b_wq"], w["b_wg"], w["gq"], w["gsum"], w["gexp"], tm=TM)
    pairs4 = lambda z: z.reshape(FOX_PAIRS, bsz, t, LANES)
    if past is None:
        lf_all = s3(lf)
    else:
        past_k, past_v, past_lf = past
        p_len = past_k.shape[1]
        past_pairs = lambda a: jnp.transpose(
            a.astype(BF16).reshape(bsz, p_len, FOX_PAIRS, LANES), (2, 0, 1, 3))
        lf_all = jnp.concatenate(
            [jnp.pad(past_lf, ((0, 0), (0, 0), (0, LANES - past_lf.shape[2]))), s3(lf)], axis=1)
    bias_tb = next(c for c in BIAS_TB_CHOICES if lf_all.shape[1] % c == 0)
    qb, kb = _fox_bias(lf_all, tb=bias_tb)
    if past is None:
        og = _fox_attn(s3(q16), qb, pairs4(k16), kb, v16, s3(sg), tq=tq, tk=tk)
    else:
        og = _fox_attn_cached(s3(q16), qb, kb, past_pairs(past_k), past_pairs(past_v), k16, v16,
                              s3(sg), tk=min(CACHE_TK, p_len))
    h = _proj_res(h, og.reshape(n, d), w["b_wo"], tm=TM)
    h = ffn(h, 1, 1)
    heads4 = lambda z: z.reshape(bsz, t, FOX_HEADS, FOX_HEAD_DIM)
    return (h.reshape(bsz, t, d), jnp.swapaxes(st_fin, 2, 3)[:, None], heads4(k_new), heads4(v_new),
            lf[:, :FOX_HEADS].reshape(bsz, t, FOX_HEADS))


def kernel(x_prompt, x_sample, state_gla, cache_k, cache_v, cache_logf, ffn_norm, w_ffn_gu, w_ffn_down, mix_norm, a_w_in, a_w_g2, a_b_g, a_g_out, a_w_o, kv_norm, w_kvf, b_f, g_k, b_w_qg, b_g_q, b_w_o):
    w = _prep_weights(ffn_norm, w_ffn_gu, w_ffn_down, mix_norm, a_w_in, a_w_g2, a_b_g, a_g_out,
                      a_w_o, kv_norm, w_kvf, b_f, g_k, b_w_qg, b_g_q, b_w_o)
    s0_prompt = jnp.zeros((x_prompt.shape[0],) + state_gla.shape[2:], F32)
    y_p, gla_p, k_p, v_p, lf_p = _trunk(x_prompt, s0_prompt, None, w)
    y_s, gla_s, k_s, v_s, lf_s = _trunk(x_sample, state_gla[:, 0], (cache_k, cache_v, cache_logf), w)
    return (y_p, y_s, gla_p, gla_s, k_p, v_p, lf_p, k_s, v_s, lf_s)
```

```python
import functools

import jax
import jax.numpy as jnp
import numpy as np
from jax import lax
from jax.experimental import pallas as pl
from jax.experimental.pallas import tpu as pltpu

F32 = jnp.float32
BF16 = jnp.bfloat16

EPS = 1e-6
NEG_INF = -1e30
LOG2E = 1.4426950408889634

LANES = 128
V7X_SCOPED_VMEM_CAP = 60000 * 1024

GLA_GATE_TAU = 16.0
FOX_HEADS = 16
FOX_HEAD_DIM = 64
FOX_PAIRS = FOX_HEADS // 2
BIAS_PARTS = 3
BIAS_LANES_PER_HEAD = 2 * BIAS_PARTS

NT_DIMS = (((1,), (1,)), ((), ()))
TN_DIMS = (((0,), (0,)), ((), ()))


def _cparams(semantics, vmem_bytes):
    limit = int(min(max(vmem_bytes, 16 * 1024 * 1024), V7X_SCOPED_VMEM_CAP))
    return pltpu.CompilerParams(dimension_semantics=semantics, vmem_limit_bytes=limit)


def _dot(a, b):
    return jnp.dot(a, b, preferred_element_type=F32)


def _rms(x, g):
    ms = jnp.mean(x * x, axis=-1, keepdims=True)
    return x * lax.rsqrt(ms + EPS) * g


def _split2(x):
    hi = x.astype(BF16)
    lo = (x - hi.astype(F32)).astype(BF16)
    return hi, lo


def _split3(x):
    p1 = x.astype(BF16)
    r1 = x - p1.astype(F32)
    p2 = r1.astype(BF16)
    p3 = (r1 - p2.astype(F32)).astype(BF16)
    return p1, p2, p3


def _log_sigmoid(x):
    return jnp.minimum(x, 0.0) - jnp.log(1.0 + jnp.exp(-jnp.abs(x)))


def _head_norm(x, gsum_ref, gexp_ref, gain, head_dim):
    ssq = _dot((x * x).astype(BF16), gsum_ref[...])
    inv = lax.rsqrt(ssq * (1.0 / head_dim) + EPS)
    ihi, ilo = _split2(inv)
    inv_full = _dot(ihi, gexp_ref[...]) + _dot(ilo, gexp_ref[...])
    return x * inv_full * gain


def _ffn_body(x_ref, g_ref, wg_ref, wu_ref, wd_ref, o_ref, xn_ref):
    j = pl.program_id(1)

    @pl.when(j == 0)
    def _():
        x = x_ref[...]
        xn_ref[...] = _rms(x, g_ref[...]).astype(BF16)
        o_ref[...] = x

    xn = xn_ref[...]
    g = _dot(xn, wg_ref[...])
    u = _dot(xn, wu_ref[...])
    h = (0.5 * (g * jax.nn.sigmoid(g) * u)).astype(BF16)
    o_ref[...] += _dot(h, wd_ref[...])


def _tile_cols_body(w_ref, o_ref):
    tf = o_ref.shape[2]
    for c in range(o_ref.shape[0]):
        o_ref[c] = w_ref[:, c * tf:(c + 1) * tf].astype(o_ref.dtype)


def _tile_cols(w, tf, *, rows):
    g, d, n = w.shape
    assert n % tf == 0 and d % rows == 0
    return pl.pallas_call(
        _tile_cols_body,
        out_shape=jax.ShapeDtypeStruct((g, n // tf, d, tf), BF16),
        grid=(g, d // rows),
        in_specs=[pl.BlockSpec((None, rows, n), lambda a, r: (a, r, 0))],
        out_specs=pl.BlockSpec((None, n // tf, rows, tf), lambda a, r: (a, 0, r, 0)),
        compiler_params=_cparams(("parallel", "parallel"), 2 * rows * n * (4 + 2) + (4 << 20)),
        name="tile_cols",
    )(w)


def _ffn(x, g, w_gu, w_down, *, tm):
    n, d = x.shape
    d_ff = w_down.shape[0]
    tf = w_gu.shape[2]
    tm = min(tm, n)
    nf = d_ff // tf
    assert n % tm == 0 and w_gu.shape[0] == 2 * nf
    vmem = 4 * tm * d * 4 + tm * d * 2 + 2 * 3 * d * tf * 2 + 3 * tm * tf * 4 + tm * d * 4
    return pl.pallas_call(
        _ffn_body,
        out_shape=jax.ShapeDtypeStruct((n, d), F32),
        grid=(n // tm, nf),
        in_specs=[
            pl.BlockSpec((tm, d), lambda i, j: (i, 0)),
            pl.BlockSpec((1, d), lambda i, j: (0, 0)),
            pl.BlockSpec((None, d, tf), lambda i, j: (j, 0, 0)),
            pl.BlockSpec((None, d, tf), lambda i, j: (j + nf, 0, 0)),
            pl.BlockSpec((tf, d), lambda i, j: (j, 0)),
        ],
        out_specs=pl.BlockSpec((tm, d), lambda i, j: (i, 0)),
        scratch_shapes=[pltpu.VMEM((tm, d), BF16)],
        compiler_params=_cparams(("parallel", "arbitrary"), vmem + (8 << 20)),
        name="ffn",
    )(x, g.reshape(1, d), w_gu, w_gu, w_down)


def _proj_res_body(h_ref, a_ref, w_ref, o_ref):
    o_ref[...] = h_ref[...] + _dot(a_ref[...], w_ref[...])


def _proj_res(h, a, w, *, tm):
    n, d = h.shape
    k = a.shape[1]
    tm = min(tm, n)
    assert n % tm == 0
    vmem = 2 * (2 * tm * d * 4 + tm * k * 2 + k * d * 2) + tm * d * 4
    return pl.pallas_call(
        _proj_res_body,
        out_shape=jax.ShapeDtypeStruct((n, d), F32),
        grid=(n // tm,),
        in_specs=[
            pl.BlockSpec((tm, d), lambda i: (i, 0)),
            pl.BlockSpec((tm, k), lambda i: (i, 0)),
            pl.BlockSpec((k, d), lambda i: (0, 0)),
        ],
        out_specs=pl.BlockSpec((tm, d), lambda i: (i, 0)),
        compiler_params=_cparams(("parallel",), vmem + (4 << 20)),
        name="proj_res",
    )(h, a, w)


def _gla_in_body(x_ref, g_ref, wq_ref, wk_ref, wv_ref, wr_ref, wgl_ref, wg2_ref, bg_ref,
                 q_ref, k_ref, v_ref, r_ref, la_ref):
    xn = _rms(x_ref[...], g_ref[...]).astype(BF16)
    q_ref[...] = _dot(xn, wq_ref[...]).astype(q_ref.dtype)
    k_ref[...] = _dot(xn, wk_ref[...]).astype(k_ref.dtype)
    v_ref[...] = _dot(xn, wv_ref[...]).astype(v_ref.dtype)
    r_ref[...] = _dot(xn, wr_ref[...]).astype(r_ref.dtype)
    gl = _dot(xn, wgl_ref[...]).astype(BF16)
    z = _dot(gl, wg2_ref[...]) + bg_ref[...]
    la_ref[...] = _log_sigmoid(z) * (1.0 / GLA_GATE_TAU)


def _gla_in(x, g, wq, wk, wv, wr, wgl, wg2, bg, *, tm):
    n, d = x.shape
    qk, vw = wq.shape[1], wv.shape[1]
    tm = min(tm, n)
    assert n % tm == 0
    row = lambda i: (i, 0)
    fix = lambda i: (0, 0)
    w_bytes = 2 * (2 * d * qk + 2 * d * vw + d * LANES + LANES * qk)
    vmem = 2 * tm * (d + 3 * qk + 2 * vw) * 4 + 2 * w_bytes + tm * d * 2
    return pl.pallas_call(
        _gla_in_body,
        out_shape=[jax.ShapeDtypeStruct((n, qk), BF16), jax.ShapeDtypeStruct((n, qk), BF16),
                   jax.ShapeDtypeStruct((n, vw), BF16), jax.ShapeDtypeStruct((n, vw), BF16),
                   jax.ShapeDtypeStruct((n, qk), F32)],
        grid=(n // tm,),
        in_specs=[
            pl.BlockSpec((tm, d), row), pl.BlockSpec((1, d), fix),
            pl.BlockSpec((d, qk), fix), pl.BlockSpec((d, qk), fix),
            pl.BlockSpec((d, vw), fix), pl.BlockSpec((d, vw), fix),
            pl.BlockSpec((d, LANES), fix), pl.BlockSpec((LANES, qk), fix), pl.BlockSpec((1, qk), fix),
        ],
        out_specs=[pl.BlockSpec((tm, qk), row), pl.BlockSpec((tm, qk), row),
                   pl.BlockSpec((tm, vw), row), pl.BlockSpec((tm, vw), row),
                   pl.BlockSpec((tm, qk), row)],
        compiler_params=_cparams(("parallel",), vmem + (8 << 20)),
        name="gla_in",
    )(x, g.reshape(1, d), wq, wk, wv, wr, wgl, wg2, bg.reshape(1, qk))


def _gla_body(q_ref, k_ref, v_ref, r_ref, la_ref, s0_ref, gout_ref, tri_ref,
              og_ref, sfin_ref, st_ref, *, chunk, n_chunks, heads, dk, dv):
    t = pl.program_id(1)

    @pl.when(t == 0)
    def _():
        st_ref[...] = s0_ref[...]

    scale = dk ** -0.5
    row = lax.broadcasted_iota(jnp.int32, (chunk, chunk), 0)
    col = lax.broadcasted_iota(jnp.int32, (chunk, chunk), 1)
    causal = col <= row
    tri = tri_ref[...]
    gout = gout_ref[...]

    for c in range(n_chunks):
        sl = slice(c * chunk, (c + 1) * chunk)
        la_hi, la_lo = _split2(la_ref[sl, :])
        b = _dot(tri, la_hi) + _dot(tri, la_lo)
        b_last = b[chunk - 1:chunk, :]
        q = q_ref[sl, :].astype(F32)
        k = k_ref[sl, :].astype(F32)
        qe = (q * scale * jnp.exp(b)).astype(BF16)
        ke = (k * jnp.exp(-b)).astype(BF16)
        kd = (k * jnp.exp(b_last - b)).astype(BF16)
        dec = jnp.exp(b_last)
        for h in range(heads):
            ks = slice(h * dk, (h + 1) * dk)
            vs = slice(h * dv, (h + 1) * dv)
            vh = v_ref[sl, vs]
            att = lax.dot_general(qe[:, ks], ke[:, ks], NT_DIMS, preferred_element_type=F32)
            att = jnp.where(causal, att, 0.0).astype(BF16)
            st = st_ref[h]
            o = _dot(att, vh) + lax.dot_general(qe[:, ks], st.astype(BF16), NT_DIMS,
                                                preferred_element_type=F32)
            st_ref[h] = st * dec[:, ks] + lax.dot_general(vh, kd[:, ks], TN_DIMS,
                                                          preferred_element_type=F32)
            on = _rms(o, gout)
            rh = r_ref[sl, vs].astype(F32)
            og_ref[sl, vs] = (on * (rh * jax.nn.sigmoid(rh))).astype(BF16)

    @pl.when(t == pl.num_programs(1) - 1)
    def _():
        sfin_ref[...] = st_ref[...]


def _gla(q, k, v, r, la, s0t, gout, *, chunk, tb):
    bsz, t, qk = q.shape
    vw = v.shape[2]
    heads = s0t.shape[1]
    dk, dv = qk // heads, vw // heads
    tb = min(tb, t)
    assert t % tb == 0 and tb % chunk == 0
    tri = jnp.tril(jnp.ones((chunk, chunk), F32)).astype(BF16)
    blk = lambda w: pl.BlockSpec((None, tb, w), lambda b, i: (b, i, 0))
    st_spec = pl.BlockSpec((None, heads, dv, dk), lambda b, i: (b, 0, 0, 0))
    vmem = 2 * tb * (3 * qk + 2 * vw) * 4 + 2 * tb * vw * 2 + 5 * heads * dv * dk * 4
    return pl.pallas_call(
        functools.partial(_gla_body, chunk=chunk, n_chunks=tb // chunk, heads=heads, dk=dk, dv=dv),
        out_shape=[jax.ShapeDtypeStruct((bsz, t, vw), BF16),
                   jax.ShapeDtypeStruct((bsz, heads, dv, dk), F32)],
        grid=(bsz, t // tb),
        in_specs=[blk(qk), blk(qk), blk(vw), blk(vw), blk(qk), st_spec,
                  pl.BlockSpec((1, dv), lambda b, i: (0, 0)),
                  pl.BlockSpec((chunk, chunk), lambda b, i: (0, 0))],
        out_specs=[blk(vw), st_spec],
        scratch_shapes=[pltpu.VMEM((heads, dv, dk), F32)],
        compiler_params=_cparams(("parallel", "arbitrary"), vmem + (8 << 20)),
        name="gla_chunks",
    )(q, k, v, r, la, s0t, gout.reshape(1, dv), tri)


def _store_pairs(ref, x):
    for p in range(ref.shape[0]):
        ref[p] = x[:, p * LANES:(p + 1) * LANES]


def _kv_body(x_ref, g_ref, wk_ref, wv_ref, wf_ref, bf_ref, gk_ref, gsum_ref, gexp_ref,
             k_ref, v_ref, lf_ref, k16_ref, v16_ref, *, vt_block):
    xn = _rms(x_ref[...], g_ref[...]).astype(BF16)
    k = _head_norm(_dot(xn, wk_ref[...]), gsum_ref, gexp_ref, gk_ref[...], FOX_HEAD_DIM)
    k_ref[...] = k.reshape(k_ref.shape)
    _store_pairs(k16_ref, k.astype(BF16))
    lf_ref[...] = _log_sigmoid(_dot(xn, wf_ref[...]) + bf_ref[...])
    if vt_block is None:
        v = _dot(xn, wv_ref[...])
        v_ref[...] = v.reshape(v_ref.shape)
        _store_pairs(v16_ref, v.astype(BF16))
    else:
        vt = lax.dot_general(wv_ref[...], xn, NT_DIMS, preferred_element_type=F32)
        v_ref[...] = vt.T.reshape(v_ref.shape)
        vt16 = vt.astype(BF16)
        for c in range(v16_ref.shape[0]):
            v16_ref[c] = vt16[:, c * vt_block:(c + 1) * vt_block]


def _shared_kv(x, g, wk, wv, wf, bf, gk, gsum, gexp, *, tm, vt_block=None):
    n, d = x.shape
    tm = min(tm, n)
    assert n % tm == 0
    row = lambda i: (i, 0)
    fix = lambda i: (0, 0)
    heads_shape = (n, FOX_HEADS, FOX_HEAD_DIM)
    heads_spec = pl.BlockSpec((tm, FOX_HEADS, FOX_HEAD_DIM), lambda i: (i, 0, 0))
    pair_shape = (d // LANES, n, LANES)
    pair_spec = pl.BlockSpec((d // LANES, tm, LANES), lambda i: (0, i, 0))
    if vt_block is None:
        v16_shape, v16_spec = pair_shape, pair_spec
    else:
        assert tm % vt_block == 0
        v16_shape = (n // vt_block, d, vt_block)
        v16_spec = pl.BlockSpec((tm // vt_block, d, vt_block), lambda i: (i, 0, 0))
    vmem = 2 * tm * d * (4 + 8 + 8 + 2 + 2) + 2 * tm * LANES * 4 + 2 * 2 * (2 * d * d + 3 * d * LANES) \
        + 8 * tm * d * 4
    return pl.pallas_call(
        functools.partial(_kv_body, vt_block=vt_block),
        out_shape=[jax.ShapeDtypeStruct(heads_shape, F32), jax.ShapeDtypeStruct(heads_shape, F32),
                   jax.ShapeDtypeStruct((n, LANES), F32),
                   jax.ShapeDtypeStruct(pair_shape, BF16), jax.ShapeDtypeStruct(v16_shape, BF16)],
        grid=(n // tm,),
        in_specs=[
            pl.BlockSpec((tm, d), row), pl.BlockSpec((1, d), fix),
            pl.BlockSpec((d, d), fix), pl.BlockSpec((d, d), fix), pl.BlockSpec((d, LANES), fix),
            pl.BlockSpec((1, LANES), fix), pl.BlockSpec((1, d), fix),
            pl.BlockSpec((d, LANES), fix), pl.BlockSpec((LANES, d), fix),
        ],
        out_specs=[heads_spec, heads_spec,
                   pl.BlockSpec((tm, LANES), row),
                   pair_spec, v16_spec],
        compiler_params=_cparams(("parallel",), vmem),
        name="shared_kv",
    )(x, g.reshape(1, d), wk, wv, wf, bf, gk, gsum, gexp)


def _fox_q_body(x_ref, g_ref, wq_ref, wg_ref, gq_ref, gsum_ref, gexp_ref, q16_ref, sg_ref):
    xn = _rms(x_ref[...], g_ref[...]).astype(BF16)
    q = _head_norm(_dot(xn, wq_ref[...]), gsum_ref, gexp_ref, gq_ref[...], FOX_HEAD_DIM)
    q16_ref[...] = (q * (FOX_HEAD_DIM ** -0.5 * LOG2E)).astype(BF16)
    sg_ref[...] = jax.nn.sigmoid(_dot(xn, wg_ref[...])).astype(sg_ref.dtype)


def _fox_q(x, g, wq, wg, gq, gsum, gexp, *, tm):
    n, d = x.shape
    tm = min(tm, n)
    assert n % tm == 0
    row = lambda i: (i, 0)
    fix = lambda i: (0, 0)
    vmem = 2 * tm * d * (4 + 2 + 4) + 2 * 2 * (2 * d * d + 2 * d * LANES) + 6 * tm * d * 4
    return pl.pallas_call(
        _fox_q_body,
        out_shape=[jax.ShapeDtypeStruct((n, d), BF16), jax.ShapeDtypeStruct((n, d), BF16)],
        grid=(n // tm,),
        in_specs=[
            pl.BlockSpec((tm, d), row), pl.BlockSpec((1, d), fix),
            pl.BlockSpec((d, d), fix), pl.BlockSpec((d, d), fix), pl.BlockSpec((1, d), fix),
            pl.BlockSpec((d, LANES), fix), pl.BlockSpec((LANES, d), fix),
        ],
        out_specs=[pl.BlockSpec((tm, d), row), pl.BlockSpec((tm, d), row)],
        compiler_params=_cparams(("parallel",), vmem),
        name="fox_q",
    )(x, g.reshape(1, d), wq, wg, gq, gsum, gexp)


def _pack_parts(x):
    p1, p2, p3 = (p.astype(F32) for p in _split3(x))
    packed = p1 + pltpu.roll(p2, FOX_HEADS, 1) + pltpu.roll(p3, 2 * FOX_HEADS, 1)
    return packed.astype(BF16)


def _bias_body(lf_ref, tri_ref, sel_ref, one_ref, qb_ref, kb_ref, carry_ref):
    @pl.when(pl.program_id(1) == 0)
    def _():
        carry_ref[...] = jnp.zeros_like(carry_ref)

    lf = lf_ref[...]
    tb = lf.shape[0]
    head_lanes = lax.broadcasted_iota(jnp.int32, lf.shape, 1) < FOX_HEADS
    cp = _dot(tri_ref[...], _pack_parts(jnp.where(head_lanes, lf, 0.0)))
    c = cp + pltpu.roll(cp, LANES - FOX_HEADS, 1) + pltpu.roll(cp, LANES - 2 * FOX_HEADS, 1)
    c = jnp.where(head_lanes, c, 0.0) + carry_ref[...]
    carry_ref[...] = c[tb - 1:tb, :]
    qkb = _dot(_pack_parts(c * LOG2E), sel_ref[...]) + one_ref[...]
    wide = FOX_PAIRS * LANES
    for p in range(FOX_PAIRS):
        qb_ref[p] = qkb[:, p * LANES:(p + 1) * LANES].astype(BF16)
        kb_ref[p] = qkb[:, wide + p * LANES:wide + (p + 1) * LANES].astype(BF16)


def _bias_constants():
    wide = FOX_PAIRS * LANES
    sel = np.zeros((LANES, 2 * wide), np.float32)
    one = np.zeros((1, 2 * wide), np.float32)
    for p in range(FOX_PAIRS):
        for e in range(2):
            base = p * LANES + e * BIAS_LANES_PER_HEAD
            for part in range(BIAS_PARTS):
                row = part * FOX_HEADS + 2 * p + e
                sel[row, base + part] = 1.0
                one[0, base + BIAS_PARTS + part] = 1.0
                one[0, wide + base + part] = 1.0
                sel[row, wide + base + BIAS_PARTS + part] = -1.0
    return jnp.asarray(sel, BF16), jnp.asarray(one)


def _fox_bias(lf, *, tb):
    bsz, kp, _ = lf.shape
    assert kp % tb == 0
    sel, one = _bias_constants()
    tri = jnp.tril(jnp.ones((tb, tb), F32)).astype(BF16)
    wide = FOX_PAIRS * LANES
    out_spec = pl.BlockSpec((None, FOX_PAIRS, tb, LANES), lambda b, i: (b, 0, i, 0))
    vmem = 2 * tb * LANES * 4 + 2 * tb * tb * 2 + 2 * LANES * 2 * wide * 2 \
        + 4 * FOX_PAIRS * tb * LANES * 2 + 4 * tb * 2 * wide * 4
    return pl.pallas_call(
        _bias_body,
        out_shape=[jax.ShapeDtypeStruct((bsz, FOX_PAIRS, kp, LANES), BF16)] * 2,
        grid=(bsz, kp // tb),
        in_specs=[
            pl.BlockSpec((None, tb, LANES), lambda b, i: (b, i, 0)),
            pl.BlockSpec((tb, tb), lambda b, i: (0, 0)),
            pl.BlockSpec((LANES, 2 * wide), lambda b, i: (0, 0)),
            pl.BlockSpec((1, 2 * wide), lambda b, i: (0, 0)),
        ],
        out_specs=[out_spec, out_spec],
        scratch_shapes=[pltpu.VMEM((1, LANES), F32)],
        compiler_params=_cparams(("parallel", "arbitrary"), vmem),
        name="fox_bias",
    )(lf, tri, sel, one)


def _q_aug(q, qb, lane):
    zero = jnp.zeros_like(q)
    out = []
    for e in range(2):
        head_lanes = (lane >> 6) == e
        bias_lanes = jnp.logical_and(lane >= e * BIAS_LANES_PER_HEAD,
                                     lane < (e + 1) * BIAS_LANES_PER_HEAD)
        out.append(jnp.concatenate(
            [jnp.where(head_lanes, q, zero), jnp.where(bias_lanes, qb, zero)], axis=1))
    return out


def _attn_body(q_ref, qb_ref, k_ref, kb_ref, v_ref, sg_ref, o_ref, m_ref, l_ref, acc_ref,
               *, tq, tk, n_qblocks):
    i = pl.program_id(2)
    lane = lax.broadcasted_iota(jnp.int32, (tq, LANES), 1)
    q_aug = _q_aug(q_ref[...], qb_ref[...], lane)

    m_ref[...] = jnp.full(m_ref.shape, NEG_INF, F32)
    l_ref[...] = jnp.zeros(l_ref.shape, F32)
    acc_ref[...] = jnp.zeros(acc_ref.shape, F32)

    def qk_scores(j, qwin=slice(None)):
        rows = slice(j * tk, (j + 1) * tk)
        k_aug = jnp.concatenate([k_ref[rows, :], kb_ref[rows, :]], axis=1)
        qs = q_aug if qwin == slice(None) else [qa[qwin] for qa in q_aug]
        return [lax.dot_general(k_aug, qs[e], NT_DIMS, preferred_element_type=F32)
                for e in range(2)]

    def softmax_pv(j, scores, qwin=slice(None)):
        v = v_ref[j]
        if qwin == slice(None):
            at = lambda ref, e: ref.at[e]
        else:
            at = lambda ref, e: ref.at[e, :, qwin]
        for e in range(2):
            s = scores[e]
            m_prev = at(m_ref, e)[...]
            m_new = jnp.maximum(m_prev, jnp.max(s, axis=0, keepdims=True))
            alpha = jnp.exp2(m_prev - m_new)
            p = jnp.exp2(s - m_new)
            at(l_ref, e)[...] = alpha * at(l_ref, e)[...] + jnp.sum(p, axis=0, keepdims=True)
            p16 = p.astype(BF16)
            at(acc_ref, e)[...] = alpha * at(acc_ref, e)[...] + _dot(v, p16)
            at(m_ref, e)[...] = m_new

    early, late = slice(0, tk), slice(tk, tq)

    def schedule(n_visible):
        ahead = qk_scores(0)
        for j in range(n_visible):
            cur, ahead = ahead, qk_scores(j + 1)
            softmax_pv(j, cur)
        tmask = jnp.where(lax.broadcasted_iota(jnp.int32, (tk, tk), 0)
                          <= lax.broadcasted_iota(jnp.int32, (tk, tk), 1), 0.0, NEG_INF)
        late_scores = qk_scores(n_visible + 1, late)
        softmax_pv(n_visible, [jnp.concatenate([s[:, early] + tmask, s[:, late]], axis=1)
                               for s in ahead])
        softmax_pv(n_visible + 1, [s + tmask for s in late_scores], late)

    lax.switch(i, [functools.partial(schedule, 2 * n) for n in range(n_qblocks)])

    o0 = acc_ref[0] * (1.0 / l_ref[0])
    o1 = acc_ref[1] * (1.0 / l_ref[1])
    feature = lax.broadcasted_iota(jnp.int32, (LANES, tq), 0)
    o = jnp.where((feature >> 6) == 0, o0, o1).T
    o_ref[...] = (o * sg_ref[...].astype(F32)).astype(BF16)


def _fox_attn(q16, qb, k16, kb, vt16, sg, *, tq, tk):
    bsz, t, d = q16.shape
    assert t % tq == 0 and tq == 2 * tk and k16.shape[2] == t
    qspec = pl.BlockSpec((None, tq, LANES), lambda b, p, i: (b, i, p))
    vmem = 2 * (3 * t * LANES * 2 + tq * LANES * (2 + 2 + 2 + 2)) + 3 * tq * LANES * 4 \
        + 2 * tk * 2 * LANES * 2 + 10 * tq * tk * 4
    return pl.pallas_call(
        functools.partial(_attn_body, tq=tq, tk=tk, n_qblocks=t // tq),
        out_shape=jax.ShapeDtypeStruct((bsz, t, d), BF16),
        grid=(bsz, FOX_PAIRS, t // tq),
        in_specs=[
            qspec,
            pl.BlockSpec((None, None, tq, LANES), lambda b, p, i: (b, p, i, 0)),
            pl.BlockSpec((None, None, t, LANES), lambda b, p, i: (p, b, 0, 0)),
            pl.BlockSpec((None, None, t, LANES), lambda b, p, i: (b, p, 0, 0)),
            pl.BlockSpec((t // tk, LANES, tk), lambda b, p, i: (b, p, 0)),
            qspec,
        ],
        out_specs=qspec,
        scratch_shapes=[pltpu.VMEM((2, 1, tq), F32), pltpu.VMEM((2, 1, tq), F32),
                        pltpu.VMEM((2, LANES, tq), F32)],
        compiler_params=_cparams(("parallel", "parallel", "arbitrary"), vmem + (8 << 20)),
        name="fox_attn",
    )(q16, qb, k16, kb, vt16, sg)


def _cache_attn_body(q_ref, qb_ref, ck_ref, cv_ref, kn_ref, vn_ref, kb_ref, sg_ref,
                     o_ref, m_ref, l_ref, acc_ref, *, tk):
    t = q_ref.shape[0]
    group, p_len = ck_ref.shape[0], ck_ref.shape[1]
    lane = lax.broadcasted_iota(jnp.int32, (t, LANES), 1)
    m_ref[...] = jnp.full(m_ref.shape, NEG_INF, F32)
    l_ref[...] = jnp.zeros(l_ref.shape, F32)
    acc_ref[...] = jnp.zeros(acc_ref.shape, F32)
    causal = jnp.where(lax.broadcasted_iota(jnp.int32, (t, t), 1)
                       <= lax.broadcasted_iota(jnp.int32, (t, t), 0), 0.0, NEG_INF)

    for g in range(group):
        cols = slice(g * LANES, (g + 1) * LANES)
        q_aug = _q_aug(q_ref[:, cols], qb_ref[g], lane)

        def fold(k, kb, v, mask):
            k_aug = jnp.concatenate([k, kb], axis=1)
            scores = [lax.dot_general(q_aug[e], k_aug, NT_DIMS, preferred_element_type=F32)
                      for e in range(2)]
            for e in range(2):
                s = scores[e] if mask is None else scores[e] + mask
                m_prev = m_ref[g, e]
                m_new = jnp.maximum(m_prev, jnp.max(s, axis=1, keepdims=True))
                alpha = jnp.exp2(m_prev - m_new)
                p = jnp.exp2(s - m_new)
                l_ref[g, e] = alpha * l_ref[g, e] + jnp.sum(p, axis=1, keepdims=True)
                acc_ref[g, e] = alpha * acc_ref[g, e] + _dot(p.astype(BF16), v)
                m_ref[g, e] = m_new

        for j in range(p_len // tk):
            rows = slice(j * tk, (j + 1) * tk)
            fold(ck_ref[g, rows, :], kb_ref[g, rows, :], cv_ref[g, rows, :], None)
        fold(kn_ref[g], kb_ref[g, p_len:p_len + t, :], vn_ref[g], causal)

        o = jnp.where((lane >> 6) == 0, acc_ref[g, 0] * (1.0 / l_ref[g, 0]),
                      acc_ref[g, 1] * (1.0 / l_ref[g, 1]))
        o_ref[:, cols] = (o * sg_ref[:, cols].astype(F32)).astype(BF16)


def _fox_attn_cached(q16, qb, kb, ck16, cv16, k16, v16, sg, *, tk):
    bsz, t, d = q16.shape
    p_len = ck16.shape[2]
    grp = CACHE_PAIR_GROUP
    assert p_len % tk == 0 and p_len % t == 0 and kb.shape[2] == p_len + t and FOX_PAIRS % grp == 0
    qspec = pl.BlockSpec((None, t, grp * LANES), lambda b, p: (b, 0, p))
    cspec = pl.BlockSpec((grp, None, p_len, LANES), lambda b, p: (p, b, 0, 0))
    nspec = pl.BlockSpec((grp, t, LANES), lambda b, p: (p, b, 0))
    vmem = grp * (2 * (3 * (p_len + t) * LANES * 2 + t * LANES * 8) + 6 * t * LANES * 4
                  + 8 * t * tk * 4 + 4 * tk * LANES * 2)
    return pl.pallas_call(
        functools.partial(_cache_attn_body, tk=tk),
        out_shape=jax.ShapeDtypeStruct((bsz, t, d), BF16),
        grid=(bsz, FOX_PAIRS // grp),
        in_specs=[
            qspec,
            pl.BlockSpec((None, grp, t, LANES), lambda b, p: (b, p, p_len // t, 0)),
            cspec, cspec, nspec, nspec,
            pl.BlockSpec((None, grp, p_len + t, LANES), lambda b, p: (b, p, 0, 0)),
            qspec,
        ],
        out_specs=qspec,
        scratch_shapes=[pltpu.VMEM((grp, 2, t, 1), F32), pltpu.VMEM((grp, 2, t, 1), F32),
                        pltpu.VMEM((grp, 2, t, LANES), F32)],
        compiler_params=_cparams(("parallel", "parallel"), vmem + (8 << 20)),
        name="fox_attn_cached",
    )(q16, qb, ck16, cv16, k16, v16, kb, sg)


TM = 1024
KV_TM = 512
FFN_TM = 1024
FFN_TF = 256
REPACK_ROWS = 128
GLA_TB = 512
ATTN_TQ = 1024
ATTN_TK = 512
CACHE_TK = 4096
CACHE_PAIR_GROUP = 4
BIAS_TB_CHOICES = (832, 512, 384, 320, 256, 192, 128, 64)


def _prep_weights(ffn_norm, w_ffn_gu, w_ffn_down, mix_norm, a_w_in, a_w_g2, a_b_g, a_g_out, a_w_o,
                  kv_norm, w_kvf, b_f, g_k, b_w_qg, b_g_q, b_w_o):
    d = w_ffn_gu.shape[2]
    rank = a_w_g2.shape[1]
    qk = a_w_g2.shape[2]
    vw = a_w_o.shape[1]
    w_in = a_w_in[0]
    pad_cols = lambda w: jnp.pad(w, ((0, 0), (0, LANES - w.shape[1])))
    groups = np.arange(d) // FOX_HEAD_DIM
    gsum = (groups[:, None] == np.arange(LANES)[None, :]).astype(np.float32)
    return dict(
        ffn_norm=ffn_norm, mix_norm=mix_norm, kv_norm=kv_norm,
        w_gu=_tile_cols(w_ffn_gu.reshape((-1,) + w_ffn_gu.shape[2:]), FFN_TF, rows=REPACK_ROWS),
        w_down=w_ffn_down.astype(BF16),
        a_wq=w_in[:, :qk].astype(BF16), a_wk=w_in[:, qk:2 * qk].astype(BF16),
        a_wv=w_in[:, 2 * qk:2 * qk + vw].astype(BF16),
        a_wr=w_in[:, 2 * qk + vw:2 * qk + 2 * vw].astype(BF16),
        a_wgl=pad_cols(w_in[:, 2 * qk + 2 * vw:]).astype(BF16),
        a_wg2=jnp.pad(a_w_g2[0], ((0, LANES - rank), (0, 0))).astype(BF16),
        a_bg=a_b_g[0], a_gout=a_g_out[0], a_wo=a_w_o[0].astype(BF16),
        kv_wk=w_kvf[:, :d].astype(BF16), kv_wv=w_kvf[:, d:2 * d].astype(BF16),
        kv_wv_t=w_kvf[:, d:2 * d].T.astype(BF16),
        kv_wf=pad_cols(w_kvf[:, 2 * d:]).astype(BF16),
        kv_bf=jnp.pad(b_f, (0, LANES - b_f.shape[0])).reshape(1, LANES),
        gk=jnp.tile(g_k, FOX_HEADS).reshape(1, d), gq=jnp.tile(b_g_q[0], FOX_HEADS).reshape(1, d),
        b_wq=b_w_qg[0][:, :d].astype(BF16), b_wg=b_w_qg[0][:, d:].astype(BF16),
        b_wo=b_w_o[0].astype(BF16),
        gsum=jnp.asarray(gsum, BF16), gexp=jnp.asarray(gsum.T, BF16),
    )


def _trunk(x, s0, past, w):
    bsz, t, d = x.shape
    n = bsz * t
    h = x.reshape(n, d)
    ffn = lambda h_, layer, half: _ffn(h_, w["ffn_norm"][layer, half], w["w_gu"][2 * layer + half],
                                       w["w_down"][layer, half], tm=FFN_TM)
    h = ffn(h, 0, 0)
    q, k, v, r, la = _gla_in(h, w["mix_norm"][0], w["a_wq"], w["a_wk"], w["a_wv"], w["a_wr"],
                             w["a_wgl"], w["a_wg2"], w["a_bg"], tm=TM)
    s3 = lambda z: z.reshape(bsz, t, z.shape[1])
    chunk = 64
    og, st_fin = _gla(s3(q), s3(k), s3(v), s3(r), s3(la), jnp.swapaxes(s0, 2, 3), w["a_gout"],
                      chunk=chunk, tb=GLA_TB)
    h = _proj_res(h, og.reshape(n, -1), w["a_wo"], tm=TM)
    h = ffn(h, 0, 1)
    tq = min(ATTN_TQ, t)
    tk = min(ATTN_TK, tq // 2)
    k_new, v_new, lf, k16, v16 = _shared_kv(
        h, w["kv_norm"], w["kv_wk"], w["kv_wv_t"] if past is None else w["kv_wv"], w["kv_wf"],
        w["kv_bf"], w["gk"], w["gsum"], w["gexp"], tm=KV_TM, vt_block=tk if past is None else None)
    h = ffn(h, 1, 0)
    q16, sg = _fox_q(h, w["mix_norm"][1], w["b_wq"], w["b_wg"], w["gq"], w["gsum"], w["gexp"], tm=TM)
    pairs4 = lambda z: z.reshape(FOX_PAIRS, bsz, t, LANES)
    if past is None:
        lf_all = s3(lf)
    else:
        past_k, past_v, past_lf = past
        p_len = past_k.shape[1]
        past_pairs = lambda a: jnp.transpose(
            a.astype(BF16).reshape(bsz, p_len, FOX_PAIRS, LANES), (2, 0, 1, 3))
        lf_all = jnp.concatenate(
            [jnp.pad(past_lf, ((0, 0), (0, 0), (0, LANES - past_lf.shape[2]))), s3(lf)], axis=1)
    bias_tb = next(c for c in BIAS_TB_CHOICES if lf_all.shape[1] % c == 0)
    qb, kb = _fox_bias(lf_all, tb=bias_tb)
    if past is None:
        og = _fox_attn(s3(q16), qb, pairs4(k16), kb, v16, s3(sg), tq=tq, tk=tk)
    else:
        og = _fox_attn_cached(s3(q16), qb, kb, past_pairs(past_k), past_pairs(past_v), k16, v16,
                              s3(sg), tk=min(CACHE_TK, p_len))
    h = _proj_res(h, og.reshape(n, d), w["b_wo"], tm=TM)
    h = ffn(h, 1, 1)
    heads4 = lambda z: z.reshape(bsz, t, FOX_HEADS, FOX_HEAD_DIM)
    return (h.reshape(bsz, t, d), jnp.swapaxes(st_fin, 2, 3)[:, None], heads4(k_new), heads4(v_new),
            lf[:, :FOX_HEADS].reshape(bsz, t, FOX_HEADS))


def kernel(x_prompt, x_sample, state_gla, cache_k, cache_v, cache_logf, ffn_norm, w_ffn_gu, w_ffn_down, mix_norm, a_w_in, a_w_g2, a_b_g, a_g_out, a_w_o, kv_norm, w_kvf, b_f, g_k, b_w_qg, b_g_q, b_w_o):
    w = _prep_weights(ffn_norm, w_ffn_gu, w_ffn_down, mix_norm, a_w_in, a_w_g2, a_b_g, a_g_out,
                      a_w_o, kv_norm, w_kvf, b_f, g_k, b_w_qg, b_g_q, b_w_o)
    s0_prompt = jnp.zeros((x_prompt.shape[0],) + state_gla.shape[2:], F32)
    y_p, gla_p, k_p, v_p, lf_p = _trunk(x_prompt, s0_prompt, None, w)
    y_s, gla_s, k_s, v_s, lf_s = _trunk(x_sample, state_gla[:, 0], (cache_k, cache_v, cache_logf), w)
    return (y_p, y_s, gla_p, gla_s, k_p, v_p, lf_p, k_s, v_s, lf_s)
```

```python
import functools

import jax
import jax.numpy as jnp
import numpy as np
from jax import lax
from jax.experimental import pallas as pl
from jax.experimental.pallas import tpu as pltpu

F32 = jnp.float32
BF16 = jnp.bfloat16

EPS = 1e-6
NEG_INF = -1e30
LOG2E = 1.4426950408889634

LANES = 128
V7X_SCOPED_VMEM_CAP = 60000 * 1024

GLA_GATE_TAU = 16.0
FOX_HEADS = 16
FOX_HEAD_DIM = 64
FOX_PAIRS = FOX_HEADS // 2
BIAS_PARTS = 3
BIAS_LANES_PER_HEAD = 2 * BIAS_PARTS
DENOM_ROWS = 16

NT_DIMS = (((1,), (1,)), ((), ()))
TN_DIMS = (((0,), (0,)), ((), ()))


def _cparams(semantics, vmem_bytes):
    limit = int(min(max(vmem_bytes, 16 * 1024 * 1024), V7X_SCOPED_VMEM_CAP))
    return pltpu.CompilerParams(dimension_semantics=semantics, vmem_limit_bytes=limit)


def _dot(a, b):
    return jnp.dot(a, b, preferred_element_type=F32)


def _rms(x, g):
    ms = jnp.mean(x * x, axis=-1, keepdims=True)
    return x * lax.rsqrt(ms + EPS) * g


def _split2(x):
    hi = x.astype(BF16)
    lo = (x - hi.astype(F32)).astype(BF16)
    return hi, lo


def _split3(x):
    p1 = x.astype(BF16)
    r1 = x - p1.astype(F32)
    p2 = r1.astype(BF16)
    p3 = (r1 - p2.astype(F32)).astype(BF16)
    return p1, p2, p3


def _log_sigmoid(x):
    return jnp.minimum(x, 0.0) - jnp.log(1.0 + jnp.exp(-jnp.abs(x)))


def _head_norm(x, gsum_ref, gexp_ref, gain, head_dim):
    ssq = _dot((x * x).astype(BF16), gsum_ref[...])
    inv = lax.rsqrt(ssq * (1.0 / head_dim) + EPS)
    ihi, ilo = _split2(inv)
    inv_full = _dot(ihi, gexp_ref[...]) + _dot(ilo, gexp_ref[...])
    return x * inv_full * gain


def _ffn_body(x_ref, g_ref, wg_ref, wu_ref, wd_ref, o_ref, xn_ref):
    j = pl.program_id(1)

    @pl.when(j == 0)
    def _():
        x = x_ref[...]
        xn_ref[...] = _rms(x, g_ref[...]).astype(BF16)
        o_ref[...] = x

    xn = xn_ref[...]
    g = _dot(xn, wg_ref[...])
    u = _dot(xn, wu_ref[...])
    h = (0.5 * (g * jax.nn.sigmoid(g) * u)).astype(BF16)
    o_ref[...] += _dot(h, wd_ref[...])


def _tile_cols_body(w_ref, o_ref):
    tf = o_ref.shape[2]
    for c in range(o_ref.shape[0]):
        o_ref[c] = w_ref[:, c * tf:(c + 1) * tf].astype(o_ref.dtype)


def _tile_cols(w, tf, *, rows):
    g, d, n = w.shape
    assert n % tf == 0 and d % rows == 0
    return pl.pallas_call(
        _tile_cols_body,
        out_shape=jax.ShapeDtypeStruct((g, n // tf, d, tf), BF16),
        grid=(g, d // rows),
        in_specs=[pl.BlockSpec((None, rows, n), lambda a, r: (a, r, 0))],
        out_specs=pl.BlockSpec((None, n // tf, rows, tf), lambda a, r: (a, 0, r, 0)),
        compiler_params=_cparams(("parallel", "parallel"), 2 * rows * n * (4 + 2) + (4 << 20)),
        name="tile_cols",
    )(w)


def _ffn(x, g, w_gu, w_down, *, tm):
    n, d = x.shape
    d_ff = w_down.shape[0]
    tf = w_gu.shape[2]
    tm = min(tm, n)
    nf = d_ff // tf
    assert n % tm == 0 and w_gu.shape[0] == 2 * nf
    vmem = 4 * tm * d * 4 + tm * d * 2 + 2 * 3 * d * tf * 2 + 3 * tm * tf * 4 + tm * d * 4
    return pl.pallas_call(
        _ffn_body,
        out_shape=jax.ShapeDtypeStruct((n, d), F32),
        grid=(n // tm, nf),
        in_specs=[
            pl.BlockSpec((tm, d), lambda i, j: (i, 0)),
            pl.BlockSpec((1, d), lambda i, j: (0, 0)),
            pl.BlockSpec((None, d, tf), lambda i, j: (j, 0, 0)),
            pl.BlockSpec((None, d, tf), lambda i, j: (j + nf, 0, 0)),
            pl.BlockSpec((tf, d), lambda i, j: (j, 0)),
        ],
        out_specs=pl.BlockSpec((tm, d), lambda i, j: (i, 0)),
        scratch_shapes=[pltpu.VMEM((tm, d), BF16)],
        compiler_params=_cparams(("parallel", "arbitrary"), vmem + (8 << 20)),
        name="ffn",
    )(x, g.reshape(1, d), w_gu, w_gu, w_down)


def _proj_res_body(h_ref, a_ref, w_ref, o_ref):
    o_ref[...] = h_ref[...] + _dot(a_ref[...], w_ref[...])


def _proj_res(h, a, w, *, tm):
    n, d = h.shape
    k = a.shape[1]
    tm = min(tm, n)
    assert n % tm == 0
    vmem = 2 * (2 * tm * d * 4 + tm * k * 2 + k * d * 2) + tm * d * 4
    return pl.pallas_call(
        _proj_res_body,
        out_shape=jax.ShapeDtypeStruct((n, d), F32),
        grid=(n // tm,),
        in_specs=[
            pl.BlockSpec((tm, d), lambda i: (i, 0)),
            pl.BlockSpec((tm, k), lambda i: (i, 0)),
            pl.BlockSpec((k, d), lambda i: (0, 0)),
        ],
        out_specs=pl.BlockSpec((tm, d), lambda i: (i, 0)),
        compiler_params=_cparams(("parallel",), vmem + (4 << 20)),
        name="proj_res",
    )(h, a, w)


def _gla_in_body(x_ref, g_ref, wq_ref, wk_ref, wv_ref, wr_ref, wgl_ref, wg2_ref, bg_ref,
                 q_ref, k_ref, v_ref, r_ref, la_ref):
    xn = _rms(x_ref[...], g_ref[...]).astype(BF16)
    q_ref[...] = _dot(xn, wq_ref[...]).astype(q_ref.dtype)
    k_ref[...] = _dot(xn, wk_ref[...]).astype(k_ref.dtype)
    v_ref[...] = _dot(xn, wv_ref[...]).astype(v_ref.dtype)
    r_ref[...] = _dot(xn, wr_ref[...]).astype(r_ref.dtype)
    gl = _dot(xn, wgl_ref[...]).astype(BF16)
    z = _dot(gl, wg2_ref[...]) + bg_ref[...]
    la_ref[...] = _log_sigmoid(z) * (1.0 / GLA_GATE_TAU)


def _gla_in(x, g, wq, wk, wv, wr, wgl, wg2, bg, *, tm):
    n, d = x.shape
    qk, vw = wq.shape[1], wv.shape[1]
    tm = min(tm, n)
    assert n % tm == 0
    row = lambda i: (i, 0)
    fix = lambda i: (0, 0)
    w_bytes = 2 * (2 * d * qk + 2 * d * vw + d * LANES + LANES * qk)
    vmem = 2 * tm * (d + 3 * qk + 2 * vw) * 4 + 2 * w_bytes + tm * d * 2
    return pl.pallas_call(
        _gla_in_body,
        out_shape=[jax.ShapeDtypeStruct((n, qk), BF16), jax.ShapeDtypeStruct((n, qk), BF16),
                   jax.ShapeDtypeStruct((n, vw), BF16), jax.ShapeDtypeStruct((n, vw), BF16),
                   jax.ShapeDtypeStruct((n, qk), F32)],
        grid=(n // tm,),
        in_specs=[
            pl.BlockSpec((tm, d), row), pl.BlockSpec((1, d), fix),
            pl.BlockSpec((d, qk), fix), pl.BlockSpec((d, qk), fix),
            pl.BlockSpec((d, vw), fix), pl.BlockSpec((d, vw), fix),
            pl.BlockSpec((d, LANES), fix), pl.BlockSpec((LANES, qk), fix), pl.BlockSpec((1, qk), fix),
        ],
        out_specs=[pl.BlockSpec((tm, qk), row), pl.BlockSpec((tm, qk), row),
                   pl.BlockSpec((tm, vw), row), pl.BlockSpec((tm, vw), row),
                   pl.BlockSpec((tm, qk), row)],
        compiler_params=_cparams(("parallel",), vmem + (8 << 20)),
        name="gla_in",
    )(x, g.reshape(1, d), wq, wk, wv, wr, wgl, wg2, bg.reshape(1, qk))


def _gla_body(q_ref, k_ref, v_ref, r_ref, la_ref, s0_ref, gout_ref, tri_ref,
              og_ref, sfin_ref, st_ref, *, chunk, n_chunks, heads, dk, dv):
    t = pl.program_id(1)

    @pl.when(t == 0)
    def _():
        st_ref[...] = s0_ref[...]

    scale = dk ** -0.5
    row = lax.broadcasted_iota(jnp.int32, (chunk, chunk), 0)
    col = lax.broadcasted_iota(jnp.int32, (chunk, chunk), 1)
    causal = col <= row
    tri = tri_ref[...]
    gout = gout_ref[...]

    for c in range(n_chunks):
        sl = slice(c * chunk, (c + 1) * chunk)
        la_hi, la_lo = _split2(la_ref[sl, :])
        b = _dot(tri, la_hi) + _dot(tri, la_lo)
        b_last = b[chunk - 1:chunk, :]
        q = q_ref[sl, :].astype(F32)
        k = k_ref[sl, :].astype(F32)
        qe = (q * scale * jnp.exp(b)).astype(BF16)
        ke = (k * jnp.exp(-b)).astype(BF16)
        kd = (k * jnp.exp(b_last - b)).astype(BF16)
        dec = jnp.exp(b_last)
        for h in range(heads):
            ks = slice(h * dk, (h + 1) * dk)
            vs = slice(h * dv, (h + 1) * dv)
            vh = v_ref[sl, vs]
            att = lax.dot_general(qe[:, ks], ke[:, ks], NT_DIMS, preferred_element_type=F32)
            att = jnp.where(causal, att, 0.0).astype(BF16)
            st = st_ref[h]
            o = _dot(att, vh) + lax.dot_general(qe[:, ks], st.astype(BF16), NT_DIMS,
                                                preferred_element_type=F32)
            st_ref[h] = st * dec[:, ks] + lax.dot_general(vh, kd[:, ks], TN_DIMS,
                                                          preferred_element_type=F32)
            on = _rms(o, gout)
            rh = r_ref[sl, vs].astype(F32)
            og_ref[sl, vs] = (on * (rh * jax.nn.sigmoid(rh))).astype(BF16)

    @pl.when(t == pl.num_programs(1) - 1)
    def _():
        sfin_ref[...] = st_ref[...]


def _gla(q, k, v, r, la, s0t, gout, *, chunk, tb):
    bsz, t, qk = q.shape
    vw = v.shape[2]
    heads = s0t.shape[1]
    dk, dv = qk // heads, vw // heads
    tb = min(tb, t)
    assert t % tb == 0 and tb % chunk == 0
    tri = jnp.tril(jnp.ones((chunk, chunk), F32)).astype(BF16)
    blk = lambda w: pl.BlockSpec((None, tb, w), lambda b, i: (b, i, 0))
    st_spec = pl.BlockSpec((None, heads, dv, dk), lambda b, i: (b, 0, 0, 0))
    vmem = 2 * tb * (3 * qk + 2 * vw) * 4 + 2 * tb * vw * 2 + 5 * heads * dv * dk * 4
    return pl.pallas_call(
        functools.partial(_gla_body, chunk=chunk, n_chunks=tb // chunk, heads=heads, dk=dk, dv=dv),
        out_shape=[jax.ShapeDtypeStruct((bsz, t, vw), BF16),
                   jax.ShapeDtypeStruct((bsz, heads, dv, dk), F32)],
        grid=(bsz, t // tb),
        in_specs=[blk(qk), blk(qk), blk(vw), blk(vw), blk(qk), st_spec,
                  pl.BlockSpec((1, dv), lambda b, i: (0, 0)),
                  pl.BlockSpec((chunk, chunk), lambda b, i: (0, 0))],
        out_specs=[blk(vw), st_spec],
        scratch_shapes=[pltpu.VMEM((heads, dv, dk), F32)],
        compiler_params=_cparams(("parallel", "arbitrary"), vmem + (8 << 20)),
        name="gla_chunks",
    )(q, k, v, r, la, s0t, gout.reshape(1, dv), tri)


def _store_pairs(ref, x):
    for p in range(ref.shape[0]):
        ref[p] = x[:, p * LANES:(p + 1) * LANES]


def _kv_body(x_ref, g_ref, wk_ref, wv_ref, wf_ref, bf_ref, gk_ref, gsum_ref, gexp_ref,
             k_ref, v_ref, lf_ref, k16_ref, v16_ref, *, vt_block):
    xn = _rms(x_ref[...], g_ref[...]).astype(BF16)
    k = _head_norm(_dot(xn, wk_ref[...]), gsum_ref, gexp_ref, gk_ref[...], FOX_HEAD_DIM)
    k_ref[...] = k.reshape(k_ref.shape)
    _store_pairs(k16_ref, k.astype(BF16))
    lf_ref[...] = _log_sigmoid(_dot(xn, wf_ref[...]) + bf_ref[...])
    if vt_block is None:
        v = _dot(xn, wv_ref[...])
        v_ref[...] = v.reshape(v_ref.shape)
        _store_pairs(v16_ref, v.astype(BF16))
    else:
        vt = lax.dot_general(wv_ref[...], xn, NT_DIMS, preferred_element_type=F32)
        v_ref[...] = vt.T.reshape(v_ref.shape)
        vt16 = vt.astype(BF16)
        for c in range(v16_ref.shape[0]):
            v16_ref[c] = vt16[:, c * vt_block:(c + 1) * vt_block]


def _shared_kv(x, g, wk, wv, wf, bf, gk, gsum, gexp, *, tm, vt_block=None):
    n, d = x.shape
    tm = min(tm, n)
    assert n % tm == 0
    row = lambda i: (i, 0)
    fix = lambda i: (0, 0)
    heads_shape = (n, FOX_HEADS, FOX_HEAD_DIM)
    heads_spec = pl.BlockSpec((tm, FOX_HEADS, FOX_HEAD_DIM), lambda i: (i, 0, 0))
    pair_shape = (d // LANES, n, LANES)
    pair_spec = pl.BlockSpec((d // LANES, tm, LANES), lambda i: (0, i, 0))
    if vt_block is None:
        v16_shape, v16_spec = pair_shape, pair_spec
    else:
        assert tm % vt_block == 0
        v16_shape = (n // vt_block, d, vt_block)
        v16_spec = pl.BlockSpec((tm // vt_block, d, vt_block), lambda i: (i, 0, 0))
    vmem = 2 * tm * d * (4 + 8 + 8 + 2 + 2) + 2 * tm * LANES * 4 + 2 * 2 * (2 * d * d + 3 * d * LANES) \
        + 8 * tm * d * 4
    return pl.pallas_call(
        functools.partial(_kv_body, vt_block=vt_block),
        out_shape=[jax.ShapeDtypeStruct(heads_shape, F32), jax.ShapeDtypeStruct(heads_shape, F32),
                   jax.ShapeDtypeStruct((n, LANES), F32),
                   jax.ShapeDtypeStruct(pair_shape, BF16), jax.ShapeDtypeStruct(v16_shape, BF16)],
        grid=(n // tm,),
        in_specs=[
            pl.BlockSpec((tm, d), row), pl.BlockSpec((1, d), fix),
            pl.BlockSpec((d, d), fix), pl.BlockSpec((d, d), fix), pl.BlockSpec((d, LANES), fix),
            pl.BlockSpec((1, LANES), fix), pl.BlockSpec((1, d), fix),
            pl.BlockSpec((d, LANES), fix), pl.BlockSpec((LANES, d), fix),
        ],
        out_specs=[heads_spec, heads_spec,
                   pl.BlockSpec((tm, LANES), row),
                   pair_spec, v16_spec],
        compiler_params=_cparams(("parallel",), vmem),
        name="shared_kv",
    )(x, g.reshape(1, d), wk, wv, wf, bf, gk, gsum, gexp)


def _fox_q_body(x_ref, g_ref, wq_ref, wg_ref, gq_ref, gsum_ref, gexp_ref, q16_ref, sg_ref):
    xn = _rms(x_ref[...], g_ref[...]).astype(BF16)
    q = _head_norm(_dot(xn, wq_ref[...]), gsum_ref, gexp_ref, gq_ref[...], FOX_HEAD_DIM)
    q16_ref[...] = (q * (FOX_HEAD_DIM ** -0.5 * LOG2E)).astype(BF16)
    sg_ref[...] = jax.nn.sigmoid(_dot(xn, wg_ref[...])).astype(sg_ref.dtype)


def _fox_q(x, g, wq, wg, gq, gsum, gexp, *, tm):
    n, d = x.shape
    tm = min(tm, n)
    assert n % tm == 0
    row = lambda i: (i, 0)
    fix = lambda i: (0, 0)
    vmem = 2 * tm * d * (4 + 2 + 4) + 2 * 2 * (2 * d * d + 2 * d * LANES) + 6 * tm * d * 4
    return pl.pallas_call(
        _fox_q_body,
        out_shape=[jax.ShapeDtypeStruct((n, d), BF16), jax.ShapeDtypeStruct((n, d), BF16)],
        grid=(n // tm,),
        in_specs=[
            pl.BlockSpec((tm, d), row), pl.BlockSpec((1, d), fix),
            pl.BlockSpec((d, d), fix), pl.BlockSpec((d, d), fix), pl.BlockSpec((1, d), fix),
            pl.BlockSpec((d, LANES), fix), pl.BlockSpec((LANES, d), fix),
        ],
        out_specs=[pl.BlockSpec((tm, d), row), pl.BlockSpec((tm, d), row)],
        compiler_params=_cparams(("parallel",), vmem),
        name="fox_q",
    )(x, g.reshape(1, d), wq, wg, gq, gsum, gexp)


def _pack_parts(x):
    p1, p2, p3 = (p.astype(F32) for p in _split3(x))
    packed = p1 + pltpu.roll(p2, FOX_HEADS, 1) + pltpu.roll(p3, 2 * FOX_HEADS, 1)
    return packed.astype(BF16)


def _bias_body(lf_ref, tri_ref, sel_ref, one_ref, qb_ref, kb_ref, carry_ref):
    @pl.when(pl.program_id(1) == 0)
    def _():
        carry_ref[...] = jnp.zeros_like(carry_ref)

    lf = lf_ref[...]
    tb = lf.shape[0]
    head_lanes = lax.broadcasted_iota(jnp.int32, lf.shape, 1) < FOX_HEADS
    cp = _dot(tri_ref[...], _pack_parts(jnp.where(head_lanes, lf, 0.0)))
    c = cp + pltpu.roll(cp, LANES - FOX_HEADS, 1) + pltpu.roll(cp, LANES - 2 * FOX_HEADS, 1)
    c = jnp.where(head_lanes, c, 0.0) + carry_ref[...]
    carry_ref[...] = c[tb - 1:tb, :]
    qkb = _dot(_pack_parts(c * LOG2E), sel_ref[...]) + one_ref[...]
    wide = FOX_PAIRS * LANES
    for p in range(FOX_PAIRS):
        qb_ref[p] = qkb[:, p * LANES:(p + 1) * LANES].astype(BF16)
        kb_ref[p] = qkb[:, wide + p * LANES:wide + (p + 1) * LANES].astype(BF16)


def _bias_constants():
    wide = FOX_PAIRS * LANES
    sel = np.zeros((LANES, 2 * wide), np.float32)
    one = np.zeros((1, 2 * wide), np.float32)
    for p in range(FOX_PAIRS):
        for e in range(2):
            base = p * LANES + e * BIAS_LANES_PER_HEAD
            for part in range(BIAS_PARTS):
                row = part * FOX_HEADS + 2 * p + e
                sel[row, base + part] = 1.0
                one[0, base + BIAS_PARTS + part] = 1.0
                one[0, wide + base + part] = 1.0
                sel[row, wide + base + BIAS_PARTS + part] = -1.0
    return jnp.asarray(sel, BF16), jnp.asarray(one)


def _fox_bias(lf, *, tb):
    bsz, kp, _ = lf.shape
    assert kp % tb == 0
    sel, one = _bias_constants()
    tri = jnp.tril(jnp.ones((tb, tb), F32)).astype(BF16)
    wide = FOX_PAIRS * LANES
    out_spec = pl.BlockSpec((None, FOX_PAIRS, tb, LANES), lambda b, i: (b, 0, i, 0))
    vmem = 2 * tb * LANES * 4 + 2 * tb * tb * 2 + 2 * LANES * 2 * wide * 2 \
        + 4 * FOX_PAIRS * tb * LANES * 2 + 4 * tb * 2 * wide * 4
    return pl.pallas_call(
        _bias_body,
        out_shape=[jax.ShapeDtypeStruct((bsz, FOX_PAIRS, kp, LANES), BF16)] * 2,
        grid=(bsz, kp // tb),
        in_specs=[
            pl.BlockSpec((None, tb, LANES), lambda b, i: (b, i, 0)),
            pl.BlockSpec((tb, tb), lambda b, i: (0, 0)),
            pl.BlockSpec((LANES, 2 * wide), lambda b, i: (0, 0)),
            pl.BlockSpec((1, 2 * wide), lambda b, i: (0, 0)),
        ],
        out_specs=[out_spec, out_spec],
        scratch_shapes=[pltpu.VMEM((1, LANES), F32)],
        compiler_params=_cparams(("parallel", "arbitrary"), vmem),
        name="fox_bias",
    )(lf, tri, sel, one)


def _q_aug(q, qb, lane):
    zero = jnp.zeros_like(q)
    out = []
    for e in range(2):
        head_lanes = (lane >> 6) == e
        bias_lanes = jnp.logical_and(lane >= e * BIAS_LANES_PER_HEAD,
                                     lane < (e + 1) * BIAS_LANES_PER_HEAD)
        out.append(jnp.concatenate(
            [jnp.where(head_lanes, q, zero), jnp.where(bias_lanes, qb, zero)], axis=1))
    return out


def _attn_body(q_ref, qb_ref, k_ref, kb_ref, v_ref, sg_ref, o_ref, m_ref, acc_ref,
               *, tq, tk, n_qblocks):
    i = pl.program_id(2)
    lane = lax.broadcasted_iota(jnp.int32, (tq, LANES), 1)
    q_aug = _q_aug(q_ref[...], qb_ref[...], lane)

    m_ref[...] = jnp.full(m_ref.shape, NEG_INF, F32)
    acc_ref[...] = jnp.zeros(acc_ref.shape, F32)

    def qk_scores(j, qwin=slice(None)):
        rows = slice(j * tk, (j + 1) * tk)
        k_aug = jnp.concatenate([k_ref[rows, :], kb_ref[rows, :]], axis=1)
        qs = q_aug if qwin == slice(None) else [qa[qwin] for qa in q_aug]
        return [lax.dot_general(k_aug, qs[e], NT_DIMS, preferred_element_type=F32)
                for e in range(2)]

    def softmax_pv(j, scores, qwin=slice(None)):
        v = jnp.concatenate([v_ref[j], jnp.ones((DENOM_ROWS, tk), BF16)], axis=0)
        if qwin == slice(None):
            at = lambda ref, e: ref.at[e]
        else:
            at = lambda ref, e: ref.at[e, :, qwin]
        for e in range(2):
            s = scores[e]
            m_prev = at(m_ref, e)[...]
            m_new = jnp.maximum(m_prev, jnp.max(s, axis=0, keepdims=True))
            alpha = jnp.exp2(m_prev - m_new)
            p16 = jnp.exp2(s - m_new).astype(BF16)
            at(acc_ref, e)[...] = alpha * at(acc_ref, e)[...] + _dot(v, p16)
            at(m_ref, e)[...] = m_new

    early, late = slice(0, tk), slice(tk, tq)

    def schedule(n_visible):
        ahead = qk_scores(0)
        for j in range(n_visible):
            cur, ahead = ahead, qk_scores(j + 1)
            softmax_pv(j, cur)
        tmask = jnp.where(lax.broadcasted_iota(jnp.int32, (tk, tk), 0)
                          <= lax.broadcasted_iota(jnp.int32, (tk, tk), 1), 0.0, NEG_INF)
        late_scores = qk_scores(n_visible + 1, late)
        softmax_pv(n_visible, [jnp.concatenate([s[:, early] + tmask, s[:, late]], axis=1)
                               for s in ahead])
        softmax_pv(n_visible + 1, [s + tmask for s in late_scores], late)

    lax.switch(i, [functools.partial(schedule, 2 * n) for n in range(n_qblocks)])

    o0 = acc_ref[0, :LANES] * (1.0 / acc_ref[0, LANES:LANES + 1])
    o1 = acc_ref[1, :LANES] * (1.0 / acc_ref[1, LANES:LANES + 1])
    feature = lax.broadcasted_iota(jnp.int32, (LANES, tq), 0)
    o = jnp.where((feature >> 6) == 0, o0, o1).T
    o_ref[...] = (o * sg_ref[...].astype(F32)).astype(BF16)


def _fox_attn(q16, qb, k16, kb, vt16, sg, *, tq, tk):
    bsz, t, d = q16.shape
    assert t % tq == 0 and tq == 2 * tk and k16.shape[2] == t
    qspec = pl.BlockSpec((None, tq, LANES), lambda b, p, i: (b, i, p))
    vmem = 2 * (3 * t * LANES * 2 + tq * LANES * (2 + 2 + 2 + 2)) + 3 * tq * LANES * 4 \
        + 2 * tk * 2 * LANES * 2 + 10 * tq * tk * 4
    return pl.pallas_call(
        functools.partial(_attn_body, tq=tq, tk=tk, n_qblocks=t // tq),
        out_shape=jax.ShapeDtypeStruct((bsz, t, d), BF16),
        grid=(bsz, FOX_PAIRS, t // tq),
        in_specs=[
            qspec,
            pl.BlockSpec((None, None, tq, LANES), lambda b, p, i: (b, p, i, 0)),
            pl.BlockSpec((None, None, t, LANES), lambda b, p, i: (p, b, 0, 0)),
            pl.BlockSpec((None, None, t, LANES), lambda b, p, i: (b, p, 0, 0)),
            pl.BlockSpec((t // tk, LANES, tk), lambda b, p, i: (b, p, 0)),
            qspec,
        ],
        out_specs=qspec,
        scratch_shapes=[pltpu.VMEM((2, 1, tq), F32),
                        pltpu.VMEM((2, LANES + DENOM_ROWS, tq), F32)],
        compiler_params=_cparams(("parallel", "parallel", "arbitrary"), vmem + (8 << 20)),
        name="fox_attn",
    )(q16, qb, k16, kb, vt16, sg)


def _cache_attn_body(q_ref, qb_ref, ck_ref, cv_ref, kn_ref, vn_ref, kb_ref, sg_ref,
                     o_ref, m_ref, l_ref, acc_ref, *, tk):
    t = q_ref.shape[0]
    group, p_len = ck_ref.shape[0], ck_ref.shape[1]
    lane = lax.broadcasted_iota(jnp.int32, (t, LANES), 1)
    m_ref[...] = jnp.full(m_ref.shape, NEG_INF, F32)
    l_ref[...] = jnp.zeros(l_ref.shape, F32)
    acc_ref[...] = jnp.zeros(acc_ref.shape, F32)
    causal = jnp.where(lax.broadcasted_iota(jnp.int32, (t, t), 1)
                       <= lax.broadcasted_iota(jnp.int32, (t, t), 0), 0.0, NEG_INF)

    for g in range(group):
        cols = slice(g * LANES, (g + 1) * LANES)
        q_aug = _q_aug(q_ref[:, cols], qb_ref[g], lane)

        def fold(k, kb, v, mask):
            k_aug = jnp.concatenate([k, kb], axis=1)
            scores = [lax.dot_general(q_aug[e], k_aug, NT_DIMS, preferred_element_type=F32)
                      for e in range(2)]
            for e in range(2):
                s = scores[e] if mask is None else scores[e] + mask
                m_prev = m_ref[g, e]
                m_new = jnp.maximum(m_prev, jnp.max(s, axis=1, keepdims=True))
                alpha = jnp.exp2(m_prev - m_new)
                p = jnp.exp2(s - m_new)
                l_ref[g, e] = alpha * l_ref[g, e] + jnp.sum(p, axis=1, keepdims=True)
                acc_ref[g, e] = alpha * acc_ref[g, e] + _dot(p.astype(BF16), v)
                m_ref[g, e] = m_new

        for j in range(p_len // tk):
            rows = slice(j * tk, (j + 1) * tk)
            fold(ck_ref[g, rows, :], kb_ref[g, rows, :], cv_ref[g, rows, :], None)
        fold(kn_ref[g], kb_ref[g, p_len:p_len + t, :], vn_ref[g], causal)

        o = jnp.where((lane >> 6) == 0, acc_ref[g, 0] * (1.0 / l_ref[g, 0]),
                      acc_ref[g, 1] * (1.0 / l_ref[g, 1]))
        o_ref[:, cols] = (o * sg_ref[:, cols].astype(F32)).astype(BF16)


def _fox_attn_cached(q16, qb, kb, ck16, cv16, k16, v16, sg, *, tk):
    bsz, t, d = q16.shape
    p_len = ck16.shape[2]
    grp = CACHE_PAIR_GROUP
    assert p_len % tk == 0 and p_len % t == 0 and kb.shape[2] == p_len + t and FOX_PAIRS % grp == 0
    qspec = pl.BlockSpec((None, t, grp * LANES), lambda b, p: (b, 0, p))
    cspec = pl.BlockSpec((grp, None, p_len, LANES), lambda b, p: (p, b, 0, 0))
    nspec = pl.BlockSpec((grp, t, LANES), lambda b, p: (p, b, 0))
    vmem = grp * (2 * (3 * (p_len + t) * LANES * 2 + t * LANES * 8) + 6 * t * LANES * 4
                  + 8 * t * tk * 4 + 4 * tk * LANES * 2)
    return pl.pallas_call(
        functools.partial(_cache_attn_body, tk=tk),
        out_shape=jax.ShapeDtypeStruct((bsz, t, d), BF16),
        grid=(bsz, FOX_PAIRS // grp),
        in_specs=[
            qspec,
            pl.BlockSpec((None, grp, t, LANES), lambda b, p: (b, p, p_len // t, 0)),
            cspec, cspec, nspec, nspec,
            pl.BlockSpec((None, grp, p_len + t, LANES), lambda b, p: (b, p, 0, 0)),
            qspec,
        ],
        out_specs=qspec,
        scratch_shapes=[pltpu.VMEM((grp, 2, t, 1), F32), pltpu.VMEM((grp, 2, t, 1), F32),
                        pltpu.VMEM((grp, 2, t, LANES), F32)],
        compiler_params=_cparams(("parallel", "parallel"), vmem + (8 << 20)),
        name="fox_attn_cached",
    )(q16, qb, ck16, cv16, k16, v16, kb, sg)


TM = 1024
KV_TM = 512
FFN_TM = 1024
FFN_TF = 256
REPACK_ROWS = 128
GLA_TB = 512
ATTN_TQ = 1024
ATTN_TK = 512
CACHE_TK = 4096
CACHE_PAIR_GROUP = 4
BIAS_TB_CHOICES = (832, 512, 384, 320, 256, 192, 128, 64)


def _prep_weights(ffn_norm, w_ffn_gu, w_ffn_down, mix_norm, a_w_in, a_w_g2, a_b_g, a_g_out, a_w_o,
                  kv_norm, w_kvf, b_f, g_k, b_w_qg, b_g_q, b_w_o):
    d = w_ffn_gu.shape[2]
    rank = a_w_g2.shape[1]
    qk = a_w_g2.shape[2]
    vw = a_w_o.shape[1]
    w_in = a_w_in[0]
    pad_cols = lambda w: jnp.pad(w, ((0, 0), (0, LANES - w.shape[1])))
    groups = np.arange(d) // FOX_HEAD_DIM
    gsum = (groups[:, None] == np.arange(LANES)[None, :]).astype(np.float32)
    return dict(
        ffn_norm=ffn_norm, mix_norm=mix_norm, kv_norm=kv_norm,
        w_gu=_tile_cols(w_ffn_gu.reshape((-1,) + w_ffn_gu.shape[2:]), FFN_TF, rows=REPACK_ROWS),
        w_down=w_ffn_down.astype(BF16),
        a_wq=w_in[:, :qk].astype(BF16), a_wk=w_in[:, qk:2 * qk].astype(BF16),
        a_wv=w_in[:, 2 * qk:2 * qk + vw].astype(BF16),
        a_wr=w_in[:, 2 * qk + vw:2 * qk + 2 * vw].astype(BF16),
        a_wgl=pad_cols(w_in[:, 2 * qk + 2 * vw:]).astype(BF16),
        a_wg2=jnp.pad(a_w_g2[0], ((0, LANES - rank), (0, 0))).astype(BF16),
        a_bg=a_b_g[0], a_gout=a_g_out[0], a_wo=a_w_o[0].astype(BF16),
        kv_wk=w_kvf[:, :d].astype(BF16), kv_wv=w_kvf[:, d:2 * d].astype(BF16),
        kv_wv_t=w_kvf[:, d:2 * d].T.astype(BF16),
        kv_wf=pad_cols(w_kvf[:, 2 * d:]).astype(BF16),
        kv_bf=jnp.pad(b_f, (0, LANES - b_f.shape[0])).reshape(1, LANES),
        gk=jnp.tile(g_k, FOX_HEADS).reshape(1, d), gq=jnp.tile(b_g_q[0], FOX_HEADS).reshape(1, d),
        b_wq=b_w_qg[0][:, :d].astype(BF16), b_wg=b_w_qg[0][:, d:].astype(BF16),
        b_wo=b_w_o[0].astype(BF16),
        gsum=jnp.asarray(gsum, BF16), gexp=jnp.asarray(gsum.T, BF16),
    )


def _trunk(x, s0, past, w):
    bsz, t, d = x.shape
    n = bsz * t
    h = x.reshape(n, d)
    ffn = lambda h_, layer, half: _ffn(h_, w["ffn_norm"][layer, half], w["w_gu"][2 * layer + half],
                                       w["w_down"][layer, half], tm=FFN_TM)
    h = ffn(h, 0, 0)
    q, k, v, r, la = _gla_in(h, w["mix_norm"][0], w["a_wq"], w["a_wk"], w["a_wv"], w["a_wr"],
                             w["a_wgl"], w["a_wg2"], w["a_bg"], tm=TM)
    s3 = lambda z: z.reshape(bsz, t, z.shape[1])
    chunk = 64
    og, st_fin = _gla(s3(q), s3(k), s3(v), s3(r), s3(la), jnp.swapaxes(s0, 2, 3), w["a_gout"],
                      chunk=chunk, tb=GLA_TB)
    h = _proj_res(h, og.reshape(n, -1), w["a_wo"], tm=TM)
    h = ffn(h, 0, 1)
    tq = min(ATTN_TQ, t)
    tk = min(ATTN_TK, tq // 2)
    k_new, v_new, lf, k16, v16 = _shared_kv(
        h, w["kv_norm"], w["kv_wk"], w["kv_wv_t"] if past is None else w["kv_wv"], w["kv_wf"],
        w["kv_bf"], w["gk"], w["gsum"], w["gexp"], tm=KV_TM, vt_block=tk if past is None else None)
    h = ffn(h, 1, 0)
    q16, sg = _fox_q(h, w["mix_norm"][1], w["b_wq"], w["b_wg"], w["gq"], w["gsum"], w["gexp"], tm=TM)
    pairs4 = lambda z: z.reshape(FOX_PAIRS, bsz, t, LANES)
    if past is None:
        lf_all = s3(lf)
    else:
        past_k, past_v, past_lf = past
        p_len = past_k.shape[1]
        past_pairs = lambda a: jnp.transpose(
            a.astype(BF16).reshape(bsz, p_len, FOX_PAIRS, LANES), (2, 0, 1, 3))
        lf_all = jnp.concatenate(
            [jnp.pad(past_lf, ((0, 0), (0, 0), (0, LANES - past_lf.shape[2]))), s3(lf)], axis=1)
    bias_tb = next(c for c in BIAS_TB_CHOICES if lf_all.shape[1] % c == 0)
    qb, kb = _fox_bias(lf_all, tb=bias_tb)
    if past is None:
        og = _fox_attn(s3(q16), qb, pairs4(k16), kb, v16, s3(sg), tq=tq, tk=tk)
    else:
        og = _fox_attn_cached(s3(q16), qb, kb, past_pairs(past_k), past_pairs(past_v), k16, v16,
                              s3(sg), tk=min(CACHE_TK, p_len))
    h = _proj_res(h, og.reshape(n, d), w["b_wo"], tm=TM)
    h = ffn(h, 1, 1)
    heads4 = lambda z: z.reshape(bsz, t, FOX_HEADS, FOX_HEAD_DIM)
    return (h.reshape(bsz, t, d), jnp.swapaxes(st_fin, 2, 3)[:, None], heads4(k_new), heads4(v_new),
            lf[:, :FOX_HEADS].reshape(bsz, t, FOX_HEADS))


def kernel(x_prompt, x_sample, state_gla, cache_k, cache_v, cache_logf, ffn_norm, w_ffn_gu, w_ffn_down, mix_norm, a_w_in, a_w_g2, a_b_g, a_g_out, a_w_o, kv_norm, w_kvf, b_f, g_k, b_w_qg, b_g_q, b_w_o):
    w = _prep_weights(ffn_norm, w_ffn_gu, w_ffn_down, mix_norm, a_w_in, a_w_g2, a_b_g, a_g_out,
                      a_w_o, kv_norm, w_kvf, b_f, g_k, b_w_qg, b_g_q, b_w_o)
    s0_prompt = jnp.zeros((x_prompt.shape[0],) + state_gla.shape[2:], F32)
    y_p, gla_p, k_p, v_p, lf_p = _trunk(x_prompt, s0_prompt, None, w)
    y_s, gla_s, k_s, v_s, lf_s = _trunk(x_sample, state_gla[:, 0], (cache_k, cache_v, cache_logf), w)
    return (y_p, y_s, gla_p, gla_s, k_p, v_p, lf_p, k_s, v_s, lf_s)
```
